```python
import functools
import jax, jax.numpy as jnp
from jax import lax
import numpy as np

D_MODEL = 1024
BATCH = 2
SEQ = 8192
DEPTH = 1
DEC_BATCH = 128
DEC_SEQ = 8
PAST_LEN = 2048
PAGE_SIZE = 128

HEAD_DIM = 64
HPG = 8
ATT_GROUPS = ((128, 1), (512, 4), (2048, 16))
N_ATT_GROUPS = 3
N_ATT_HEADS = N_ATT_GROUPS * HPG
ATT_WIDTH = HPG * HEAD_DIM
RWKV_HEADS = D_MODEL // 128
RWKV_WIDTH = RWKV_HEADS * HEAD_DIM
DECAY_LORA = 64
ICLR_LORA = 64
QBLOCK = 128
RMS_EPS = 1e-6
GN_EPS = 64e-5
L2_EPS = 1e-12

QKV_COLS = N_ATT_HEADS * HEAD_DIM
SHIFT_COLS = 3 * RWKV_WIDTH + DECAY_LORA + ICLR_LORA
IN_COLS = 3 * QKV_COLS + SHIFT_COLS + ATT_WIDTH + RWKV_WIDTH + 2 * D_MODEL
IN_SPLITS = (QKV_COLS, 2 * QKV_COLS, 3 * QKV_COLS, 3 * QKV_COLS + SHIFT_COLS,
             3 * QKV_COLS + SHIFT_COLS + ATT_WIDTH,
             3 * QKV_COLS + SHIFT_COLS + ATT_WIDTH + RWKV_WIDTH,
             3 * QKV_COLS + SHIFT_COLS + ATT_WIDTH + RWKV_WIDTH + D_MODEL)
SHIFT_SPLITS = (RWKV_WIDTH, 2 * RWKV_WIDTH, 3 * RWKV_WIDTH, 3 * RWKV_WIDTH + DECAY_LORA)

kernel_name = "dilated_swa_rwkv7_gated_hybrid_step"


def _rms(x, g, eps):
    xf = x.astype(jnp.float32)
    y = xf * lax.rsqrt(jnp.mean(xf * xf, axis=-1, keepdims=True) + eps) * g.astype(jnp.float32)
    return y.astype(x.dtype)


def _alibi_slopes():
    h = jnp.arange(1, N_ATT_HEADS + 1, dtype=jnp.float32)
    return jnp.exp2(-8.0 * h / N_ATT_HEADS).reshape(N_ATT_GROUPS, HPG)


def _dilated_band(q, k, v, slopes, window, dil):
    B, S, H, E = q.shape
    span = window // dil
    L = S // dil
    nb = -(-L // QBLOCK)
    Lp = nb * QBLOCK
    n_prev = -(-span // QBLOCK)

    def to_blocks(t):
        t = t.reshape(B, L, dil, H, E).transpose(0, 2, 3, 1, 4)
        t = jnp.pad(t, ((0, 0), (0, 0), (0, 0), (0, Lp - L), (0, 0)))
        return t.reshape(B, dil, H, nb, QBLOCK, E)

    def band(t):
        tp = jnp.pad(t, ((0, 0), (0, 0), (0, 0), (n_prev, 0), (0, 0), (0, 0)))
        return jnp.concatenate([tp[:, :, :, i:i + nb] for i in range(n_prev + 1)], axis=4)

    qb = to_blocks(q)
    kband = band(to_blocks(k))
    vband = band(to_blocks(v))
    s = jnp.einsum('bdhnqe,bdhnke->bdhnqk', qb, kband,
                   preferred_element_type=jnp.float32) / np.float32(np.sqrt(E))
    qi = jnp.arange(QBLOCK)[:, None]
    ki = jnp.arange((n_prev + 1) * QBLOCK)[None, :]
    rel = qi + n_prev * QBLOCK - ki
    key_idx = jnp.arange(nb)[:, None, None] * QBLOCK + ki[None] - n_prev * QBLOCK
    valid = (rel >= 0)[None] & (rel <= span)[None] & (key_idx >= 0)
    dist = (rel * dil).astype(jnp.float32)
    s = s - slopes.astype(jnp.float32)[None, None, :, None, None, None] * dist
    s = jnp.where(valid[None, None, None], s, -jnp.inf)
    m = jnp.max(s, axis=-1, keepdims=True)
    p = jnp.exp(s - m)
    l = jnp.sum(p, axis=-1, keepdims=True)
    acc = jnp.einsum('bdhnqk,bdhnke->bdhnqe', p, vband.astype(jnp.float32))

    def from_blocks(t):
        last = t.shape[-1]
        t = t.reshape(B, dil, H, Lp, last)[:, :, :, :L]
        return t.transpose(0, 3, 1, 2, 4).reshape(B, S, H, last)

    return from_blocks(acc), from_blocks(m), from_blocks(l)


def _dilated_gather(q, k_new, v_new, k_buf, v_buf, slopes, window, dil):
    B, T, H, E = q.shape
    Wb = k_buf.shape[1]
    span = window // dil
    J = span + 1
    kc = jnp.concatenate([k_buf.astype(k_new.dtype), k_new], axis=1)
    vc = jnp.concatenate([v_buf.astype(v_new.dtype), v_new], axis=1)
    idx = Wb + jnp.arange(T)[:, None] - jnp.arange(J)[None, :] * dil
    valid = idx >= 0
    idx_c = jnp.maximum(idx, 0)
    kg = kc[:, idx_c]
    vg = vc[:, idx_c]
    s = jnp.einsum('bthe,btjhe->bthj', q, kg,
                   preferred_element_type=jnp.float32) / np.float32(np.sqrt(E))
    dist = (jnp.arange(J) * dil).astype(jnp.float32)
    s = s - slopes.astype(jnp.float32)[None, None, :, None] * dist[None, None, None, :]
    s = jnp.where(valid[None, :, None, :], s, -jnp.inf)
    m = jnp.max(s, axis=-1, keepdims=True)
    p = jnp.exp(s - m)
    l = jnp.sum(p, axis=-1, keepdims=True)
    acc = jnp.einsum('bthj,btjhe->bthe', p, vg.astype(jnp.float32))
    return acc, m, l, kc[:, -Wb:], vc[:, -Wb:]


def _combine(parts):
    m_all = functools.reduce(jnp.maximum, [m for _, m, _ in parts])
    num = sum(jnp.exp(m - m_all) * acc for acc, m, _ in parts)
    den = sum(jnp.exp(m - m_all) * l for _, m, l in parts)
    return num / den


def _attend_prompt(q, k, v, slopes):
    S = q.shape[1]
    parts, new = [], []
    for g, (window, dil) in enumerate(ATT_GROUPS):
        hs = slice(g * HPG, (g + 1) * HPG)
        parts.append(_dilated_band(q[:, :, hs], k[:, :, hs], v[:, :, hs], slopes[g], window, dil))
        keep = min(window, S)
        new += [k[:, S - keep:, hs], v[:, S - keep:, hs]]
    return _combine(parts), tuple(new)


def _attend_sample(q, k, v, slopes, bufs):
    parts, new = [], []
    for g, (window, dil) in enumerate(ATT_GROUPS):
        hs = slice(g * HPG, (g + 1) * HPG)
        acc, m, l, kb, vb = _dilated_gather(q[:, :, hs], k[:, :, hs], v[:, :, hs],
                                            bufs[2 * g], bufs[2 * g + 1], slopes[g], window, dil)
        parts.append((acc, m, l))
        new += [kb, vb]
    return _combine(parts), tuple(new)


def _rwkv_scan(state, r, w, k, v, kk, a):
    def step(S, inp):
        r_t, w_t, k_t, v_t, kk_t, a_t = inp
        sa = jnp.einsum('bhvk,bhk->bhv', S, -kk_t)
        S = (S * w_t[:, :, None, :] + sa[..., None] * (kk_t * a_t)[:, :, None, :]
             + v_t[..., None] * k_t[:, :, None, :])
        return S, jnp.einsum('bhvk,bhk->bhv', S, r_t)
    xs = tuple(jnp.moveaxis(t, 1, 0) for t in (r, w, k, v, kk, a))
    S, ys = lax.scan(step, state, xs)
    return jnp.moveaxis(ys, 0, 1), S


def _layer(x, shift_prev, rwkv_state, attend, lw):
    (norm_g, w_in, shift_mu, q_norm_g, k_norm_g, w0, w2, a0, a2,
     k_k, k_a, r_k, ln_x_w, ln_x_b, p_a, p_b, w_out) = lw
    B, T, _ = x.shape
    f32 = jnp.float32
    xn = _rms(x, norm_g, RMS_EPS)
    proj = jnp.einsum('btd,dc->btc', xn, w_in)
    q, k, v, sh, z_a, z_b, g_a, g_b = jnp.split(proj, IN_SPLITS, axis=-1)

    qh = _rms(q.reshape(B, T, N_ATT_HEADS, HEAD_DIM), q_norm_g, RMS_EPS)
    kh = _rms(k.reshape(B, T, N_ATT_HEADS, HEAD_DIM), k_norm_g, RMS_EPS)
    vh = v.reshape(B, T, N_ATT_HEADS, HEAD_DIM)
    o_a, attn_state = attend(qh, kh, vh)
    o_a = o_a.reshape(B, T, ATT_WIDTH).astype(x.dtype)

    prev = jnp.concatenate([shift_prev[:, None].astype(sh.dtype), sh[:, :-1]], axis=1)
    xs = sh + (prev - sh) * shift_mu
    r, kr, vr, wd, ad = jnp.split(xs, SHIFT_SPLITS, axis=-1)
    w_log = -jax.nn.softplus(-(w0 + jnp.tanh(wd) @ w2).astype(f32)) - 0.5
    decay = jnp.exp(-jnp.exp(w_log))
    a = jax.nn.sigmoid((a0 + ad @ a2).astype(f32))

    def heads(t):
        return t.astype(f32).reshape(B, T, RWKV_HEADS, HEAD_DIM)

    r_h, k_h, v_h, w_h, a_h = heads(r), heads(kr), heads(vr), heads(decay), heads(a)
    kk = k_h * k_k.astype(f32).reshape(RWKV_HEADS, HEAD_DIM)
    kk = kk / jnp.maximum(jnp.sqrt(jnp.sum(kk * kk, axis=-1, keepdims=True)), L2_EPS)
    k_h = k_h * (1.0 + (a_h - 1.0) * k_a.astype(f32).reshape(RWKV_HEADS, HEAD_DIM))
    y_b, new_rwkv = _rwkv_scan(rwkv_state.astype(f32), r_h, w_h, k_h, v_h, kk, a_h)
    mu = jnp.mean(y_b, axis=-1, keepdims=True)
    var = jnp.mean(jnp.square(y_b - mu), axis=-1, keepdims=True)
    y_b = ((y_b - mu) * lax.rsqrt(var + GN_EPS) * ln_x_w.astype(f32).reshape(RWKV_HEADS, HEAD_DIM)
           + ln_x_b.astype(f32).reshape(RWKV_HEADS, HEAD_DIM))
    bonus = jnp.sum(r_h * k_h * r_k.astype(f32), axis=-1, keepdims=True) * v_h
    o_b = (y_b + bonus).reshape(B, T, RWKV_WIDTH).astype(x.dtype)

    h_a = jnp.einsum('btc,cd->btd', o_a * jax.nn.silu(z_a), p_a)
    h_b = jnp.einsum('btc,cd->btd', o_b * jax.nn.silu(z_b), p_b)
    mixed = jax.nn.sigmoid(g_a) * h_a + jax.nn.sigmoid(g_b) * h_b
    y = x + jnp.einsum('btd,de->bte', mixed, w_out)
    return y, (*attn_state, new_rwkv, sh[:, -1])


def setup_inputs(seed: int = 0) -> dict:
    key = jax.random.key(seed)
    ks = jax.random.split(key, 32)
    f32 = jnp.float32

    def nrm(k, shape, s):
        return s * jax.random.normal(k, shape, f32)

    inp = {}
    inp["x_prompt"] = nrm(ks[0], (BATCH, SEQ, D_MODEL), 1.0)
    inp["x_sample"] = nrm(ks[1], (DEC_BATCH, DEC_SEQ, D_MODEL), 1.0)
    for g, (window, _) in enumerate(ATT_GROUPS):
        wb = min(window, PAST_LEN)
        inp[f"cache_k_g{g}"] = nrm(ks[2 + 2 * g], (DEPTH, DEC_BATCH, wb, HPG, HEAD_DIM), 1.0)
        inp[f"cache_v_g{g}"] = nrm(ks[3 + 2 * g], (DEPTH, DEC_BATCH, wb, HPG, HEAD_DIM), 1.0)
    inp["state_rwkv"] = nrm(ks[8], (DEPTH, DEC_BATCH, RWKV_HEADS, HEAD_DIM, HEAD_DIM), 0.5)
    inp["state_shift"] = nrm(ks[9], (DEPTH, DEC_BATCH, SHIFT_COLS), 1.0)
    inp["norm_g"] = 1.0 + nrm(ks[10], (DEPTH, D_MODEL), 0.02)
    inp["w_in"] = nrm(ks[11], (DEPTH, D_MODEL, IN_COLS), D_MODEL ** -0.5)
    inp["shift_mu"] = jax.random.uniform(ks[12], (DEPTH, SHIFT_COLS), f32)
    inp["q_norm_g"] = 1.0 + nrm(ks[13], (DEPTH, HEAD_DIM), 0.02)
    inp["k_norm_g"] = 1.0 + nrm(ks[14], (DEPTH, HEAD_DIM), 0.02)
    inp["w0"] = jax.random.uniform(ks[15], (DEPTH, RWKV_WIDTH), f32, -6.0, -1.0)
    inp["w2"] = nrm(ks[16], (DEPTH, DECAY_LORA, RWKV_WIDTH), 0.5 * DECAY_LORA ** -0.5)
    inp["a0"] = nrm(ks[17], (DEPTH, RWKV_WIDTH), 0.1)
    inp["a2"] = nrm(ks[18], (DEPTH, ICLR_LORA, RWKV_WIDTH), 0.5 * ICLR_LORA ** -0.5)
    inp["k_k"] = 0.85 + nrm(ks[19], (DEPTH, RWKV_WIDTH), 0.05)
    inp["k_a"] = 1.0 + nrm(ks[20], (DEPTH, RWKV_WIDTH), 0.05)
    inp["r_k"] = nrm(ks[21], (DEPTH, RWKV_HEADS, HEAD_DIM), 0.1)
    inp["ln_x_w"] = 1.0 + nrm(ks[22], (DEPTH, RWKV_WIDTH), 0.02)
    inp["ln_x_b"] = nrm(ks[23], (DEPTH, RWKV_WIDTH), 0.02)
    inp["p_a"] = nrm(ks[24], (DEPTH, ATT_WIDTH, D_MODEL), ATT_WIDTH ** -0.5)
    inp["p_b"] = nrm(ks[25], (DEPTH, RWKV_WIDTH, D_MODEL), RWKV_WIDTH ** -0.5)
    inp["w_out"] = nrm(ks[26], (DEPTH, D_MODEL, D_MODEL), D_MODEL ** -0.5)
    return inp


def reference(x_prompt, x_sample, cache_k_g0, cache_v_g0, cache_k_g1, cache_v_g1,
              cache_k_g2, cache_v_g2, state_rwkv, state_shift, norm_g, w_in, shift_mu,
              q_norm_g, k_norm_g, w0, w2, a0, a2, k_k, k_a, r_k, ln_x_w, ln_x_b,
              p_a, p_b, w_out):
    slopes = _alibi_slopes()
    h_p, h_s = x_prompt, x_sample
    prompt_states, sample_states = [], []
    for layer in range(DEPTH):
        lw = (norm_g[layer], w_in[layer], shift_mu[layer], q_norm_g[layer], k_norm_g[layer],
              w0[layer], w2[layer], a0[layer], a2[layer], k_k[layer], k_a[layer], r_k[layer],
              ln_x_w[layer], ln_x_b[layer], p_a[layer], p_b[layer], w_out[layer])
        zero_shift = jnp.zeros((h_p.shape[0], SHIFT_COLS), h_p.dtype)
        zero_state = jnp.zeros((h_p.shape[0], RWKV_HEADS, HEAD_DIM, HEAD_DIM), jnp.float32)
        h_p, st_p = _layer(h_p, zero_shift, zero_state,
                           functools.partial(_attend_prompt, slopes=slopes), lw)
        bufs = (cache_k_g0[layer], cache_v_g0[layer], cache_k_g1[layer], cache_v_g1[layer],
                cache_k_g2[layer], cache_v_g2[layer])
        h_s, st_s = _layer(h_s, state_shift[layer], state_rwkv[layer],
                           functools.partial(_attend_sample, slopes=slopes, bufs=bufs), lw)
        prompt_states.append(st_p)
        sample_states.append(st_s)
    ps = [jnp.stack(t) for t in zip(*prompt_states)]
    ss = [jnp.stack(t) for t in zip(*sample_states)]
    return (h_p, h_s, ps[0], ps[1], ps[2], ps[3], ps[4], ps[5], ps[6], ps[7],
            ss[0], ss[1], ss[2], ss[3], ss[4], ss[5], ss[6], ss[7])
```

```python
import functools

import jax
import jax.numpy as jnp
from jax import lax
from jax.experimental import pallas as pl
from jax.experimental.pallas import tpu as pltpu

F32 = jnp.float32
BF16 = jnp.bfloat16

HEAD_DIM = 64
HPG = 8
ATT_GROUPS = ((128, 1), (512, 4), (2048, 16))
N_ATT_HEADS = 24
RWKV_HEADS = 8
LORA = 64
QBLOCK = 128
RMS_EPS = 1e-6
GN_EPS = 64e-5
L2_EPS = 1e-12
CHUNK = 64

LANES = 128
NEW_PAD = 16
LN2 = 0.6931471805599453
VMEM_LIMIT = 52 * 1024 * 1024

QKV = N_ATT_HEADS * HEAD_DIM
ATT_W = HPG * HEAD_DIM
RW_W = RWKV_HEADS * HEAD_DIM
C_GATES = 0
C_Q = 3072
C_K = C_Q + QKV
C_V = C_K + QKV
C_SH = C_V + QKV
SHIFT_COLS = 3 * RW_W + 2 * LORA
IN_COLS = C_SH + SHIFT_COLS
TN = 512
P_COLS = pl.cdiv(IN_COLS, TN) * TN


def _cparams(sem):
    return pltpu.CompilerParams(dimension_semantics=sem, vmem_limit_bytes=VMEM_LIMIT)


def _segsum(x, bd):
    hi = x.astype(BF16)
    lo = (x - hi.astype(F32)).astype(BF16)
    return (jnp.dot(hi, bd, preferred_element_type=F32)
            + jnp.dot(lo, bd, preferred_element_type=F32))


def _bd_ones():
    i = jnp.arange(LANES)
    return (i[:, None] // HEAD_DIM == i[None, :] // HEAD_DIM).astype(BF16)


def _inproj_kernel(x_ref, g_ref, w_ref, qg_ref, kg_ref, bd_ref, o_ref, xn_ref):
    j = pl.program_id(1)

    @pl.when(j == 0)
    def _():
        x = x_ref[...]
        ms = jnp.mean(x * x, axis=-1, keepdims=True)
        xn_ref[...] = (x * lax.rsqrt(ms + RMS_EPS) * g_ref[...]).astype(BF16)

    t = jnp.dot(xn_ref[...], w_ref[...], preferred_element_type=F32)
    q_lo, k_lo, k_hi = C_Q // TN, C_K // TN, C_V // TN

    @pl.when(jnp.logical_or(j < q_lo, j >= k_hi))
    def _():
        o_ref[...] = t

    @pl.when(jnp.logical_and(j >= q_lo, j < k_hi))
    def _():
        gain = jnp.where(j < k_lo, qg_ref[...], kg_ref[...])
        bd = bd_ref[...]
        for c in range(TN // LANES):
            tc = t[:, c * LANES:(c + 1) * LANES]
            ms = _segsum(tc * tc, bd) * (1.0 / HEAD_DIM)
            o_ref[:, c * LANES:(c + 1) * LANES] = tc * lax.rsqrt(ms + RMS_EPS) * gain


def _inproj(x, norm_g, w_perm, qg, kg, bd, tm):
    n, d = x.shape
    return pl.pallas_call(
        _inproj_kernel,
        grid=(n // tm, P_COLS // TN),
        in_specs=[
            pl.BlockSpec((tm, d), lambda i, j: (i, 0)),
            pl.BlockSpec((1, d), lambda i, j: (0, 0)),
            pl.BlockSpec((d, TN), lambda i, j: (0, j)),
            pl.BlockSpec((1, LANES), lambda i, j: (0, 0)),
            pl.BlockSpec((1, LANES), lambda i, j: (0, 0)),
            pl.BlockSpec((LANES, LANES), lambda i, j: (0, 0)),
        ],
        out_specs=pl.BlockSpec((tm, TN), lambda i, j: (i, j)),
        out_shape=jax.ShapeDtypeStruct((n, P_COLS), F32),
        scratch_shapes=[pltpu.VMEM((tm, d), BF16)],
        compiler_params=_cparams(("arbitrary", "arbitrary")),
        name="inproj",
    )(x, norm_g, w_perm, qg, kg, bd)


def _band_kernel(q_ref, kp_ref, kc_ref, vp_ref, vc_ref, acc_ref, m_ref, l_ref, *, dil, slopes, tq):
    n = pl.program_id(2)
    lane = lax.broadcasted_iota(jnp.int32, (1, LANES), 1)
    mask_a = lane < HEAD_DIM
    qi = lax.broadcasted_iota(jnp.int32, (QBLOCK, 2 * QBLOCK), 0)
    ki = lax.broadcasted_iota(jnp.int32, (QBLOCK, 2 * QBLOCK), 1)
    rel = qi + QBLOCK - ki
    band = jnp.logical_and(rel >= 0, rel <= QBLOCK)
    dist = (rel * dil).astype(F32)

    for s in range(tq // QBLOCK):
        first = jnp.logical_and(n == 0, s == 0) if s == 0 else False
        valid = jnp.logical_and(band, jnp.logical_or(ki >= QBLOCK, jnp.logical_not(first)))
        for p in range(ATT_W // LANES):
            cols = slice(p * LANES, (p + 1) * LANES)
            rows = slice(s * QBLOCK, (s + 1) * QBLOCK)
            q = q_ref[rows, cols]
            if s == 0:
                k_prev, v_prev = kp_ref[:, cols], vp_ref[:, cols]
            else:
                prev = slice((s - 1) * QBLOCK, s * QBLOCK)
                k_prev, v_prev = kc_ref[prev, cols], vc_ref[prev, cols]
            k = jnp.concatenate([k_prev, kc_ref[rows, cols]], axis=0).astype(BF16)
            v = jnp.concatenate([v_prev, vc_ref[rows, cols]], axis=0).astype(BF16)
            outs = []
            for h in range(2):
                hm = mask_a if h == 0 else jnp.logical_not(mask_a)
                qh = jnp.where(hm, q, 0.0).astype(BF16)
                sc = lax.dot_general(qh, k, (((1,), (1,)), ((), ())), preferred_element_type=F32)
                sc = sc * (1.0 / 8.0) - slopes[2 * p + h] * dist
                sc = jnp.where(valid, sc, -jnp.inf)
                m = jnp.max(sc, axis=-1, keepdims=True)
                pr = jnp.exp(sc - m)
                l = jnp.sum(pr, axis=-1, keepdims=True)
                acc = jnp.dot(pr.astype(BF16), v, preferred_element_type=F32)
                outs.append((acc, m, l))
            (acc0, m0, l0), (acc1, m1, l1) = outs
            acc_ref[rows, cols] = jnp.where(mask_a, acc0, acc1)
            m_ref[rows, cols] = jnp.where(mask_a, m0, m1)
            l_ref[rows, cols] = jnp.where(mask_a, l0, l1)


def _band_attention(proj, batch, seq, g, tq):
    window, dil = ATT_GROUPS[g]
    assert window // dil == QBLOCK
    cls_len = seq // dil
    tq = min(tq, cls_len)
    pv = proj.reshape(batch, cls_len, dil * P_COLS)
    nblk = P_COLS // ATT_W
    sub = tq // QBLOCK
    slopes = tuple(2.0 ** (-8.0 * (g * HPG + h + 1) / N_ATT_HEADS) for h in range(HPG))
    qb, kb, vb = C_Q // ATT_W + g, C_K // ATT_W + g, C_V // ATT_W + g

    def cur(blk):
        return pl.BlockSpec((None, tq, ATT_W), lambda b, r, n: (b, n, r * nblk + blk))

    def prev(blk):
        return pl.BlockSpec((None, QBLOCK, ATT_W),
                            lambda b, r, n: (b, jnp.maximum(n * sub - 1, 0), r * nblk + blk))

    out_spec = pl.BlockSpec((None, tq, ATT_W), lambda b, r, n: (b, n, r))
    out_sds = jax.ShapeDtypeStruct((batch, cls_len, dil * ATT_W), F32)
    acc, m, l = pl.pallas_call(
        functools.partial(_band_kernel, dil=dil, slopes=slopes, tq=tq),
        grid=(batch, dil, cls_len // tq),
        in_specs=[cur(qb), prev(kb), cur(kb), prev(vb), cur(vb)],
        out_specs=[out_spec, out_spec, out_spec],
        out_shape=[out_sds, out_sds, out_sds],
        compiler_params=_cparams(("arbitrary", "arbitrary", "arbitrary")),
        name=f"band_attn_g{g}",
    )(pv, pv, pv, pv, pv)
    shape = (batch * seq, ATT_W)
    return acc.reshape(shape), m.reshape(shape), l.reshape(shape)


def _decode_kernel(q_ref, kn_ref, vn_ref, kb_ref, vb_ref, acc_ref, m_ref, l_ref, ko_ref, vo_ref,
                   *, g, dil, wb, t_new, n_split):
    span = wb // dil
    hrow = lax.broadcasted_iota(jnp.int32, (HPG, ATT_W), 0)
    hlane = lax.broadcasted_iota(jnp.int32, (HPG, ATT_W), 1) // HEAD_DIM
    seg = (hrow == hlane).astype(F32)
    hidx = (lax.broadcasted_iota(jnp.int32, (HPG, 1), 0) + (g * HPG + 1)).astype(F32)
    slope = jnp.exp(hidx * (-8.0 * LN2 / N_ATT_HEADS))

    def shift_copy(rs):
        for src, dst, new in ((kb_ref, ko_ref, kn_ref), (vb_ref, vo_ref, vn_ref)):
            if rs % 8 == 0:
                dst[pl.ds(0, span - rs), :] = src[pl.ds(rs, span - rs), :]
            else:
                dst[...] = pltpu.roll(src[...], span - rs, 0)
            for t in range(t_new):
                row = span - rs + (t // dil if n_split == 1 else 0)
                col = (t % dil) * ATT_W
                dst[pl.ds(row, 1), pl.ds(col, ATT_W)] = new[pl.ds(t, 1), :]

    def attend():
        pad = jnp.zeros((NEW_PAD - t_new, ATT_W), F32)
        kn = jnp.concatenate([kn_ref[...], pad], axis=0).astype(BF16)
        vn = jnp.concatenate([vn_ref[...], pad], axis=0).astype(BF16)
        mi = lax.broadcasted_iota(jnp.int32, (HPG, span), 1)
        tj = lax.broadcasted_iota(jnp.int32, (HPG, NEW_PAD), 1)
        for t in range(t_new):
            cls = t % dil
            cols = pl.ds(cls * ATT_W, ATT_W)
            k_cls = kb_ref[:, cols].astype(BF16)
            v_cls = vb_ref[:, cols].astype(BF16)
            qm = (q_ref[pl.ds(t, 1), :] * seg).astype(BF16)
            s_buf = lax.dot_general(qm, k_cls, (((1,), (1,)), ((), ())),
                                    preferred_element_type=F32) * (1.0 / 8.0)
            rho = cls + dil * mi
            s_buf = s_buf - slope * (wb + t - rho).astype(F32)
            s_buf = jnp.where(rho >= t, s_buf, -jnp.inf)
            s_new = lax.dot_general(qm, kn, (((1,), (1,)), ((), ())),
                                    preferred_element_type=F32) * (1.0 / 8.0)
            ok_new = tj <= t
            if dil > 1:
                ok_new = jnp.logical_and(ok_new, ((t + dil * NEW_PAD - tj) % dil) == 0)
            s_new = s_new - slope * (t - tj).astype(F32)
            s_new = jnp.where(ok_new, s_new, -jnp.inf)
            m = jnp.maximum(jnp.max(s_buf, axis=-1, keepdims=True),
                            jnp.max(s_new, axis=-1, keepdims=True))
            p_buf = jnp.exp(s_buf - m)
            p_new = jnp.exp(s_new - m)
            l = jnp.sum(p_buf, axis=-1, keepdims=True) + jnp.sum(p_new, axis=-1, keepdims=True)
            acc = (jnp.dot(p_buf.astype(BF16), v_cls, preferred_element_type=F32)
                   + jnp.dot(p_new.astype(BF16), vn, preferred_element_type=F32))
            acc_ref[pl.ds(t, 1), :] = jnp.sum(acc * seg, axis=0, keepdims=True)
            m_ref[pl.ds(t, 1), :] = jnp.sum(jnp.where(seg > 0, m, 0.0), axis=0, keepdims=True)
            l_ref[pl.ds(t, 1), :] = jnp.sum(l * seg, axis=0, keepdims=True)

    if n_split == 1:
        shift_copy(t_new // dil)
        attend()
    else:
        ch = pl.program_id(1)

        @pl.when(ch == 0)
        def _():
            shift_copy(1)
            attend()

        @pl.when(ch == 1)
        def _():
            ko_ref[...] = kb_ref[...]
            vo_ref[...] = vb_ref[...]


def _decode_attention(proj, k_cache, v_cache, g, dec_batch, t_new):
    window, dil = ATT_GROUPS[g]
    wb = k_cache.shape[1]
    assert wb == window and wb % dil == 0
    assert (dil <= t_new and t_new % dil == 0) or dil == 2 * t_new
    n_split = 1 if dil <= t_new else 2
    span = wb // dil
    cw = dil * ATT_W // n_split
    pv = proj.reshape(dec_batch, t_new, P_COLS)
    kview = k_cache.reshape(dec_batch, span, dil * ATT_W)
    vview = v_cache.reshape(dec_batch, span, dil * ATT_W)

    def new_spec(col0):
        return pl.BlockSpec((None, t_new, ATT_W), lambda b, c: (b, 0, col0 // ATT_W + g))

    cache_in = pl.BlockSpec((None, span, cw), lambda b, c: (b, 0, c))
    cache_out = pl.BlockSpec((None, span, cw), lambda b, c: (b, 0, n_split - 1 - c))
    part = pl.BlockSpec((None, t_new, ATT_W), lambda b, c: (b, 0, 0))
    part_sds = jax.ShapeDtypeStruct((dec_batch, t_new, ATT_W), F32)
    view_sds = jax.ShapeDtypeStruct(kview.shape, F32)
    acc, m, l, ko, vo = pl.pallas_call(
        functools.partial(_decode_kernel, g=g, dil=dil, wb=wb, t_new=t_new, n_split=n_split),
        grid=(dec_batch, n_split),
        in_specs=[new_spec(C_Q), new_spec(C_K), new_spec(C_V), cache_in, cache_in],
        out_specs=[part, part, part, cache_out, cache_out],
        out_shape=[part_sds, part_sds, part_sds, view_sds, view_sds],
        compiler_params=_cparams(("arbitrary", "arbitrary")),
        name=f"decode_attn_g{g}",
    )(pv, pv, pv, kview, vview)
    flat = (dec_batch * t_new, ATT_W)
    return ([acc.reshape(flat), m.reshape(flat), l.reshape(flat)],
            ko.reshape(k_cache.shape), vo.reshape(v_cache.shape))


def _prep_kernel(cur_ref, wa_ref, pcur_ref, pwa_ref, fcur_ref, fwa_ref, mu_ref, muwa_ref, lora_ref,
                 w0_ref, a0_ref, kk_ref, ka_ref, rk_ref, bd_ref,
                 r_o, lw_o, k_o, v_o, kkn_o, b_o, bonus_o, *, seq_len, tm):
    i = pl.program_id(0)
    row = lax.broadcasted_iota(jnp.int32, (tm, 1), 0)

    def shifted(cur, p8, first):
        rolled = pltpu.roll(cur, 1, 0)
        if seq_len >= tm:
            at_start = (i * tm) % seq_len == 0
            edge = jnp.where(at_start, first, p8[7:8, :])
            return jnp.where(row == 0, edge, rolled)
        return jnp.where(row % seq_len == 0, first, rolled)

    cur = cur_ref[...]
    wa = wa_ref[...]
    xs = cur + (shifted(cur, pcur_ref[...], fcur_ref[...]) - cur) * mu_ref[...]
    xwa = wa + (shifted(wa, pwa_ref[...], fwa_ref[...]) - wa) * muwa_ref[...]
    r = xs[:, 0:RW_W]
    kr = xs[:, RW_W:2 * RW_W]
    vr = xs[:, 2 * RW_W:3 * RW_W]
    lane = lax.broadcasted_iota(jnp.int32, (1, LANES), 1)
    lin = jnp.where(lane < LORA, jnp.tanh(xwa), xwa).astype(BF16)
    lo = jnp.dot(lin, lora_ref[...], preferred_element_type=F32)
    z = -(w0_ref[...] + lo[:, 0:RW_W])
    softplus = jnp.maximum(z, 0.0) + jnp.log1p(jnp.exp(-jnp.abs(z)))
    lw = -jnp.exp(-softplus - 0.5)
    a = jax.nn.sigmoid(a0_ref[...] + lo[:, RW_W:2 * RW_W])
    kk = kr * kk_ref[...]
    k_mod = kr * (1.0 + (a - 1.0) * ka_ref[...])
    rkk = r * k_mod * rk_ref[...]
    bd = bd_ref[...]
    r_o[...] = r
    lw_o[...] = lw
    k_o[...] = k_mod
    v_o[...] = vr
    for c in range(RW_W // LANES):
        cols = slice(c * LANES, (c + 1) * LANES)
        kc = kk[:, cols]
        nrm = jnp.maximum(jnp.sqrt(_segsum(kc * kc, bd)), L2_EPS)
        kkn = kc / nrm
        kkn_o[:, cols] = kkn
        b_o[:, cols] = kkn * a[:, cols]
        bonus_o[:, cols] = _segsum(rkk[:, cols], bd) * vr[:, cols]


def _rwkv_prep(proj, first, mu, lora_w, w0, a0, k_k, k_a, r_k, bd, seq_len, tm):
    n = proj.shape[0]
    main_w = 3 * RW_W
    cb, wb = C_SH // main_w, (C_SH + main_w) // LANES
    per_seq_first = seq_len >= tm
    if per_seq_first:
        f_main = first[:, None, 0:main_w]
        f_wa = first[:, None, main_w:]
        fm_spec = pl.BlockSpec((None, 1, main_w), lambda i: ((i * tm) // seq_len, 0, 0))
        fw_spec = pl.BlockSpec((None, 1, LANES), lambda i: ((i * tm) // seq_len, 0, 0))
    else:
        rep = jnp.repeat(first, seq_len, axis=0)
        f_main, f_wa = rep[:, 0:main_w], rep[:, main_w:]
        fm_spec = pl.BlockSpec((tm, main_w), lambda i: (i, 0))
        fw_spec = pl.BlockSpec((tm, LANES), lambda i: (i, 0))
    vec = lambda w: pl.BlockSpec((1, w), lambda i: (0, 0))
    out_spec = pl.BlockSpec((tm, RW_W), lambda i: (i, 0))
    out_sds = jax.ShapeDtypeStruct((n, RW_W), F32)
    return pl.pallas_call(
        functools.partial(_prep_kernel, seq_len=seq_len, tm=tm),
        grid=(n // tm,),
        in_specs=[
            pl.BlockSpec((tm, main_w), lambda i: (i, cb)),
            pl.BlockSpec((tm, LANES), lambda i: (i, wb)),
            pl.BlockSpec((8, main_w), lambda i: (jnp.maximum(i * (tm // 8) - 1, 0), cb)),
            pl.BlockSpec((8, LANES), lambda i: (jnp.maximum(i * (tm // 8) - 1, 0), wb)),
            fm_spec, fw_spec,
            vec(main_w), vec(LANES),
            pl.BlockSpec((LANES, 2 * RW_W), lambda i: (0, 0)),
            vec(RW_W), vec(RW_W), vec(RW_W), vec(RW_W), vec(RW_W),
            pl.BlockSpec((LANES, LANES), lambda i: (0, 0)),
        ],
        out_specs=[out_spec] * 7,
        out_shape=[out_sds] * 7,
        compiler_params=_cparams(("arbitrary",)),
        name="rwkv_prep",
    )(proj, proj, proj, proj, f_main, f_wa, mu[:, 0:main_w], mu[:, main_w:], lora_w,
      w0, a0, k_k, k_a, r_k, bd)


def _mm(a, b):
    return jnp.dot(a.astype(BF16), b.astype(BF16), preferred_element_type=F32)


def _mm_nt(a, b):
    return lax.dot_general(a.astype(BF16), b.astype(BF16), (((1,), (1,)), ((), ())),
                           preferred_element_type=F32)


def _mm_tn(a, b):
    return lax.dot_general(a.astype(BF16), b.astype(BF16), (((0,), (0,)), ((), ())),
                           preferred_element_type=F32)


def _scan_kernel(r_ref, lw_ref, k_ref, v_ref, kk_ref, b_ref, z0_ref, y_ref, zt_ref, z_scr,
                 *, rows, levels):
    c = pl.program_id(1)
    n_pairs = RW_W // LANES
    C = CHUNK

    @pl.when(c == 0)
    def _():
        z_scr[...] = z0_ref[...]

    def load(ref):
        x = ref[...]
        if rows < C:
            x = jnp.concatenate([x, jnp.zeros((C - rows, RW_W), F32)], axis=0)
        return x

    r, lw, k, v, kk, b = (load(ref) for ref in (r_ref, lw_ref, k_ref, v_ref, kk_ref, b_ref))
    ti = lax.broadcasted_iota(jnp.int32, (C, C), 0)
    tj = lax.broadcasted_iota(jnp.int32, (C, C), 1)
    tri = (ti >= tj).astype(F32)
    cum = jnp.dot(tri, lw, precision=lax.Precision.HIGHEST, preferred_element_type=F32)
    cum_end = cum[C - 1:C, :]
    w_in = jnp.exp(-cum)
    w_rest = jnp.exp(cum_end - cum)
    a_t = jnp.exp(cum - lw) * (-kk)
    b_t = b * w_in
    k_t = k * w_in
    r_t = r * jnp.exp(cum)
    b_h = b * w_rest
    k_h = k * w_rest
    w_end = jnp.exp(cum_end)

    lane = lax.broadcasted_iota(jnp.int32, (1, LANES), 1)
    in_a = lane < HEAD_DIM
    si = lax.broadcasted_iota(jnp.int32, (2 * C, 2 * C), 0)
    sj = lax.broadcasted_iota(jnp.int32, (2 * C, 2 * C), 1)
    same = (si // C) == (sj // C)
    strict = jnp.logical_and(same, (si % C) > (sj % C))
    incl = jnp.logical_and(same, (si % C) >= (sj % C))
    eye = (si == sj).astype(F32)

    def stack(x):
        return jnp.concatenate([jnp.where(in_a, x, 0.0), jnp.where(in_a, 0.0, x)], axis=0)

    for p in range(n_pairs):
        cols = slice(p * LANES, (p + 1) * LANES)
        a_s, b_s, k_s, r_s, v_s = (stack(x[:, cols]) for x in (a_t, b_t, k_t, r_t, v))
        bh_s, kh_s = stack(b_h[:, cols]), stack(k_h[:, cols])
        g = _mm_nt(jnp.concatenate([a_s, r_s], axis=0), jnp.concatenate([b_s, k_s], axis=0))
        ab = jnp.where(strict, g[0:2 * C, 0:2 * C], 0.0)
        ak = jnp.where(strict, g[0:2 * C, 2 * C:4 * C], 0.0)
        rb = jnp.where(incl, g[2 * C:4 * C, 0:2 * C], 0.0)
        rk = jnp.where(incl, g[2 * C:4 * C, 2 * C:4 * C], 0.0)
        tinv = eye + ab
        lp = ab
        for _ in range(levels - 1):
            lp = _mm(lp, lp)
            tinv = tinv + _mm(tinv, lp)
        akv = _mm(ak, v_s)
        au = _mm(tinv, jnp.concatenate([a_s, akv], axis=1))
        rhs = jnp.concatenate(
            [au, jnp.concatenate([jnp.zeros((2 * C, LANES), F32), v_s], axis=1)], axis=0)
        mn = _mm_tn(jnp.concatenate([bh_s, kh_s], axis=0), rhs)
        ry = _mm(jnp.concatenate([rb, rk], axis=1), rhs)
        m_mat = mn[:, 0:LANES] + eye * w_end[:, cols]
        r_hat = r_s + ry[:, 0:LANES]
        z = z_scr[p]
        yz = _mm(jnp.concatenate([r_hat, m_mat], axis=0), z)
        y_s = yz[0:2 * C] + ry[:, LANES:2 * LANES]
        z_new = yz[2 * C:2 * C + LANES] + mn[:, LANES:2 * LANES]
        z_scr[p] = z_new
        y_ref[:, cols] = (y_s[0:C] + y_s[C:2 * C])[0:rows]
        zt_ref[p] = z_new.T


def _rwkv_scan(prep, z0, batch, seq):
    rows = min(seq, CHUNK)
    n_chunks = seq // rows
    levels = max(1, (rows - 1).bit_length())
    n_pairs = RW_W // LANES
    tok = pl.BlockSpec((rows, RW_W), lambda bi, c: (bi * n_chunks + c, 0))
    st = pl.BlockSpec((None, n_pairs, LANES, LANES), lambda bi, c: (bi, 0, 0, 0))
    y, zt = pl.pallas_call(
        functools.partial(_scan_kernel, rows=rows, levels=levels),
        grid=(batch, n_chunks),
        in_specs=[tok] * 6 + [st],
        out_specs=[tok, st],
        out_shape=[jax.ShapeDtypeStruct((batch * seq, RW_W), F32),
                   jax.ShapeDtypeStruct((batch, n_pairs, LANES, LANES), F32)],
        scratch_shapes=[pltpu.VMEM((n_pairs, LANES, LANES), F32)],
        compiler_params=_cparams(("arbitrary", "arbitrary")),
        name="rwkv_scan",
    )(*prep, z0)
    return y, zt


def _pair_blockdiag_t(state):
    bsz = state.shape[0]
    zt = jnp.swapaxes(state, -1, -2).reshape(bsz, RWKV_HEADS // 2, 2, HEAD_DIM, HEAD_DIM)
    zero = jnp.zeros_like(zt[:, :, 0])
    top = jnp.concatenate([zt[:, :, 0], zero], axis=-1)
    bot = jnp.concatenate([zero, zt[:, :, 1]], axis=-1)
    return jnp.concatenate([top, bot], axis=-2)


def _unpair_state(st):
    a = st[:, :, 0:HEAD_DIM, 0:HEAD_DIM]
    b = st[:, :, HEAD_DIM:, HEAD_DIM:]
    return jnp.stack([a, b], axis=2).reshape(st.shape[0], RWKV_HEADS, HEAD_DIM, HEAD_DIM)


def _final_kernel(*refs, n_parts):
    att = refs[0:3 * n_parts] if n_parts else refs[0:1]
    n_att = len(att)
    (y_ref, bonus_ref, gates_ref, x_ref, lnw_ref, lnb_ref, pa_ref, pb_ref, wo_ref, bd_ref,
     o_ref) = refs[n_att:]
    if n_parts:
        parts = [(att[3 * g][...], att[3 * g + 1][...], att[3 * g + 2][...]) for g in range(n_parts)]
        m_all = functools.reduce(jnp.maximum, [m for _, m, _ in parts])
        num = sum(jnp.exp(m - m_all) * a for a, m, _ in parts)
        den = sum(jnp.exp(m - m_all) * l for _, m, l in parts)
        o_a = num / den
    else:
        o_a = att[0][...]
    gates = gates_ref[...]
    z_a = gates[:, 0:ATT_W]
    z_b = gates[:, ATT_W:ATT_W + RW_W]
    g_a = gates[:, ATT_W + RW_W:ATT_W + RW_W + 1024]
    g_b = gates[:, ATT_W + RW_W + 1024:]
    y = y_ref[...]
    bd = bd_ref[...]
    cols_out = []
    for c in range(RW_W // LANES):
        cols = slice(c * LANES, (c + 1) * LANES)
        yc = y[:, cols]
        mu = _segsum(yc, bd) * (1.0 / HEAD_DIM)
        d = yc - mu
        var = _segsum(d * d, bd) * (1.0 / HEAD_DIM)
        cols_out.append(d * lax.rsqrt(var + GN_EPS))
    yn = jnp.concatenate(cols_out, axis=1)
    o_b = yn * lnw_ref[...] + lnb_ref[...] + bonus_ref[...]
    h_a = jnp.dot((o_a * jax.nn.silu(z_a)).astype(BF16), pa_ref[...], preferred_element_type=F32)
    h_b = jnp.dot((o_b * jax.nn.silu(z_b)).astype(BF16), pb_ref[...], preferred_element_type=F32)
    mixed = jax.nn.sigmoid(g_a) * h_a + jax.nn.sigmoid(g_b) * h_b
    o_ref[...] = x_ref[...] + jnp.dot(mixed.astype(BF16), wo_ref[...], preferred_element_type=F32)


def _final(att, y_raw, bonus, proj, x, ln_w, ln_b, p_a, p_b, w_out, bd, tm):
    n, d = x.shape
    tm = min(tm, n)
    n_parts = len(att) // 3 if len(att) > 1 else 0
    gates_w = C_Q - C_GATES
    row = lambda w: pl.BlockSpec((tm, w), lambda i: (i, 0))
    full = lambda a: pl.BlockSpec(a.shape, lambda i: (0, 0))
    return pl.pallas_call(
        functools.partial(_final_kernel, n_parts=n_parts),
        grid=(n // tm,),
        in_specs=([row(ATT_W)] * len(att)
                  + [row(RW_W), row(RW_W), row(gates_w), row(d),
                     full(ln_w), full(ln_b), full(p_a), full(p_b), full(w_out), full(bd)]),
        out_specs=row(d),
        out_shape=jax.ShapeDtypeStruct((n, d), F32),
        compiler_params=_cparams(("arbitrary",)),
        name="gated_out",
    )(*att, y_raw, bonus, proj, x, ln_w, ln_b, p_a, p_b, w_out, bd)


def _permute_w_in(w_in):
    q, k, v, sh, gates = (w_in[:, 0:QKV], w_in[:, QKV:2 * QKV], w_in[:, 2 * QKV:3 * QKV],
                          w_in[:, 3 * QKV:3 * QKV + SHIFT_COLS], w_in[:, 3 * QKV + SHIFT_COLS:])
    pad = jnp.zeros((w_in.shape[0], P_COLS - IN_COLS), w_in.dtype)
    return jnp.concatenate([gates, q, k, v, sh, pad], axis=1).astype(BF16)


def _layer(x, batch, seq, shift_prev, z0, caches, lw, tm_proj, tm_prep, tm_out):
    (norm_g, w_perm, shift_mu, qg, kg, lora_w, w0, a0, k_k, k_a, r_k, ln_w, ln_b, p_a, p_b, w_out,
     bd) = lw
    proj = _inproj(x, norm_g, w_perm, qg, kg, bd, tm_proj)
    if caches is None:
        att = []
        for g in range(len(ATT_GROUPS)):
            att += list(_band_attention(proj, batch, seq, g, tq=512))
        pv = proj.reshape(batch, seq, P_COLS)
        new_k, new_v = [], []
        for g, (window, _) in enumerate(ATT_GROUPS):
            keep = min(window, seq)
            new_k.append(pv[:, seq - keep:, C_K + g * ATT_W:C_K + (g + 1) * ATT_W])
            new_v.append(pv[:, seq - keep:, C_V + g * ATT_W:C_V + (g + 1) * ATT_W])
    else:
        att, new_k, new_v = [], [], []
        for g, (kc, vc) in enumerate(caches):
            parts, ko, vo = _decode_attention(proj, kc, vc, g, batch, seq)
            att += parts
            new_k.append(ko)
            new_v.append(vo)
    prep = _rwkv_prep(proj, shift_prev, shift_mu, lora_w, w0, a0, k_k, k_a, r_k, bd, seq, tm_prep)
    y_raw, zt = _rwkv_scan(prep[:6], z0, batch, seq)
    y = _final(att, y_raw, prep[6], proj, x, ln_w, ln_b, p_a, p_b, w_out, bd, tm_out)
    new_shift = proj.reshape(batch, seq, P_COLS)[:, seq - 1, C_SH:C_SH + SHIFT_COLS]
    attn_state = []
    for kc, vc in zip(new_k, new_v):
        attn_state += [kc.reshape(batch, -1, HPG, HEAD_DIM), vc.reshape(batch, -1, HPG, HEAD_DIM)]
    return y, (*attn_state, _unpair_state(zt), new_shift)


def kernel(x_prompt, x_sample, cache_k_g0, cache_v_g0, cache_k_g1, cache_v_g1, cache_k_g2, cache_v_g2, state_rwkv, state_shift, norm_g, w_in, shift_mu, q_norm_g, k_norm_g, w0, w2, a0, a2, k_k, k_a, r_k, ln_x_w, ln_x_b, p_a, p_b, w_out):
    depth = norm_g.shape[0]
    batch, seq, d_model = x_prompt.shape
    dec_batch, dec_seq, _ = x_sample.shape
    bd = _bd_ones()
    h_p = x_prompt.reshape(batch * seq, d_model)
    h_s = x_sample.reshape(dec_batch * dec_seq, d_model)
    prompt_states, sample_states = [], []
    for layer in range(depth):
        zero = jnp.zeros((LORA, RW_W), F32)
        lora_w = jnp.concatenate([jnp.concatenate([w2[layer], zero], axis=1),
                                  jnp.concatenate([zero, a2[layer]], axis=1)], axis=0).astype(BF16)
        row = lambda a: a.reshape(1, -1)
        lw = (row(norm_g[layer]), _permute_w_in(w_in[layer]), row(shift_mu[layer]),
              jnp.tile(row(q_norm_g[layer]), (1, LANES // HEAD_DIM)),
              jnp.tile(row(k_norm_g[layer]), (1, LANES // HEAD_DIM)),
              lora_w, row(w0[layer]), row(a0[layer]), row(k_k[layer]), row(k_a[layer]),
              row(r_k[layer]), row(ln_x_w[layer]), row(ln_x_b[layer]),
              p_a[layer].astype(BF16), p_b[layer].astype(BF16), w_out[layer].astype(BF16), bd)
        zero_shift = jnp.zeros((batch, SHIFT_COLS), F32)
        zero_z = jnp.zeros((batch, RWKV_HEADS // 2, LANES, LANES), F32)
        h_p, st_p = _layer(h_p, batch, seq, zero_shift, zero_z, None, lw,
                           tm_proj=2048, tm_prep=512, tm_out=256)
        caches = [(k[layer].reshape(dec_batch, k.shape[2], ATT_W),
                   v[layer].reshape(dec_batch, v.shape[2], ATT_W))
                  for k, v in ((cache_k_g0, cache_v_g0), (cache_k_g1, cache_v_g1),
                               (cache_k_g2, cache_v_g2))]
        h_s, st_s = _layer(h_s, dec_batch, dec_seq, state_shift[layer],
                           _pair_blockdiag_t(state_rwkv[layer]), caches, lw,
                           tm_proj=min(1024, dec_batch * dec_seq), tm_prep=min(512, dec_batch * dec_seq),
                           tm_out=256)
        prompt_states.append(st_p)
        sample_states.append(st_s)
    ps = [jnp.stack(t) for t in zip(*prompt_states)]
    ss = [jnp.stack(t) for t in zip(*sample_states)]
    return (h_p.reshape(batch, seq, d_model), h_s.reshape(dec_batch, dec_seq, d_model),
            *ps, *ss)
```

```python
import functools

import jax
import jax.numpy as jnp
from jax import lax
from jax.experimental import pallas as pl
from jax.experimental.pallas import tpu as pltpu

F32 = jnp.float32
BF16 = jnp.bfloat16

HEAD_DIM = 64
HPG = 8
ATT_GROUPS = ((128, 1), (512, 4), (2048, 16))
N_ATT_HEADS = 24
RWKV_HEADS = 8
LORA = 64
QBLOCK = 128
RMS_EPS = 1e-6
GN_EPS = 64e-5
L2_EPS = 1e-12
CHUNK = 64

LANES = 128
NEW_PAD = 16
VMEM_LIMIT = 52 * 1024 * 1024

QKV = N_ATT_HEADS * HEAD_DIM
ATT_W = HPG * HEAD_DIM
RW_W = RWKV_HEADS * HEAD_DIM
C_GATES = 0
C_Q = 3072
C_K = C_Q + QKV
C_V = C_K + QKV
C_SH = C_V + QKV
SHIFT_COLS = 3 * RW_W + 2 * LORA
IN_COLS = C_SH + SHIFT_COLS
TN = 512
P_COLS = pl.cdiv(IN_COLS, TN) * TN


def _cparams(sem):
    return pltpu.CompilerParams(dimension_semantics=sem, vmem_limit_bytes=VMEM_LIMIT)


def _segsum(x, bd):
    hi = x.astype(BF16)
    lo = (x - hi.astype(F32)).astype(BF16)
    return (jnp.dot(hi, bd, preferred_element_type=F32)
            + jnp.dot(lo, bd, preferred_element_type=F32))


def _bd_ones():
    i = jnp.arange(LANES)
    return (i[:, None] // HEAD_DIM == i[None, :] // HEAD_DIM).astype(BF16)


def _alibi_slopes(g):
    return jnp.exp2(-8.0 * (jnp.arange(HPG, dtype=F32) + (g * HPG + 1)) / N_ATT_HEADS)


def _inproj_kernel(x_ref, g_ref, w_ref, qg_ref, kg_ref, bd_ref, o_ref, xn_ref):
    j = pl.program_id(1)

    @pl.when(j == 0)
    def _():
        x = x_ref[...]
        ms = jnp.mean(x * x, axis=-1, keepdims=True)
        xn_ref[...] = (x * lax.rsqrt(ms + RMS_EPS) * g_ref[...]).astype(BF16)

    t = jnp.dot(xn_ref[...], w_ref[...], preferred_element_type=F32)
    q_lo, k_lo, k_hi = C_Q // TN, C_K // TN, C_V // TN

    @pl.when(jnp.logical_or(j < q_lo, j >= k_hi))
    def _():
        o_ref[...] = t

    @pl.when(jnp.logical_and(j >= q_lo, j < k_hi))
    def _():
        gain = jnp.where(j < k_lo, qg_ref[...], kg_ref[...])
        bd = bd_ref[...]
        for c in range(TN // LANES):
            tc = t[:, c * LANES:(c + 1) * LANES]
            ms = _segsum(tc * tc, bd) * (1.0 / HEAD_DIM)
            o_ref[:, c * LANES:(c + 1) * LANES] = tc * lax.rsqrt(ms + RMS_EPS) * gain


def _inproj(x, norm_g, w_perm, qg, kg, bd, tm):
    n, d = x.shape
    return pl.pallas_call(
        _inproj_kernel,
        grid=(n // tm, P_COLS // TN),
        in_specs=[
            pl.BlockSpec((tm, d), lambda i, j: (i, 0)),
            pl.BlockSpec((1, d), lambda i, j: (0, 0)),
            pl.BlockSpec((d, TN), lambda i, j: (0, j)),
            pl.BlockSpec((1, LANES), lambda i, j: (0, 0)),
            pl.BlockSpec((1, LANES), lambda i, j: (0, 0)),
            pl.BlockSpec((LANES, LANES), lambda i, j: (0, 0)),
        ],
        out_specs=pl.BlockSpec((tm, TN), lambda i, j: (i, j)),
        out_shape=jax.ShapeDtypeStruct((n, P_COLS), F32),
        scratch_shapes=[pltpu.VMEM((tm, d), BF16)],
        compiler_params=_cparams(("arbitrary", "arbitrary")),
        name="inproj",
    )(x, norm_g, w_perm, qg, kg, bd)


def _band_kernel(sl_ref, q_ref, kp_ref, kc_ref, vp_ref, vc_ref, acc_ref, m_ref, l_ref, *, dil, tile):
    n = pl.program_id(1)
    p = pl.program_id(2)
    lane = lax.broadcasted_iota(jnp.int32, (1, LANES), 1)
    mask_a = lane < HEAD_DIM
    qi = lax.broadcasted_iota(jnp.int32, (QBLOCK, 2 * QBLOCK), 0)
    ki = lax.broadcasted_iota(jnp.int32, (QBLOCK, 2 * QBLOCK), 1)
    rel = qi + QBLOCK - ki
    band = jnp.logical_and(rel >= 0, rel <= QBLOCK)
    band_first = jnp.logical_and(band, jnp.logical_or(ki >= QBLOCK, n > 0))
    dist = (rel * dil).astype(F32)

    def rows(r, s):
        if dil == 1:
            return pl.ds(s * QBLOCK, QBLOCK)
        return pl.ds(r + s * QBLOCK * dil, QBLOCK, stride=dil)

    for r in range(dil):
        for s in range(tile // (dil * QBLOCK)):
            valid = band_first if s == 0 else band
            q = q_ref[rows(r, s), :]
            if s == 0:
                k_prev, v_prev = kp_ref[rows(r, 0), :], vp_ref[rows(r, 0), :]
            else:
                k_prev, v_prev = kc_ref[rows(r, s - 1), :], vc_ref[rows(r, s - 1), :]
            k = jnp.concatenate([k_prev, kc_ref[rows(r, s), :]], axis=0).astype(BF16)
            v = jnp.concatenate([v_prev, vc_ref[rows(r, s), :]], axis=0).astype(BF16)
            outs = []
            for h in range(2):
                hm = mask_a if h == 0 else jnp.logical_not(mask_a)
                qh = jnp.where(hm, q, 0.0).astype(BF16)
                sc = lax.dot_general(qh, k, (((1,), (1,)), ((), ())), preferred_element_type=F32)
                sc = sc * (1.0 / 8.0) - sl_ref[2 * p + h] * dist
                sc = jnp.where(valid, sc, -jnp.inf)
                m = jnp.max(sc, axis=-1, keepdims=True)
                pr = jnp.exp(sc - m)
                l = jnp.sum(pr, axis=-1, keepdims=True)
                acc = jnp.dot(pr.astype(BF16), v, preferred_element_type=F32)
                outs.append((acc, m, l))
            (acc0, m0, l0), (acc1, m1, l1) = outs
            acc_ref[rows(r, s), :] = jnp.where(mask_a, acc0, acc1)
            m_ref[rows(r, s), :] = jnp.where(mask_a, m0, m1)
            l_ref[rows(r, s), :] = jnp.where(mask_a, l0, l1)


def _band_attention(proj, batch, seq, g, tile):
    window, dil = ATT_GROUPS[g]
    assert window // dil == QBLOCK
    halo = QBLOCK * dil
    tile = min(tile, seq)
    assert tile % halo == 0 and seq % tile == 0
    nt, nh = seq // tile, seq // halo
    cq, ck, cv = ((c + g * ATT_W) // LANES for c in (C_Q, C_K, C_V))

    def cur(col):
        return pl.BlockSpec((tile, LANES), lambda b, n, p: (b * nt + n, col + p))

    def prev(col):
        return pl.BlockSpec((halo, LANES),
                            lambda b, n, p: (b * nh + jnp.maximum(n * (tile // halo) - 1, 0), col + p))

    out_spec = pl.BlockSpec((tile, LANES), lambda b, n, p: (b * nt + n, p))
    out_sds = jax.ShapeDtypeStruct((batch * seq, ATT_W), F32)
    return pl.pallas_call(
        functools.partial(_band_kernel, dil=dil, tile=tile),
        grid=(batch, nt, ATT_W // LANES),
        in_specs=[pl.BlockSpec(memory_space=pltpu.SMEM),
                  cur(cq), prev(ck), cur(ck), prev(cv), cur(cv)],
        out_specs=[out_spec, out_spec, out_spec],
        out_shape=[out_sds, out_sds, out_sds],
        compiler_params=_cparams(("arbitrary", "arbitrary", "arbitrary")),
        name=f"band_attn_g{g}",
    )(_alibi_slopes(g), proj, proj, proj, proj, proj)


def _decode_kernel(sl_ref, q_ref, knt_ref, vnt_ref, *refs, t_new, hps):
    n_g = len(ATT_GROUPS)
    kb, vb = refs[0:2 * n_g:2], refs[1:2 * n_g:2]
    o_ref = refs[2 * n_g]
    ko, vo = refs[2 * n_g + 1::2], refs[2 * n_g + 2::2]
    hh = pl.program_id(1)
    ti_n = lax.broadcasted_iota(jnp.int32, (t_new, NEW_PAD), 0)
    tj_n = lax.broadcasted_iota(jnp.int32, (t_new, NEW_PAD), 1)
    pi = lax.broadcasted_iota(jnp.int32, (NEW_PAD, LANES), 0)
    pj = lax.broadcasted_iota(jnp.int32, (NEW_PAD, LANES), 1)
    place = jnp.logical_and(pj == pi + (LANES - t_new), pi < t_new).astype(BF16)
    tail = lax.broadcasted_iota(jnp.int32, (HEAD_DIM, LANES), 1) >= LANES - t_new
    nt = (((1,), (1,)), ((), ()))

    for j in range(hps):
        parts = []
        for g, (wb, dil) in enumerate(ATT_GROUPS):
            k_t, v_t = kb[g][j], vb[g][j]
            kn_t, vn_t = knt_ref[g, j], vnt_ref[g, j]
            q = q_ref[g, j].astype(BF16)
            slope = sl_ref[g * HPG + hh * hps + j]
            wi = lax.broadcasted_iota(jnp.int32, (t_new, wb), 1)
            ti = lax.broadcasted_iota(jnp.int32, (t_new, wb), 0)
            d_buf = wb + ti - wi
            ok_buf = wi >= ti
            d_new = ti_n - tj_n
            ok_new = d_new >= 0
            if dil > 1:
                ok_buf = jnp.logical_and(ok_buf, (d_buf & (dil - 1)) == 0)
                ok_new = jnp.logical_and(ok_new, (d_new & (dil - 1)) == 0)
            s_buf = jnp.dot(q, k_t.astype(BF16), preferred_element_type=F32) * (1.0 / 8.0)
            s_buf = jnp.where(ok_buf, s_buf - slope * d_buf.astype(F32), -jnp.inf)
            s_new = jnp.dot(q, kn_t.astype(BF16), preferred_element_type=F32) * (1.0 / 8.0)
            s_new = jnp.where(ok_new, s_new - slope * d_new.astype(F32), -jnp.inf)
            m = jnp.maximum(jnp.max(s_buf, axis=-1, keepdims=True),
                            jnp.max(s_new, axis=-1, keepdims=True))
            p_buf = jnp.exp(s_buf - m)
            p_new = jnp.exp(s_new - m)
            l = jnp.sum(p_buf, axis=-1, keepdims=True) + jnp.sum(p_new, axis=-1, keepdims=True)
            acc = (lax.dot_general(p_buf.astype(BF16), v_t.astype(BF16), nt, preferred_element_type=F32)
                   + lax.dot_general(p_new.astype(BF16), vn_t.astype(BF16), nt,
                                     preferred_element_type=F32))
            parts.append((acc, m, l))
            for src, new, dst in ((k_t, kn_t, ko[g]), (v_t, vn_t, vo[g])):
                rolled = pltpu.roll(src, wb - t_new, 1)
                hi = new.astype(BF16)
                lo = (new - hi.astype(F32)).astype(BF16)
                placed = (jnp.dot(hi, place, preferred_element_type=F32)
                          + jnp.dot(lo, place, preferred_element_type=F32))
                if wb > LANES:
                    dst[j, :, 0:wb - LANES] = rolled[:, 0:wb - LANES]
                dst[j, :, wb - LANES:wb] = jnp.where(tail, placed, rolled[:, wb - LANES:wb])
        m_all = functools.reduce(jnp.maximum, [m for _, m, _ in parts])
        num = sum(jnp.exp(m - m_all) * a for a, m, _ in parts)
        den = sum(jnp.exp(m - m_all) * l for _, m, l in parts)
        o_ref[j] = num / den


def _decode_attention(proj, caches, dec_batch, t_new):
    n_g = len(ATT_GROUPS)
    hps = HPG // 2
    assert all(k.shape[1] == w for (k, _), (w, _) in zip(caches, ATT_GROUPS))

    def heads(col0):
        return proj[:, col0:col0 + QKV].reshape(dec_batch, t_new, n_g, HPG, HEAD_DIM)

    q = jnp.transpose(heads(C_Q), (0, 2, 3, 1, 4))
    pad = ((0, 0),) * 4 + ((0, NEW_PAD - t_new),)
    knt = jnp.pad(jnp.transpose(heads(C_K), (0, 2, 3, 4, 1)), pad)
    vnt = jnp.pad(jnp.transpose(heads(C_V), (0, 2, 3, 4, 1)), pad)
    native = [jnp.transpose(c, (0, 2, 3, 1)) for kv in caches for c in kv]
    slopes = jnp.concatenate([_alibi_slopes(g) for g in range(n_g)])

    new_q = pl.BlockSpec((None, n_g, hps, t_new, HEAD_DIM), lambda b, h: (b, 0, h, 0, 0))
    new_kv = pl.BlockSpec((None, n_g, hps, HEAD_DIM, NEW_PAD), lambda b, h: (b, 0, h, 0, 0))
    cache_specs = [pl.BlockSpec((None, hps, HEAD_DIM, c.shape[-1]), lambda b, h: (b, h, 0, 0))
                   for c in native]
    outs = pl.pallas_call(
        functools.partial(_decode_kernel, t_new=t_new, hps=hps),
        grid=(dec_batch, HPG // hps),
        in_specs=[pl.BlockSpec(memory_space=pltpu.SMEM), new_q, new_kv, new_kv] + cache_specs,
        out_specs=[pl.BlockSpec((None, hps, t_new, HEAD_DIM), lambda b, h: (b, h, 0, 0))] + cache_specs,
        out_shape=([jax.ShapeDtypeStruct((dec_batch, HPG, t_new, HEAD_DIM), F32)]
                   + [jax.ShapeDtypeStruct(c.shape, F32) for c in native]),
        compiler_params=_cparams(("arbitrary", "arbitrary")),
        name="decode_attn",
    )(slopes, q, knt, vnt, *native)
    o_a = jnp.transpose(outs[0], (0, 2, 1, 3)).reshape(dec_batch * t_new, ATT_W)
    new = [jnp.transpose(c, (0, 3, 1, 2)) for c in outs[1:]]
    return o_a, new[0::2], new[1::2]


def _prep_kernel(cur_ref, wa_ref, pcur_ref, pwa_ref, fcur_ref, fwa_ref, mu_ref, muwa_ref, lora_ref,
                 w0_ref, a0_ref, kk_ref, ka_ref, rk_ref, bd_ref,
                 r_o, lw_o, k_o, v_o, kkn_o, b_o, bonus_o, *, seq_len, tm):
    i = pl.program_id(0)
    row = lax.broadcasted_iota(jnp.int32, (tm, 1), 0)

    def shifted(cur, p8, first):
        rolled = pltpu.roll(cur, 1, 0)
        if seq_len >= tm:
            at_start = (i * tm) % seq_len == 0
            edge = jnp.where(at_start, first, p8[7:8, :])
            return jnp.where(row == 0, edge, rolled)
        return jnp.where(row % seq_len == 0, first, rolled)

    cur = cur_ref[...]
    wa = wa_ref[...]
    xs = cur + (shifted(cur, pcur_ref[...], fcur_ref[...]) - cur) * mu_ref[...]
    xwa = wa + (shifted(wa, pwa_ref[...], fwa_ref[...]) - wa) * muwa_ref[...]
    r = xs[:, 0:RW_W]
    kr = xs[:, RW_W:2 * RW_W]
    vr = xs[:, 2 * RW_W:3 * RW_W]
    lane = lax.broadcasted_iota(jnp.int32, (1, LANES), 1)
    lin = jnp.where(lane < LORA, jnp.tanh(xwa), xwa).astype(BF16)
    lo = jnp.dot(lin, lora_ref[...], preferred_element_type=F32)
    z = -(w0_ref[...] + lo[:, 0:RW_W])
    softplus = jnp.maximum(z, 0.0) + jnp.log1p(jnp.exp(-jnp.abs(z)))
    lw = -jnp.exp(-softplus - 0.5)
    a = jax.nn.sigmoid(a0_ref[...] + lo[:, RW_W:2 * RW_W])
    kk = kr * kk_ref[...]
    k_mod = kr * (1.0 + (a - 1.0) * ka_ref[...])
    rkk = r * k_mod * rk_ref[...]
    bd = bd_ref[...]
    r_o[...] = r
    lw_o[...] = lw
    k_o[...] = k_mod
    v_o[...] = vr
    for c in range(RW_W // LANES):
        cols = slice(c * LANES, (c + 1) * LANES)
        kc = kk[:, cols]
        nrm = jnp.maximum(jnp.sqrt(_segsum(kc * kc, bd)), L2_EPS)
        kkn = kc / nrm
        kkn_o[:, cols] = kkn
        b_o[:, cols] = kkn * a[:, cols]
        bonus_o[:, cols] = _segsum(rkk[:, cols], bd) * vr[:, cols]


def _rwkv_prep(proj, first, mu, lora_w, w0, a0, k_k, k_a, r_k, bd, seq_len, tm):
    n = proj.shape[0]
    main_w = 3 * RW_W
    cb, wb = C_SH // main_w, (C_SH + main_w) // LANES
    per_seq_first = seq_len >= tm
    if per_seq_first:
        f_main = first[:, None, 0:main_w]
        f_wa = first[:, None, main_w:]
        fm_spec = pl.BlockSpec((None, 1, main_w), lambda i: ((i * tm) // seq_len, 0, 0))
        fw_spec = pl.BlockSpec((None, 1, LANES), lambda i: ((i * tm) // seq_len, 0, 0))
    else:
        rep = jnp.repeat(first, seq_len, axis=0)
        f_main, f_wa = rep[:, 0:main_w], rep[:, main_w:]
        fm_spec = pl.BlockSpec((tm, main_w), lambda i: (i, 0))
        fw_spec = pl.BlockSpec((tm, LANES), lambda i: (i, 0))
    vec = lambda w: pl.BlockSpec((1, w), lambda i: (0, 0))
    out_spec = pl.BlockSpec((tm, RW_W), lambda i: (i, 0))
    out_sds = jax.ShapeDtypeStruct((n, RW_W), F32)
    return pl.pallas_call(
        functools.partial(_prep_kernel, seq_len=seq_len, tm=tm),
        grid=(n // tm,),
        in_specs=[
            pl.BlockSpec((tm, main_w), lambda i: (i, cb)),
            pl.BlockSpec((tm, LANES), lambda i: (i, wb)),
            pl.BlockSpec((8, main_w), lambda i: (jnp.maximum(i * (tm // 8) - 1, 0), cb)),
            pl.BlockSpec((8, LANES), lambda i: (jnp.maximum(i * (tm // 8) - 1, 0), wb)),
            fm_spec, fw_spec,
            vec(main_w), vec(LANES),
            pl.BlockSpec((LANES, 2 * RW_W), lambda i: (0, 0)),
            vec(RW_W), vec(RW_W), vec(RW_W), vec(RW_W), vec(RW_W),
            pl.BlockSpec((LANES, LANES), lambda i: (0, 0)),
        ],
        out_specs=[out_spec] * 7,
        out_shape=[out_sds] * 7,
        compiler_params=_cparams(("arbitrary",)),
        name="rwkv_prep",
    )(proj, proj, proj, proj, f_main, f_wa, mu[:, 0:main_w], mu[:, main_w:], lora_w,
      w0, a0, k_k, k_a, r_k, bd)


def _mm(a, b):
    return jnp.dot(a.astype(BF16), b.astype(BF16), preferred_element_type=F32)


def _mm_nt(a, b):
    return lax.dot_general(a.astype(BF16), b.astype(BF16), (((1,), (1,)), ((), ())),
                           preferred_element_type=F32)


def _mm_tn(a, b):
    return lax.dot_general(a.astype(BF16), b.astype(BF16), (((0,), (0,)), ((), ())),
                           preferred_element_type=F32)


def _scan_kernel(r_ref, lw_ref, k_ref, v_ref, kk_ref, b_ref, y_ref, zt_ref, z_scr):
    c = pl.program_id(1)
    n_pairs = RW_W // LANES
    C = CHUNK
    levels = (C - 1).bit_length()

    @pl.when(c == 0)
    def _():
        z_scr[...] = jnp.zeros_like(z_scr)

    r, lw, k, v, kk, b = (ref[...] for ref in (r_ref, lw_ref, k_ref, v_ref, kk_ref, b_ref))
    ti = lax.broadcasted_iota(jnp.int32, (C, C), 0)
    tj = lax.broadcasted_iota(jnp.int32, (C, C), 1)
    tri = (ti >= tj).astype(F32)
    cum = jnp.dot(tri, lw, precision=lax.Precision.HIGHEST, preferred_element_type=F32)
    cum_end = cum[C - 1:C, :]
    w_in = jnp.exp(-cum)
    w_rest = jnp.exp(cum_end - cum)
    a_t = jnp.exp(cum - lw) * (-kk)
    b_t = b * w_in
    k_t = k * w_in
    r_t = r * jnp.exp(cum)
    b_h = b * w_rest
    k_h = k * w_rest
    w_end = jnp.exp(cum_end)

    lane = lax.broadcasted_iota(jnp.int32, (1, LANES), 1)
    in_a = lane < HEAD_DIM
    si = lax.broadcasted_iota(jnp.int32, (2 * C, 2 * C), 0)
    sj = lax.broadcasted_iota(jnp.int32, (2 * C, 2 * C), 1)
    same = (si // C) == (sj // C)
    strict = jnp.logical_and(same, (si % C) > (sj % C))
    incl = jnp.logical_and(same, (si % C) >= (sj % C))
    eye = (si == sj).astype(F32)

    def stack(x):
        return jnp.concatenate([jnp.where(in_a, x, 0.0), jnp.where(in_a, 0.0, x)], axis=0)

    for p in range(n_pairs):
        cols = slice(p * LANES, (p + 1) * LANES)
        a_s, b_s, k_s, r_s, v_s = (stack(x[:, cols]) for x in (a_t, b_t, k_t, r_t, v))
        bh_s, kh_s = stack(b_h[:, cols]), stack(k_h[:, cols])
        g = _mm_nt(jnp.concatenate([a_s, r_s], axis=0), jnp.concatenate([b_s, k_s], axis=0))
        ab = jnp.where(strict, g[0:2 * C, 0:2 * C], 0.0)
        ak = jnp.where(strict, g[0:2 * C, 2 * C:4 * C], 0.0)
        rb = jnp.where(incl, g[2 * C:4 * C, 0:2 * C], 0.0)
        rk = jnp.where(incl, g[2 * C:4 * C, 2 * C:4 * C], 0.0)
        tinv = eye + ab
        lp = ab
        for _ in range(levels - 1):
            lp = _mm(lp, lp)
            tinv = tinv + _mm(tinv, lp)
        akv = _mm(ak, v_s)
        au = _mm(tinv, jnp.concatenate([a_s, akv], axis=1))
        rhs = jnp.concatenate(
            [au, jnp.concatenate([jnp.zeros((2 * C, LANES), F32), v_s], axis=1)], axis=0)
        mn = _mm_tn(jnp.concatenate([bh_s, kh_s], axis=0), rhs)
        ry = _mm(jnp.concatenate([rb, rk], axis=1), rhs)
        m_mat = mn[:, 0:LANES] + eye * w_end[:, cols]
        r_hat = r_s + ry[:, 0:LANES]
        z = z_scr[p]
        yz = _mm(jnp.concatenate([r_hat, m_mat], axis=0), z)
        y_s = yz[0:2 * C] + ry[:, LANES:2 * LANES]
        z_new = yz[2 * C:2 * C + LANES] + mn[:, LANES:2 * LANES]
        z_scr[p] = z_new
        y_ref[:, cols] = y_s[0:C] + y_s[C:2 * C]
        zt_ref[p] = z_new.T


def _rwkv_scan(prep, batch, seq):
    assert seq % CHUNK == 0
    n_chunks = seq // CHUNK
    n_pairs = RW_W // LANES
    tok = pl.BlockSpec((CHUNK, RW_W), lambda bi, c: (bi * n_chunks + c, 0))
    st = pl.BlockSpec((None, n_pairs, LANES, LANES), lambda bi, c: (bi, 0, 0, 0))
    y, zt = pl.pallas_call(
        _scan_kernel,
        grid=(batch, n_chunks),
        in_specs=[tok] * 6,
        out_specs=[tok, st],
        out_shape=[jax.ShapeDtypeStruct((batch * seq, RW_W), F32),
                   jax.ShapeDtypeStruct((batch, n_pairs, LANES, LANES), F32)],
        scratch_shapes=[pltpu.VMEM((n_pairs, LANES, LANES), F32)],
        compiler_params=_cparams(("arbitrary", "arbitrary")),
        name="rwkv_scan",
    )(*prep)
    s_a = zt[:, :, 0:HEAD_DIM, 0:HEAD_DIM]
    s_b = zt[:, :, HEAD_DIM:, HEAD_DIM:]
    return y, jnp.stack([s_a, s_b], axis=2).reshape(batch, RWKV_HEADS, HEAD_DIM, HEAD_DIM)


def _seq_scan_kernel(r_ref, lw_ref, k_ref, v_ref, kk_ref, b_ref, s_ref, y_ref, so_ref, w_scr, *, t_new):
    w_scr[...] = jnp.exp(lw_ref[...])

    def body(vi, carry):
        sv = s_ref[vi]
        for t in range(t_new):
            sa = jnp.sum(sv * kk_ref[t], axis=0, keepdims=True)
            sv = sv * w_scr[t] - sa * b_ref[t] + v_ref[t, pl.ds(vi, 1), :] * k_ref[t]
            y_ref[t, pl.ds(vi, 1), :] = jnp.sum(sv * r_ref[t], axis=0, keepdims=True)
        so_ref[vi] = sv
        return carry

    lax.fori_loop(0, HEAD_DIM, body, 0)


def _seq_scan(prep, state, dec_batch, t_new):
    def lanes(x):
        return jnp.transpose(x.reshape(dec_batch, t_new, RWKV_HEADS, HEAD_DIM), (1, 2, 3, 0))

    vec = pl.BlockSpec((t_new, None, HEAD_DIM, dec_batch), lambda h: (0, h, 0, 0))
    st = pl.BlockSpec((None, HEAD_DIM, HEAD_DIM, dec_batch), lambda h: (h, 0, 0, 0))
    s_native = jnp.transpose(state, (1, 2, 3, 0))
    y, s_new = pl.pallas_call(
        functools.partial(_seq_scan_kernel, t_new=t_new),
        grid=(RWKV_HEADS,),
        in_specs=[vec] * 6 + [st],
        out_specs=[vec, st],
        out_shape=[jax.ShapeDtypeStruct((t_new, RWKV_HEADS, HEAD_DIM, dec_batch), F32),
                   jax.ShapeDtypeStruct(s_native.shape, F32)],
        scratch_shapes=[pltpu.VMEM((t_new, HEAD_DIM, dec_batch), F32)],
        compiler_params=_cparams(("arbitrary",)),
        name="rwkv_seq_scan",
    )(*[lanes(x) for x in prep], s_native)
    y = jnp.transpose(y, (3, 0, 1, 2)).reshape(dec_batch * t_new, RW_W)
    return y, jnp.transpose(s_new, (3, 0, 1, 2))


def _final_kernel(*refs, n_parts):
    att = refs[0:3 * n_parts] if n_parts else refs[0:1]
    n_att = len(att)
    (y_ref, bonus_ref, gates_ref, x_ref, lnw_ref, lnb_ref, pa_ref, pb_ref, wo_ref, bd_ref,
     o_ref) = refs[n_att:]
    if n_parts:
        parts = [(att[3 * g][...], att[3 * g + 1][...], att[3 * g + 2][...]) for g in range(n_parts)]
        m_all = functools.reduce(jnp.maximum, [m for _, m, _ in parts])
        num = sum(jnp.exp(m - m_all) * a for a, m, _ in parts)
        den = sum(jnp.exp(m - m_all) * l for _, m, l in parts)
        o_a = num / den
    else:
        o_a = att[0][...]
    gates = gates_ref[...]
    z_a = gates[:, 0:ATT_W]
    z_b = gates[:, ATT_W:ATT_W + RW_W]
    g_a = gates[:, ATT_W + RW_W:ATT_W + RW_W + 1024]
    g_b = gates[:, ATT_W + RW_W + 1024:]
    y = y_ref[...]
    bd = bd_ref[...]
    cols_out = []
    for c in range(RW_W // LANES):
        cols = slice(c * LANES, (c + 1) * LANES)
        yc = y[:, cols]
        mu = _segsum(yc, bd) * (1.0 / HEAD_DIM)
        d = yc - mu
        var = _segsum(d * d, bd) * (1.0 / HEAD_DIM)
        cols_out.append(d * lax.rsqrt(var + GN_EPS))
    yn = jnp.concatenate(cols_out, axis=1)
    o_b = yn * lnw_ref[...] + lnb_ref[...] + bonus_ref[...]
    h_a = jnp.dot((o_a * jax.nn.silu(z_a)).astype(BF16), pa_ref[...], preferred_element_type=F32)
    h_b = jnp.dot((o_b * jax.nn.silu(z_b)).astype(BF16), pb_ref[...], preferred_element_type=F32)
    mixed = jax.nn.sigmoid(g_a) * h_a + jax.nn.sigmoid(g_b) * h_b
    o_ref[...] = x_ref[...] + jnp.dot(mixed.astype(BF16), wo_ref[...], preferred_element_type=F32)


def _final(att, y_raw, bonus, proj, x, ln_w, ln_b, p_a, p_b, w_out, bd, tm):
    n, d = x.shape
    tm = min(tm, n)
    n_parts = len(att) // 3 if len(att) > 1 else 0
    gates_w = C_Q - C_GATES
    row = lambda w: pl.BlockSpec((tm, w), lambda i: (i, 0))
    full = lambda a: pl.BlockSpec(a.shape, lambda i: (0, 0))
    return pl.pallas_call(
        functools.partial(_final_kernel, n_parts=n_parts),
        grid=(n // tm,),
        in_specs=([row(ATT_W)] * len(att)
                  + [row(RW_W), row(RW_W), row(gates_w), row(d),
                     full(ln_w), full(ln_b), full(p_a), full(p_b), full(w_out), full(bd)]),
        out_specs=row(d),
        out_shape=jax.ShapeDtypeStruct((n, d), F32),
        compiler_params=_cparams(("arbitrary",)),
        name="gated_out",
    )(*att, y_raw, bonus, proj, x, ln_w, ln_b, p_a, p_b, w_out, bd)


def _permute_w_in(w_in):
    q, k, v, sh, gates = (w_in[:, 0:QKV], w_in[:, QKV:2 * QKV], w_in[:, 2 * QKV:3 * QKV],
                          w_in[:, 3 * QKV:3 * QKV + SHIFT_COLS], w_in[:, 3 * QKV + SHIFT_COLS:])
    pad = jnp.zeros((w_in.shape[0], P_COLS - IN_COLS), w_in.dtype)
    return jnp.concatenate([gates, q, k, v, sh, pad], axis=1).astype(BF16)


def _layer(x, batch, seq, shift_prev, state, caches, lw, tm_proj, tm_prep, tm_out):
    (norm_g, w_perm, shift_mu, qg, kg, lora_w, w0, a0, k_k, k_a, r_k, ln_w, ln_b, p_a, p_b, w_out,
     bd) = lw
    proj = _inproj(x, norm_g, w_perm, qg, kg, bd, tm_proj)
    prep = _rwkv_prep(proj, shift_prev, shift_mu, lora_w, w0, a0, k_k, k_a, r_k, bd, seq, tm_prep)
    if caches is None:
        att = []
        for g in range(len(ATT_GROUPS)):
            att += _band_attention(proj, batch, seq, g, tile=2048)
        pv = proj.reshape(batch, seq, P_COLS)
        new_k, new_v = [], []
        for g, (window, _) in enumerate(ATT_GROUPS):
            keep = min(window, seq)
            new_k.append(pv[:, seq - keep:, C_K + g * ATT_W:C_K + (g + 1) * ATT_W]
                         .reshape(batch, keep, HPG, HEAD_DIM))
            new_v.append(pv[:, seq - keep:, C_V + g * ATT_W:C_V + (g + 1) * ATT_W]
                         .reshape(batch, keep, HPG, HEAD_DIM))
        y_raw, new_state = _rwkv_scan(prep[:6], batch, seq)
    else:
        o_a, new_k, new_v = _decode_attention(proj, caches, batch, seq)
        att = [o_a]
        y_raw, new_state = _seq_scan(prep[:6], state, batch, seq)
    y = _final(att, y_raw, prep[6], proj, x, ln_w, ln_b, p_a, p_b, w_out, bd, tm_out)
    new_shift = proj.reshape(batch, seq, P_COLS)[:, seq - 1, C_SH:C_SH + SHIFT_COLS]
    attn_state = [c for kv in zip(new_k, new_v) for c in kv]
    return y, (*attn_state, new_state, new_shift)


def kernel(x_prompt, x_sample, cache_k_g0, cache_v_g0, cache_k_g1, cache_v_g1, cache_k_g2, cache_v_g2, state_rwkv, state_shift, norm_g, w_in, shift_mu, q_norm_g, k_norm_g, w0, w2, a0, a2, k_k, k_a, r_k, ln_x_w, ln_x_b, p_a, p_b, w_out):
    depth = norm_g.shape[0]
    batch, seq, d_model = x_prompt.shape
    dec_batch, dec_seq, _ = x_sample.shape
    bd = _bd_ones()
    h_p = x_prompt.reshape(batch * seq, d_model)
    h_s = x_sample.reshape(dec_batch * dec_seq, d_model)
    prompt_states, sample_states = [], []
    for layer in range(depth):
        zero = jnp.zeros((LORA, RW_W), F32)
        lora_w = jnp.concatenate([jnp.concatenate([w2[layer], zero], axis=1),
                                  jnp.concatenate([zero, a2[layer]], axis=1)], axis=0).astype(BF16)
        row = lambda a: a.reshape(1, -1)
        lw = (row(norm_g[layer]), _permute_w_in(w_in[layer]), row(shift_mu[layer]),
              jnp.tile(row(q_norm_g[layer]), (1, LANES // HEAD_DIM)),
              jnp.tile(row(k_norm_g[layer]), (1, LANES // HEAD_DIM)),
              lora_w, row(w0[layer]), row(a0[layer]), row(k_k[layer]), row(k_a[layer]),
              row(r_k[layer]), row(ln_x_w[layer]), row(ln_x_b[layer]),
              p_a[layer].astype(BF16), p_b[layer].astype(BF16), w_out[layer].astype(BF16), bd)
        zero_shift = jnp.zeros((batch, SHIFT_COLS), F32)
        h_p, st_p = _layer(h_p, batch, seq, zero_shift, None, None, lw,
                           tm_proj=2048, tm_prep=512, tm_out=256)
        caches = [(cache_k_g0[layer], cache_v_g0[layer]), (cache_k_g1[layer], cache_v_g1[layer]),
                  (cache_k_g2[layer], cache_v_g2[layer])]
        n_s = dec_batch * dec_seq
        h_s, st_s = _layer(h_s, dec_batch, dec_seq, state_shift[layer], state_rwkv[layer], caches, lw,
                           tm_proj=min(1024, n_s), tm_prep=min(512, n_s), tm_out=256)
        prompt_states.append(st_p)
        sample_states.append(st_s)
    ps = [jnp.stack(t) for t in zip(*prompt_states)]
    ss = [jnp.stack(t) for t in zip(*sample_states)]
    return (h_p.reshape(batch, seq, d_model), h_s.reshape(dec_batch, dec_seq, d_model),
            *ps, *ss)
```

```python
import functools

import jax
import jax.numpy as jnp
from jax import lax
from jax.experimental import pallas as pl
from jax.experimental.pallas import tpu as pltpu

F32 = jnp.float32
BF16 = jnp.bfloat16

HEAD_DIM = 64
HPG = 8
ATT_GROUPS = ((128, 1), (512, 4), (2048, 16))
N_ATT_HEADS = 24
RWKV_HEADS = 8
LORA = 64
QBLOCK = 128
RMS_EPS = 1e-6
GN_EPS = 64e-5
L2_EPS = 1e-12
CHUNK = 64

LANES = 128
NEW_PAD = 16
VMEM_LIMIT = 52 * 1024 * 1024

QKV = N_ATT_HEADS * HEAD_DIM
ATT_W = HPG * HEAD_DIM
RW_W = RWKV_HEADS * HEAD_DIM
C_GATES = 0
C_Q = 3072
C_K = C_Q + QKV
C_V = C_K + QKV
C_SH = C_V + QKV
SHIFT_COLS = 3 * RW_W + 2 * LORA
IN_COLS = C_SH + SHIFT_COLS
TN = 512
P_COLS = pl.cdiv(IN_COLS, TN) * TN


def _cparams(sem):
    return pltpu.CompilerParams(dimension_semantics=sem, vmem_limit_bytes=VMEM_LIMIT)


def _segsum(x, bd):
    hi = x.astype(BF16)
    lo = (x - hi.astype(F32)).astype(BF16)
    return (jnp.dot(hi, bd, preferred_element_type=F32)
            + jnp.dot(lo, bd, preferred_element_type=F32))


def _bd_ones():
    i = jnp.arange(LANES)
    return (i[:, None] // HEAD_DIM == i[None, :] // HEAD_DIM).astype(BF16)


def _alibi_slopes(g):
    return jnp.exp2(-8.0 * (jnp.arange(HPG, dtype=F32) + (g * HPG + 1)) / N_ATT_HEADS)


def _inproj_kernel(x_ref, g_ref, w_ref, qg_ref, kg_ref, bd_ref, o_ref, xn_ref):
    j = pl.program_id(1)

    @pl.when(j == 0)
    def _():
        x = x_ref[...]
        ms = jnp.mean(x * x, axis=-1, keepdims=True)
        xn_ref[...] = (x * lax.rsqrt(ms + RMS_EPS) * g_ref[...]).astype(BF16)

    t = jnp.dot(xn_ref[...], w_ref[...], preferred_element_type=F32)
    q_lo, k_lo, k_hi = C_Q // TN, C_K // TN, C_V // TN

    @pl.when(jnp.logical_or(j < q_lo, j >= k_hi))
    def _():
        o_ref[...] = t

    @pl.when(jnp.logical_and(j >= q_lo, j < k_hi))
    def _():
        gain = jnp.where(j < k_lo, qg_ref[...], kg_ref[...])
        bd = bd_ref[...]
        for c in range(TN // LANES):
            tc = t[:, c * LANES:(c + 1) * LANES]
            ms = _segsum(tc * tc, bd) * (1.0 / HEAD_DIM)
            o_ref[:, c * LANES:(c + 1) * LANES] = tc * lax.rsqrt(ms + RMS_EPS) * gain


def _inproj(x, norm_g, w_perm, qg, kg, bd, tm):
    n, d = x.shape
    return pl.pallas_call(
        _inproj_kernel,
        grid=(n // tm, P_COLS // TN),
        in_specs=[
            pl.BlockSpec((tm, d), lambda i, j: (i, 0)),
            pl.BlockSpec((1, d), lambda i, j: (0, 0)),
            pl.BlockSpec((d, TN), lambda i, j: (0, j)),
            pl.BlockSpec((1, LANES), lambda i, j: (0, 0)),
            pl.BlockSpec((1, LANES), lambda i, j: (0, 0)),
            pl.BlockSpec((LANES, LANES), lambda i, j: (0, 0)),
        ],
        out_specs=pl.BlockSpec((tm, TN), lambda i, j: (i, j)),
        out_shape=jax.ShapeDtypeStruct((n, P_COLS), F32),
        scratch_shapes=[pltpu.VMEM((tm, d), BF16)],
        compiler_params=_cparams(("arbitrary", "arbitrary")),
        name="inproj",
    )(x, norm_g, w_perm, qg, kg, bd)


def _band_kernel(sl_ref, q_ref, kp_ref, kc_ref, vp_ref, vc_ref, acc_ref, m_ref, l_ref, *, dil, tile):
    n = pl.program_id(1)
    p = pl.program_id(2)
    lane = lax.broadcasted_iota(jnp.int32, (1, LANES), 1)
    mask_a = lane < HEAD_DIM
    qi = lax.broadcasted_iota(jnp.int32, (QBLOCK, 2 * QBLOCK), 0)
    ki = lax.broadcasted_iota(jnp.int32, (QBLOCK, 2 * QBLOCK), 1)
    rel = qi + QBLOCK - ki
    band = jnp.logical_and(rel >= 0, rel <= QBLOCK)
    band_first = jnp.logical_and(band, jnp.logical_or(ki >= QBLOCK, n > 0))
    dist = (rel * dil).astype(F32)

    def rows(r, s):
        if dil == 1:
            return pl.ds(s * QBLOCK, QBLOCK)
        return pl.ds(r + s * QBLOCK * dil, QBLOCK, stride=dil)

    for r in range(dil):
        for s in range(tile // (dil * QBLOCK)):
            valid = band_first if s == 0 else band
            q = q_ref[rows(r, s), :]
            if s == 0:
                k_prev, v_prev = kp_ref[rows(r, 0), :], vp_ref[rows(r, 0), :]
            else:
                k_prev, v_prev = kc_ref[rows(r, s - 1), :], vc_ref[rows(r, s - 1), :]
            k = jnp.concatenate([k_prev, kc_ref[rows(r, s), :]], axis=0).astype(BF16)
            v = jnp.concatenate([v_prev, vc_ref[rows(r, s), :]], axis=0).astype(BF16)
            outs = []
            for h in range(2):
                hm = mask_a if h == 0 else jnp.logical_not(mask_a)
                qh = jnp.where(hm, q, 0.0).astype(BF16)
                sc = lax.dot_general(qh, k, (((1,), (1,)), ((), ())), preferred_element_type=F32)
                sc = sc * (1.0 / 8.0) - sl_ref[2 * p + h] * dist
                sc = jnp.where(valid, sc, -jnp.inf)
                m = jnp.max(sc, axis=-1, keepdims=True)
                pr = jnp.exp(sc - m)
                l = jnp.sum(pr, axis=-1, keepdims=True)
                acc = jnp.dot(pr.astype(BF16), v, preferred_element_type=F32)
                outs.append((acc, m, l))
            (acc0, m0, l0), (acc1, m1, l1) = outs
            acc_ref[rows(r, s), :] = jnp.where(mask_a, acc0, acc1)
            m_ref[rows(r, s), :] = jnp.where(mask_a, m0, m1)
            l_ref[rows(r, s), :] = jnp.where(mask_a, l0, l1)


def _band_attention(proj, batch, seq, g, tile):
    window, dil = ATT_GROUPS[g]
    assert window // dil == QBLOCK
    halo = QBLOCK * dil
    tile = min(tile, seq)
    assert tile % halo == 0 and seq % tile == 0
    nt, nh = seq // tile, seq // halo
    cq, ck, cv = ((c + g * ATT_W) // LANES for c in (C_Q, C_K, C_V))

    def cur(col):
        return pl.BlockSpec((tile, LANES), lambda b, n, p: (b * nt + n, col + p))

    def prev(col):
        return pl.BlockSpec((halo, LANES),
                            lambda b, n, p: (b * nh + jnp.maximum(n * (tile // halo) - 1, 0), col + p))

    out_spec = pl.BlockSpec((tile, LANES), lambda b, n, p: (b * nt + n, p))
    out_sds = jax.ShapeDtypeStruct((batch * seq, ATT_W), F32)
    return pl.pallas_call(
        functools.partial(_band_kernel, dil=dil, tile=tile),
        grid=(batch, nt, ATT_W // LANES),
        in_specs=[pl.BlockSpec(memory_space=pltpu.SMEM),
                  cur(cq), prev(ck), cur(ck), prev(cv), cur(cv)],
        out_specs=[out_spec, out_spec, out_spec],
        out_shape=[out_sds, out_sds, out_sds],
        compiler_params=_cparams(("arbitrary", "arbitrary", "arbitrary")),
        name=f"band_attn_g{g}",
    )(_alibi_slopes(g), proj, proj, proj, proj, proj)


def _decode_kernel(sl_ref, q_ref, kn_ref, vn_ref, *refs, t_new, hps):
    n_g = len(ATT_GROUPS)
    kb, vb = refs[0:2 * n_g:2], refs[1:2 * n_g:2]
    o_ref = refs[2 * n_g]
    ko, vo = refs[2 * n_g + 1::2], refs[2 * n_g + 2::2]
    hh = pl.program_id(1)
    ti_n = lax.broadcasted_iota(jnp.int32, (t_new, NEW_PAD), 0)
    tj_n = lax.broadcasted_iota(jnp.int32, (t_new, NEW_PAD), 1)
    pi = lax.broadcasted_iota(jnp.int32, (NEW_PAD, LANES), 0)
    pj = lax.broadcasted_iota(jnp.int32, (NEW_PAD, LANES), 1)
    place = jnp.logical_and(pj == pi + (LANES - t_new), pi < t_new).astype(BF16)
    tail = lax.broadcasted_iota(jnp.int32, (HEAD_DIM, LANES), 1) >= LANES - t_new
    nt = (((1,), (1,)), ((), ()))
    tn = (((0,), (0,)), ((), ()))
    zpad = jnp.zeros((NEW_PAD - t_new, HEAD_DIM), F32)

    combos = [(j, g) for j in range(hps) for g in range(n_g)]
    d_new = ti_n - tj_n
    kn = {c: jnp.concatenate([kn_ref[c[1], c[0]], zpad], axis=0) for c in combos}
    vn = {c: jnp.concatenate([vn_ref[c[1], c[0]], zpad], axis=0) for c in combos}
    q = {c: q_ref[c[1], c[0]].astype(BF16) for c in combos}
    s_buf = {(j, g): jnp.dot(q[j, g], kb[g][j].astype(BF16), preferred_element_type=F32)
             for j, g in combos}
    s_new = {c: lax.dot_general(q[c], kn[c].astype(BF16), nt, preferred_element_type=F32)
             for c in combos}
    p_buf, p_new, stats = {}, {}, {}
    for j, g in combos:
        wb, dil = ATT_GROUPS[g]
        slope = sl_ref[g * HPG + hh * hps + j]
        wi = lax.broadcasted_iota(jnp.int32, (t_new, wb), 1)
        ti = lax.broadcasted_iota(jnp.int32, (t_new, wb), 0)
        d_buf = wb + ti - wi
        ok_buf = wi >= ti
        ok_new = d_new >= 0
        if dil > 1:
            ok_buf = jnp.logical_and(ok_buf, (d_buf & (dil - 1)) == 0)
            ok_new = jnp.logical_and(ok_new, (d_new & (dil - 1)) == 0)
        sb = jnp.where(ok_buf, s_buf[j, g] * (1.0 / 8.0) - slope * d_buf.astype(F32), -jnp.inf)
        sn = jnp.where(ok_new, s_new[j, g] * (1.0 / 8.0) - slope * d_new.astype(F32), -jnp.inf)
        m = jnp.maximum(jnp.max(sb, axis=-1, keepdims=True), jnp.max(sn, axis=-1, keepdims=True))
        pb, pn = jnp.exp(sb - m), jnp.exp(sn - m)
        p_buf[j, g], p_new[j, g] = pb.astype(BF16), pn.astype(BF16)
        stats[j, g] = (m, jnp.sum(pb, axis=-1, keepdims=True) + jnp.sum(pn, axis=-1, keepdims=True))
    acc = {(j, g): lax.dot_general(p_buf[j, g], vb[g][j].astype(BF16), nt, preferred_element_type=F32)
           + jnp.dot(p_new[j, g], vn[j, g].astype(BF16), preferred_element_type=F32)
           for j, g in combos}
    for j in range(hps):
        m_all = functools.reduce(jnp.maximum, [stats[j, g][0] for g in range(n_g)])
        wts = [jnp.exp(stats[j, g][0] - m_all) for g in range(n_g)]
        num = sum(wts[g] * acc[j, g] for g in range(n_g))
        den = sum(wts[g] * stats[j, g][1] for g in range(n_g))
        o_ref[j] = num / den

    for j, g in combos:
        wb = ATT_GROUPS[g][0]
        for src, new, dst in ((kb[g], kn[j, g], ko[g]), (vb[g], vn[j, g], vo[g])):
            rolled = pltpu.roll(src[j], wb - t_new, 1)
            hi = new.astype(BF16)
            lo = (new - hi.astype(F32)).astype(BF16)
            placed = (lax.dot_general(hi, place, tn, preferred_element_type=F32)
                      + lax.dot_general(lo, place, tn, preferred_element_type=F32))
            if wb > LANES:
                dst[j, :, 0:wb - LANES] = rolled[:, 0:wb - LANES]
            dst[j, :, wb - LANES:wb] = jnp.where(tail, placed, rolled[:, wb - LANES:wb])


def _decode_attention(proj, caches, dec_batch, t_new):
    n_g = len(ATT_GROUPS)
    hps = HPG // 2
    assert all(k.shape[1] == w for (k, _), (w, _) in zip(caches, ATT_GROUPS))

    def heads(col0):
        x = proj[:, col0:col0 + QKV].reshape(dec_batch, t_new, n_g, HPG, HEAD_DIM)
        return jnp.transpose(x, (0, 2, 3, 1, 4))

    q, kn, vn = heads(C_Q), heads(C_K), heads(C_V)
    native = [jnp.transpose(c, (0, 2, 3, 1)) for kv in caches for c in kv]
    slopes = jnp.concatenate([_alibi_slopes(g) for g in range(n_g)])

    new_q = pl.BlockSpec((None, n_g, hps, t_new, HEAD_DIM), lambda b, h: (b, 0, h, 0, 0))
    cache_specs = [pl.BlockSpec((None, hps, HEAD_DIM, c.shape[-1]), lambda b, h: (b, h, 0, 0))
                   for c in native]
    outs = pl.pallas_call(
        functools.partial(_decode_kernel, t_new=t_new, hps=hps),
        grid=(dec_batch, HPG // hps),
        in_specs=[pl.BlockSpec(memory_space=pltpu.SMEM), new_q, new_q, new_q] + cache_specs,
        out_specs=[pl.BlockSpec((None, hps, t_new, HEAD_DIM), lambda b, h: (b, h, 0, 0))] + cache_specs,
        out_shape=([jax.ShapeDtypeStruct((dec_batch, HPG, t_new, HEAD_DIM), F32)]
                   + [jax.ShapeDtypeStruct(c.shape, F32) for c in native]),
        compiler_params=_cparams(("arbitrary", "arbitrary")),
        name="decode_attn",
    )(slopes, q, kn, vn, *native)
    o_a = jnp.transpose(outs[0], (0, 2, 1, 3)).reshape(dec_batch * t_new, ATT_W)
    new = [jnp.transpose(c, (0, 3, 1, 2)) for c in outs[1:]]
    return o_a, new[0::2], new[1::2]


def _prep_kernel(cur_ref, wa_ref, pcur_ref, pwa_ref, fcur_ref, fwa_ref, mu_ref, muwa_ref, lora_ref,
                 w0_ref, a0_ref, kk_ref, ka_ref, rk_ref, bd_ref,
                 r_o, lw_o, k_o, v_o, kkn_o, b_o, bonus_o, *, seq_len, tm):
    i = pl.program_id(0)
    row = lax.broadcasted_iota(jnp.int32, (tm, 1), 0)

    def shifted(cur, p8, first):
        rolled = pltpu.roll(cur, 1, 0)
        if seq_len >= tm:
            at_start = (i * tm) % seq_len == 0
            edge = jnp.where(at_start, first, p8[7:8, :])
            return jnp.where(row == 0, edge, rolled)
        return jnp.where(row % seq_len == 0, first, rolled)

    cur = cur_ref[...]
    wa = wa_ref[...]
    xs = cur + (shifted(cur, pcur_ref[...], fcur_ref[...]) - cur) * mu_ref[...]
    xwa = wa + (shifted(wa, pwa_ref[...], fwa_ref[...]) - wa) * muwa_ref[...]
    r = xs[:, 0:RW_W]
    kr = xs[:, RW_W:2 * RW_W]
    vr = xs[:, 2 * RW_W:3 * RW_W]
    lane = lax.broadcasted_iota(jnp.int32, (1, LANES), 1)
    lin = jnp.where(lane < LORA, jnp.tanh(xwa), xwa).astype(BF16)
    lo = jnp.dot(lin, lora_ref[...], preferred_element_type=F32)
    z = -(w0_ref[...] + lo[:, 0:RW_W])
    softplus = jnp.maximum(z, 0.0) + jnp.log1p(jnp.exp(-jnp.abs(z)))
    lw = -jnp.exp(-softplus - 0.5)
    a = jax.nn.sigmoid(a0_ref[...] + lo[:, RW_W:2 * RW_W])
    kk = kr * kk_ref[...]
    k_mod = kr * (1.0 + (a - 1.0) * ka_ref[...])
    rkk = r * k_mod * rk_ref[...]
    bd = bd_ref[...]
    r_o[...] = r
    lw_o[...] = lw
    k_o[...] = k_mod
    v_o[...] = vr
    for c in range(RW_W // LANES):
        cols = slice(c * LANES, (c + 1) * LANES)
        kc = kk[:, cols]
        nrm = jnp.maximum(jnp.sqrt(_segsum(kc * kc, bd)), L2_EPS)
        kkn = kc / nrm
        kkn_o[:, cols] = kkn
        b_o[:, cols] = kkn * a[:, cols]
        bonus_o[:, cols] = _segsum(rkk[:, cols], bd) * vr[:, cols]


def _rwkv_prep(proj, first, mu, lora_w, w0, a0, k_k, k_a, r_k, bd, seq_len, tm):
    n = proj.shape[0]
    main_w = 3 * RW_W
    cb, wb = C_SH // main_w, (C_SH + main_w) // LANES
    per_seq_first = seq_len >= tm
    if per_seq_first:
        f_main = first[:, None, 0:main_w]
        f_wa = first[:, None, main_w:]
        fm_spec = pl.BlockSpec((None, 1, main_w), lambda i: ((i * tm) // seq_len, 0, 0))
        fw_spec = pl.BlockSpec((None, 1, LANES), lambda i: ((i * tm) // seq_len, 0, 0))
    else:
        rep = jnp.repeat(first, seq_len, axis=0)
        f_main, f_wa = rep[:, 0:main_w], rep[:, main_w:]
        fm_spec = pl.BlockSpec((tm, main_w), lambda i: (i, 0))
        fw_spec = pl.BlockSpec((tm, LANES), lambda i: (i, 0))
    vec = lambda w: pl.BlockSpec((1, w), lambda i: (0, 0))
    out_spec = pl.BlockSpec((tm, RW_W), lambda i: (i, 0))
    out_sds = jax.ShapeDtypeStruct((n, RW_W), F32)
    return pl.pallas_call(
        functools.partial(_prep_kernel, seq_len=seq_len, tm=tm),
        grid=(n // tm,),
        in_specs=[
            pl.BlockSpec((tm, main_w), lambda i: (i, cb)),
            pl.BlockSpec((tm, LANES), lambda i: (i, wb)),
            pl.BlockSpec((8, main_w), lambda i: (jnp.maximum(i * (tm // 8) - 1, 0), cb)),
            pl.BlockSpec((8, LANES), lambda i: (jnp.maximum(i * (tm // 8) - 1, 0), wb)),
            fm_spec, fw_spec,
            vec(main_w), vec(LANES),
            pl.BlockSpec((LANES, 2 * RW_W), lambda i: (0, 0)),
            vec(RW_W), vec(RW_W), vec(RW_W), vec(RW_W), vec(RW_W),
            pl.BlockSpec((LANES, LANES), lambda i: (0, 0)),
        ],
        out_specs=[out_spec] * 7,
        out_shape=[out_sds] * 7,
        compiler_params=_cparams(("arbitrary",)),
        name="rwkv_prep",
    )(proj, proj, proj, proj, f_main, f_wa, mu[:, 0:main_w], mu[:, main_w:], lora_w,
      w0, a0, k_k, k_a, r_k, bd)


def _mm(a, b):
    return jnp.dot(a.astype(BF16), b.astype(BF16), preferred_element_type=F32)


def _mm_nt(a, b):
    return lax.dot_general(a.astype(BF16), b.astype(BF16), (((1,), (1,)), ((), ())),
                           preferred_element_type=F32)


def _mm_tn(a, b):
    return lax.dot_general(a.astype(BF16), b.astype(BF16), (((0,), (0,)), ((), ())),
                           preferred_element_type=F32)


def _scan_kernel(r_ref, lw_ref, k_ref, v_ref, kk_ref, b_ref, y_ref, zt_ref, z_scr, *, nb):
    c = pl.program_id(0)
    n_pairs = RW_W // LANES
    C = CHUNK
    levels = (C - 1).bit_length()

    @pl.when(c == 0)
    def _():
        z_scr[...] = jnp.zeros_like(z_scr)

    ti = lax.broadcasted_iota(jnp.int32, (C, C), 0)
    tj = lax.broadcasted_iota(jnp.int32, (C, C), 1)
    tri = (ti >= tj).astype(F32)
    lane = lax.broadcasted_iota(jnp.int32, (1, LANES), 1)
    in_a = lane < HEAD_DIM
    si = lax.broadcasted_iota(jnp.int32, (2 * C, 2 * C), 0)
    sj = lax.broadcasted_iota(jnp.int32, (2 * C, 2 * C), 1)
    same = (si // C) == (sj // C)
    strict = jnp.logical_and(same, (si % C) > (sj % C))
    incl = jnp.logical_and(same, (si % C) >= (sj % C))
    eye = (si == sj).astype(F32)

    def stack(x):
        return jnp.concatenate([jnp.where(in_a, x, 0.0), jnp.where(in_a, 0.0, x)], axis=0)

    chains = []
    for bi in range(nb):
        r, lw, k, v, kk, b = (ref[bi] for ref in (r_ref, lw_ref, k_ref, v_ref, kk_ref, b_ref))
        cum = jnp.dot(tri, lw, precision=lax.Precision.HIGHEST, preferred_element_type=F32)
        cum_end = cum[C - 1:C, :]
        w_in = jnp.exp(-cum)
        w_rest = jnp.exp(cum_end - cum)
        a_t = jnp.exp(cum - lw) * (-kk)
        b_t = b * w_in
        k_t = k * w_in
        r_t = r * jnp.exp(cum)
        b_h = b * w_rest
        k_h = k * w_rest
        w_end = jnp.exp(cum_end)
        for p in range(n_pairs):
            cols = slice(p * LANES, (p + 1) * LANES)
            ch = {name: stack(x[:, cols]) for name, x in
                  (("a", a_t), ("b", b_t), ("k", k_t), ("r", r_t), ("v", v), ("bh", b_h), ("kh", k_h))}
            ch["w_end"] = w_end[:, cols]
            ch["z"] = z_scr[bi * n_pairs + p]
            chains.append(ch)

    n = len(chains)
    g = [_mm_nt(jnp.concatenate([ch["a"], ch["r"]], axis=0), jnp.concatenate([ch["b"], ch["k"]], axis=0))
         for ch in chains]
    ab = [jnp.where(strict, x[0:2 * C, 0:2 * C], 0.0) for x in g]
    ak = [jnp.where(strict, x[0:2 * C, 2 * C:4 * C], 0.0) for x in g]
    rbk = [jnp.concatenate([jnp.where(incl, x[2 * C:4 * C, 0:2 * C], 0.0),
                            jnp.where(incl, x[2 * C:4 * C, 2 * C:4 * C], 0.0)], axis=1) for x in g]
    akv = [_mm(ak[i], chains[i]["v"]) for i in range(n)]
    tinv = [eye + x for x in ab]
    lp = ab
    for _ in range(levels - 1):
        lp = [_mm(x, x) for x in lp]
        tinv = [tinv[i] + _mm(tinv[i], lp[i]) for i in range(n)]
    au = [_mm(tinv[i], jnp.concatenate([chains[i]["a"], akv[i]], axis=1)) for i in range(n)]
    zeros = jnp.zeros((2 * C, LANES), F32)
    rhs = [jnp.concatenate([au[i], jnp.concatenate([zeros, chains[i]["v"]], axis=1)], axis=0)
           for i in range(n)]
    mn = [_mm_tn(jnp.concatenate([chains[i]["bh"], chains[i]["kh"]], axis=0), rhs[i])
          for i in range(n)]
    ry = [_mm(rbk[i], rhs[i]) for i in range(n)]
    yz = [_mm(jnp.concatenate([chains[i]["r"] + ry[i][:, 0:LANES],
                               mn[i][:, 0:LANES] + eye * chains[i]["w_end"]], axis=0), chains[i]["z"])
          for i in range(n)]
    z_news = [yz[i][2 * C:2 * C + LANES] + mn[i][:, LANES:2 * LANES] for i in range(n)]
    y_st = [yz[i][0:2 * C] + ry[i][:, LANES:2 * LANES] for i in range(n)]
    y_pair = [y[0:C] + y[C:2 * C] for y in y_st]
    y_ref[...] = jnp.stack([jnp.concatenate(y_pair[bi * n_pairs:(bi + 1) * n_pairs], axis=1)
                            for bi in range(nb)], axis=0)
    z_scr[...] = jnp.stack(z_news, axis=0)
    zt_ref[...] = jnp.stack([z.T for z in z_news], axis=0)


def _rwkv_scan(prep, batch, seq):
    assert seq % CHUNK == 0
    n_pairs = RW_W // LANES
    tok = pl.BlockSpec((batch, CHUNK, RW_W), lambda c: (0, c, 0))
    st = pl.BlockSpec((batch * n_pairs, LANES, LANES), lambda c: (0, 0, 0))
    y, zt = pl.pallas_call(
        functools.partial(_scan_kernel, nb=batch),
        grid=(seq // CHUNK,),
        in_specs=[tok] * 6,
        out_specs=[tok, st],
        out_shape=[jax.ShapeDtypeStruct((batch, seq, RW_W), F32),
                   jax.ShapeDtypeStruct((batch * n_pairs, LANES, LANES), F32)],
        scratch_shapes=[pltpu.VMEM((batch * n_pairs, LANES, LANES), F32)],
        compiler_params=_cparams(("arbitrary",)),
        name="rwkv_scan",
    )(*[x.reshape(batch, seq, RW_W) for x in prep])
    zt = zt.reshape(batch, n_pairs, LANES, LANES)
    s_a = zt[:, :, 0:HEAD_DIM, 0:HEAD_DIM]
    s_b = zt[:, :, HEAD_DIM:, HEAD_DIM:]
    state = jnp.stack([s_a, s_b], axis=2).reshape(batch, RWKV_HEADS, HEAD_DIM, HEAD_DIM)
    return y.reshape(batch * seq, RW_W), state


def _seq_scan_kernel(r_ref, lw_ref, k_ref, v_ref, kk_ref, b_ref, s_ref, y_ref, so_ref, w_scr, *, t_new):
    w_scr[...] = jnp.exp(lw_ref[...])

    def body(vi, carry):
        sv = s_ref[vi]
        for t in range(t_new):
            sa = jnp.sum(sv * kk_ref[t], axis=0, keepdims=True)
            sv = sv * w_scr[t] - sa * b_ref[t] + v_ref[t, pl.ds(vi, 1), :] * k_ref[t]
            y_ref[t, pl.ds(vi, 1), :] = jnp.sum(sv * r_ref[t], axis=0, keepdims=True)
        so_ref[vi] = sv
        return carry

    lax.fori_loop(0, HEAD_DIM, body, 0)


def _seq_scan(prep, state, dec_batch, t_new):
    def lanes(x):
        return jnp.transpose(x.reshape(dec_batch, t_new, RWKV_HEADS, HEAD_DIM), (1, 2, 3, 0))

    vec = pl.BlockSpec((t_new, None, HEAD_DIM, dec_batch), lambda h: (0, h, 0, 0))
    st = pl.BlockSpec((None, HEAD_DIM, HEAD_DIM, dec_batch), lambda h: (h, 0, 0, 0))
    s_native = jnp.transpose(state, (1, 2, 3, 0))
    y, s_new = pl.pallas_call(
        functools.partial(_seq_scan_kernel, t_new=t_new),
        grid=(RWKV_HEADS,),
        in_specs=[vec] * 6 + [st],
        out_specs=[vec, st],
        out_shape=[jax.ShapeDtypeStruct((t_new, RWKV_HEADS, HEAD_DIM, dec_batch), F32),
                   jax.ShapeDtypeStruct(s_native.shape, F32)],
        scratch_shapes=[pltpu.VMEM((t_new, HEAD_DIM, dec_batch), F32)],
        compiler_params=_cparams(("arbitrary",)),
        name="rwkv_seq_scan",
    )(*[lanes(x) for x in prep], s_native)
    y = jnp.transpose(y, (3, 0, 1, 2)).reshape(dec_batch * t_new, RW_W)
    return y, jnp.transpose(s_new, (3, 0, 1, 2))


def _final_kernel(*refs, n_parts):
    att = refs[0:3 * n_parts] if n_parts else refs[0:1]
    n_att = len(att)
    (y_ref, bonus_ref, gates_ref, x_ref, lnw_ref, lnb_ref, pa_ref, pb_ref, wo_ref, bd_ref,
     o_ref) = refs[n_att:]
    if n_parts:
        parts = [(att[3 * g][...], att[3 * g + 1][...], att[3 * g + 2][...]) for g in range(n_parts)]
        m_all = functools.reduce(jnp.maximum, [m for _, m, _ in parts])
        num = sum(jnp.exp(m - m_all) * a for a, m, _ in parts)
        den = sum(jnp.exp(m - m_all) * l for _, m, l in parts)
        o_a = num / den
    else:
        o_a = att[0][...]
    gates = gates_ref[...]
    z_a = gates[:, 0:ATT_W]
    z_b = gates[:, ATT_W:ATT_W + RW_W]
    g_a = gates[:, ATT_W + RW_W:ATT_W + RW_W + 1024]
    g_b = gates[:, ATT_W + RW_W + 1024:]
    y = y_ref[...]
    bd = bd_ref[...]
    cols_out = []
    for c in range(RW_W // LANES):
        cols = slice(c * LANES, (c + 1) * LANES)
        yc = y[:, cols]
        mu = _segsum(yc, bd) * (1.0 / HEAD_DIM)
        d = yc - mu
        var = _segsum(d * d, bd) * (1.0 / HEAD_DIM)
        cols_out.append(d * lax.rsqrt(var + GN_EPS))
    yn = jnp.concatenate(cols_out, axis=1)
    o_b = yn * lnw_ref[...] + lnb_ref[...] + bonus_ref[...]
    h_a = jnp.dot((o_a * jax.nn.silu(z_a)).astype(BF16), pa_ref[...], preferred_element_type=F32)
    h_b = jnp.dot((o_b * jax.nn.silu(z_b)).astype(BF16), pb_ref[...], preferred_element_type=F32)
    mixed = jax.nn.sigmoid(g_a) * h_a + jax.nn.sigmoid(g_b) * h_b
    o_ref[...] = x_ref[...] + jnp.dot(mixed.astype(BF16), wo_ref[...], preferred_element_type=F32)


def _final(att, y_raw, bonus, proj, x, ln_w, ln_b, p_a, p_b, w_out, bd, tm):
    n, d = x.shape
    tm = min(tm, n)
    n_parts = len(att) // 3 if len(att) > 1 else 0
    gates_w = C_Q - C_GATES
    row = lambda w: pl.BlockSpec((tm, w), lambda i: (i, 0))
    full = lambda a: pl.BlockSpec(a.shape, lambda i: (0, 0))
    return pl.pallas_call(
        functools.partial(_final_kernel, n_parts=n_parts),
        grid=(n // tm,),
        in_specs=([row(ATT_W)] * len(att)
                  + [row(RW_W), row(RW_W), row(gates_w), row(d),
                     full(ln_w), full(ln_b), full(p_a), full(p_b), full(w_out), full(bd)]),
        out_specs=row(d),
        out_shape=jax.ShapeDtypeStruct((n, d), F32),
        compiler_params=_cparams(("arbitrary",)),
        name="gated_out",
    )(*att, y_raw, bonus, proj, x, ln_w, ln_b, p_a, p_b, w_out, bd)


def _permute_w_in(w_in):
    q, k, v, sh, gates = (w_in[:, 0:QKV], w_in[:, QKV:2 * QKV], w_in[:, 2 * QKV:3 * QKV],
                          w_in[:, 3 * QKV:3 * QKV + SHIFT_COLS], w_in[:, 3 * QKV + SHIFT_COLS:])
    pad = jnp.zeros((w_in.shape[0], P_COLS - IN_COLS), w_in.dtype)
    return jnp.concatenate([gates, q, k, v, sh, pad], axis=1).astype(BF16)


def _layer(x, batch, seq, shift_prev, state, caches, lw, tm_proj, tm_prep, tm_out):
    (norm_g, w_perm, shift_mu, qg, kg, lora_w, w0, a0, k_k, k_a, r_k, ln_w, ln_b, p_a, p_b, w_out,
     bd) = lw
    proj = _inproj(x, norm_g, w_perm, qg, kg, bd, tm_proj)
    prep = _rwkv_prep(proj, shift_prev, shift_mu, lora_w, w0, a0, k_k, k_a, r_k, bd, seq, tm_prep)
    if caches is None:
        att = []
        for g in range(len(ATT_GROUPS)):
            att += _band_attention(proj, batch, seq, g, tile=2048)
        pv = proj.reshape(batch, seq, P_COLS)
        new_k, new_v = [], []
        for g, (window, _) in enumerate(ATT_GROUPS):
            keep = min(window, seq)
            new_k.append(pv[:, seq - keep:, C_K + g * ATT_W:C_K + (g + 1) * ATT_W]
                         .reshape(batch, keep, HPG, HEAD_DIM))
            new_v.append(pv[:, seq - keep:, C_V + g * ATT_W:C_V + (g + 1) * ATT_W]
                         .reshape(batch, keep, HPG, HEAD_DIM))
        y_raw, new_state = _rwkv_scan(prep[:6], batch, seq)
    else:
        o_a, new_k, new_v = _decode_attention(proj, caches, batch, seq)
        att = [o_a]
        y_raw, new_state = _seq_scan(prep[:6], state, batch, seq)
    y = _final(att, y_raw, prep[6], proj, x, ln_w, ln_b, p_a, p_b, w_out, bd, tm_out)
    new_shift = proj.reshape(batch, seq, P_COLS)[:, seq - 1, C_SH:C_SH + SHIFT_COLS]
    attn_state = [c for kv in zip(new_k, new_v) for c in kv]
    return y, (*attn_state, new_state, new_shift)


def kernel(x_prompt, x_sample, cache_k_g0, cache_v_g0, cache_k_g1, cache_v_g1, cache_k_g2, cache_v_g2, state_rwkv, state_shift, norm_g, w_in, shift_mu, q_norm_g, k_norm_g, w0, w2, a0, a2, k_k, k_a, r_k, ln_x_w, ln_x_b, p_a, p_b, w_out):
    depth = norm_g.shape[0]
    batch, seq, d_model = x_prompt.shape
    dec_batch, dec_seq, _ = x_sample.shape
    bd = _bd_ones()
    h_p = x_prompt.reshape(batch * seq, d_model)
    h_s = x_sample.reshape(dec_batch * dec_seq, d_model)
    prompt_states, sample_states = [], []
    for layer in range(depth):
        zero = jnp.zeros((LORA, RW_W), F32)
        lora_w = jnp.concatenate([jnp.concatenate([w2[layer], zero], axis=1),
                                  jnp.concatenate([zero, a2[layer]], axis=1)], axis=0).astype(BF16)
        row = lambda a: a.reshape(1, -1)
        lw = (row(norm_g[layer]), _permute_w_in(w_in[layer]), row(shift_mu[layer]),
              jnp.tile(row(q_norm_g[layer]), (1, LANES // HEAD_DIM)),
              jnp.tile(row(k_norm_g[layer]), (1, LANES // HEAD_DIM)),
              lora_w, row(w0[layer]), row(a0[layer]), row(k_k[layer]), row(k_a[layer]),
              row(r_k[layer]), row(ln_x_w[layer]), row(ln_x_b[layer]),
              p_a[layer].astype(BF16), p_b[layer].astype(BF16), w_out[layer].astype(BF16), bd)
        zero_shift = jnp.zeros((batch, SHIFT_COLS), F32)
        h_p, st_p = _layer(h_p, batch, seq, zero_shift, None, None, lw,
                           tm_proj=2048, tm_prep=512, tm_out=256)
        caches = [(cache_k_g0[layer], cache_v_g0[layer]), (cache_k_g1[layer], cache_v_g1[layer]),
                  (cache_k_g2[layer], cache_v_g2[layer])]
        n_s = dec_batch * dec_seq
        h_s, st_s = _layer(h_s, dec_batch, dec_seq, state_shift[layer], state_rwkv[layer], caches, lw,
                           tm_proj=min(1024, n_s), tm_prep=min(512, n_s), tm_out=256)
        prompt_states.append(st_p)
        sample_states.append(st_s)
    ps = [jnp.stack(t) for t in zip(*prompt_states)]
    ss = [jnp.stack(t) for t in zip(*sample_states)]
    return (h_p.reshape(batch, seq, d_model), h_s.reshape(dec_batch, dec_seq, d_model),
            *ps, *ss)
```

```python
import functools

import jax
import jax.numpy as jnp
from jax import lax
from jax.experimental import pallas as pl
from jax.experimental.pallas import tpu as pltpu

F32 = jnp.float32
BF16 = jnp.bfloat16

HEAD_DIM = 64
HPG = 8
ATT_GROUPS = ((128, 1), (512, 4), (2048, 16))
N_ATT_HEADS = 24
RWKV_HEADS = 8
LORA = 64
QBLOCK = 128
RMS_EPS = 1e-6
GN_EPS = 64e-5
L2_EPS = 1e-12
CHUNK = 64

LANES = 128
NEW_PAD = 16
VMEM_LIMIT = 52 * 1024 * 1024

QKV = N_ATT_HEADS * HEAD_DIM
ATT_W = HPG * HEAD_DIM
RW_W = RWKV_HEADS * HEAD_DIM
C_GATES = 0
C_Q = 3072
C_K = C_Q + QKV
C_V = C_K + QKV
C_SH = C_V + QKV
SHIFT_COLS = 3 * RW_W + 2 * LORA
IN_COLS = C_SH + SHIFT_COLS
TN = 512
P_COLS = pl.cdiv(IN_COLS, TN) * TN


def _cparams(sem):
    return pltpu.CompilerParams(dimension_semantics=sem, vmem_limit_bytes=VMEM_LIMIT)


def _segsum(x, bd):
    hi = x.astype(BF16)
    lo = (x - hi.astype(F32)).astype(BF16)
    return (jnp.dot(hi, bd, preferred_element_type=F32)
            + jnp.dot(lo, bd, preferred_element_type=F32))


def _bd_ones():
    i = jnp.arange(LANES)
    return (i[:, None] // HEAD_DIM == i[None, :] // HEAD_DIM).astype(BF16)


def _alibi_slopes(g):
    return jnp.exp2(-8.0 * (jnp.arange(HPG, dtype=F32) + (g * HPG + 1)) / N_ATT_HEADS)


def _inproj_kernel(x_ref, g_ref, w_ref, qg_ref, kg_ref, bd_ref, o_ref, xn_ref):
    j = pl.program_id(1)

    @pl.when(j == 0)
    def _():
        x = x_ref[...]
        ms = jnp.mean(x * x, axis=-1, keepdims=True)
        xn_ref[...] = (x * lax.rsqrt(ms + RMS_EPS) * g_ref[...]).astype(BF16)

    t = jnp.dot(xn_ref[...], w_ref[...], preferred_element_type=F32)
    q_lo, k_lo, k_hi = C_Q // TN, C_K // TN, C_V // TN

    @pl.when(jnp.logical_or(j < q_lo, j >= k_hi))
    def _():
        o_ref[...] = t

    @pl.when(jnp.logical_and(j >= q_lo, j < k_hi))
    def _():
        gain = jnp.where(j < k_lo, qg_ref[...], kg_ref[...])
        bd = bd_ref[...]
        for c in range(TN // LANES):
            tc = t[:, c * LANES:(c + 1) * LANES]
            ms = _segsum(tc * tc, bd) * (1.0 / HEAD_DIM)
            o_ref[:, c * LANES:(c + 1) * LANES] = tc * lax.rsqrt(ms + RMS_EPS) * gain


def _inproj(x, norm_g, w_perm, qg, kg, bd, tm):
    n, d = x.shape
    return pl.pallas_call(
        _inproj_kernel,
        grid=(n // tm, P_COLS // TN),
        in_specs=[
            pl.BlockSpec((tm, d), lambda i, j: (i, 0)),
            pl.BlockSpec((1, d), lambda i, j: (0, 0)),
            pl.BlockSpec((d, TN), lambda i, j: (0, j)),
            pl.BlockSpec((1, LANES), lambda i, j: (0, 0)),
            pl.BlockSpec((1, LANES), lambda i, j: (0, 0)),
            pl.BlockSpec((LANES, LANES), lambda i, j: (0, 0)),
        ],
        out_specs=pl.BlockSpec((tm, TN), lambda i, j: (i, j)),
        out_shape=jax.ShapeDtypeStruct((n, P_COLS), F32),
        scratch_shapes=[pltpu.VMEM((tm, d), BF16)],
        compiler_params=_cparams(("arbitrary", "arbitrary")),
        name="inproj",
    )(x, norm_g, w_perm, qg, kg, bd)


def _band_kernel(sl_ref, q_ref, kp_ref, kc_ref, vp_ref, vc_ref, acc_ref, m_ref, l_ref, *, dil, tile):
    n = pl.program_id(1)
    p = pl.program_id(2)
    lane = lax.broadcasted_iota(jnp.int32, (1, LANES), 1)
    mask_a = lane < HEAD_DIM
    qi = lax.broadcasted_iota(jnp.int32, (QBLOCK, 2 * QBLOCK), 0)
    ki = lax.broadcasted_iota(jnp.int32, (QBLOCK, 2 * QBLOCK), 1)
    rel = qi + QBLOCK - ki
    band = jnp.logical_and(rel >= 0, rel <= QBLOCK)
    band_first = jnp.logical_and(band, jnp.logical_or(ki >= QBLOCK, n > 0))
    dist = (rel * dil).astype(F32)

    def rows(r, s):
        if dil == 1:
            return pl.ds(s * QBLOCK, QBLOCK)
        return pl.ds(r + s * QBLOCK * dil, QBLOCK, stride=dil)

    for r in range(dil):
        for s in range(tile // (dil * QBLOCK)):
            valid = band_first if s == 0 else band
            q = q_ref[rows(r, s), :]
            if s == 0:
                k_prev, v_prev = kp_ref[rows(r, 0), :], vp_ref[rows(r, 0), :]
            else:
                k_prev, v_prev = kc_ref[rows(r, s - 1), :], vc_ref[rows(r, s - 1), :]
            k = jnp.concatenate([k_prev, kc_ref[rows(r, s), :]], axis=0).astype(BF16)
            v = jnp.concatenate([v_prev, vc_ref[rows(r, s), :]], axis=0).astype(BF16)
            outs = []
            for h in range(2):
                hm = mask_a if h == 0 else jnp.logical_not(mask_a)
                qh = jnp.where(hm, q, 0.0).astype(BF16)
                sc = lax.dot_general(qh, k, (((1,), (1,)), ((), ())), preferred_element_type=F32)
                sc = sc * (1.0 / 8.0) - sl_ref[2 * p + h] * dist
                sc = jnp.where(valid, sc, -jnp.inf)
                m = jnp.max(sc, axis=-1, keepdims=True)
                pr = jnp.exp(sc - m)
                l = jnp.sum(pr, axis=-1, keepdims=True)
                acc = jnp.dot(pr.astype(BF16), v, preferred_element_type=F32)
                outs.append((acc, m, l))
            (acc0, m0, l0), (acc1, m1, l1) = outs
            acc_ref[rows(r, s), :] = jnp.where(mask_a, acc0, acc1)
            m_ref[rows(r, s), :] = jnp.where(mask_a, m0, m1)
            l_ref[rows(r, s), :] = jnp.where(mask_a, l0, l1)


def _band_attention(proj, batch, seq, g, tile):
    window, dil = ATT_GROUPS[g]
    assert window // dil == QBLOCK
    halo = QBLOCK * dil
    tile = min(tile, seq)
    assert tile % halo == 0 and seq % tile == 0
    nt, nh = seq // tile, seq // halo
    cq, ck, cv = ((c + g * ATT_W) // LANES for c in (C_Q, C_K, C_V))

    def cur(col):
        return pl.BlockSpec((tile, LANES), lambda b, n, p: (b * nt + n, col + p))

    def prev(col):
        return pl.BlockSpec((halo, LANES),
                            lambda b, n, p: (b * nh + jnp.maximum(n * (tile // halo) - 1, 0), col + p))

    out_spec = pl.BlockSpec((tile, LANES), lambda b, n, p: (b * nt + n, p))
    out_sds = jax.ShapeDtypeStruct((batch * seq, ATT_W), F32)
    return pl.pallas_call(
        functools.partial(_band_kernel, dil=dil, tile=tile),
        grid=(batch, nt, ATT_W // LANES),
        in_specs=[pl.BlockSpec(memory_space=pltpu.SMEM),
                  cur(cq), prev(ck), cur(ck), prev(cv), cur(cv)],
        out_specs=[out_spec, out_spec, out_spec],
        out_shape=[out_sds, out_sds, out_sds],
        compiler_params=_cparams(("arbitrary", "arbitrary", "arbitrary")),
        name=f"band_attn_g{g}",
    )(_alibi_slopes(g), proj, proj, proj, proj, proj)


def _decode_body(sl_ref, q_ref, kn_ref, vn_ref, kb, vb, o_ref, ko, vo, hh, t_new, hps):
    n_g = len(ATT_GROUPS)
    ti_n = lax.broadcasted_iota(jnp.int32, (t_new, NEW_PAD), 0)
    tj_n = lax.broadcasted_iota(jnp.int32, (t_new, NEW_PAD), 1)
    pi = lax.broadcasted_iota(jnp.int32, (NEW_PAD, LANES), 0)
    pj = lax.broadcasted_iota(jnp.int32, (NEW_PAD, LANES), 1)
    place = jnp.logical_and(pj == pi + (LANES - t_new), pi < t_new).astype(BF16)
    tail = lax.broadcasted_iota(jnp.int32, (HEAD_DIM, LANES), 1) >= LANES - t_new
    nt = (((1,), (1,)), ((), ()))
    tn = (((0,), (0,)), ((), ()))
    zpad = jnp.zeros((NEW_PAD - t_new, HEAD_DIM), F32)

    combos = [(j, g) for j in range(hps) for g in range(n_g)]
    d_new = ti_n - tj_n
    kn = {c: jnp.concatenate([kn_ref[c[1], c[0]], zpad], axis=0) for c in combos}
    vn = {c: jnp.concatenate([vn_ref[c[1], c[0]], zpad], axis=0) for c in combos}
    q = {c: q_ref[c[1], c[0]].astype(BF16) for c in combos}
    s_buf = {(j, g): jnp.dot(q[j, g], kb[g][j].astype(BF16), preferred_element_type=F32)
             for j, g in combos}
    s_new = {c: lax.dot_general(q[c], kn[c].astype(BF16), nt, preferred_element_type=F32)
             for c in combos}
    p_buf, p_new, stats = {}, {}, {}
    for j, g in combos:
        wb, dil = ATT_GROUPS[g]
        slope = sl_ref[g * HPG + hh * hps + j]
        wi = lax.broadcasted_iota(jnp.int32, (t_new, wb), 1)
        ti = lax.broadcasted_iota(jnp.int32, (t_new, wb), 0)
        d_buf = wb + ti - wi
        ok_buf = wi >= ti
        ok_new = d_new >= 0
        if dil > 1:
            ok_buf = jnp.logical_and(ok_buf, (d_buf & (dil - 1)) == 0)
            ok_new = jnp.logical_and(ok_new, (d_new & (dil - 1)) == 0)
        sb = jnp.where(ok_buf, s_buf[j, g] * (1.0 / 8.0) - slope * d_buf.astype(F32), -jnp.inf)
        sn = jnp.where(ok_new, s_new[j, g] * (1.0 / 8.0) - slope * d_new.astype(F32), -jnp.inf)
        m = jnp.maximum(jnp.max(sb, axis=-1, keepdims=True), jnp.max(sn, axis=-1, keepdims=True))
        pb, pn = jnp.exp(sb - m), jnp.exp(sn - m)
        p_buf[j, g], p_new[j, g] = pb.astype(BF16), pn.astype(BF16)
        stats[j, g] = (m, jnp.sum(pb, axis=-1, keepdims=True) + jnp.sum(pn, axis=-1, keepdims=True))
    acc = {(j, g): lax.dot_general(p_buf[j, g], vb[g][j].astype(BF16), nt, preferred_element_type=F32)
           + jnp.dot(p_new[j, g], vn[j, g].astype(BF16), preferred_element_type=F32)
           for j, g in combos}
    for j in range(hps):
        m_all = functools.reduce(jnp.maximum, [stats[j, g][0] for g in range(n_g)])
        wts = [jnp.exp(stats[j, g][0] - m_all) for g in range(n_g)]
        num = sum(wts[g] * acc[j, g] for g in range(n_g))
        den = sum(wts[g] * stats[j, g][1] for g in range(n_g))
        o_ref[j] = num / den

    for j, g in combos:
        wb = ATT_GROUPS[g][0]
        for src, new, dst in ((kb[g], kn[j, g], ko[g]), (vb[g], vn[j, g], vo[g])):
            rolled = pltpu.roll(src[j], wb - t_new, 1)
            hi = new.astype(BF16)
            lo = (new - hi.astype(F32)).astype(BF16)
            placed = (lax.dot_general(hi, place, tn, preferred_element_type=F32)
                      + lax.dot_general(lo, place, tn, preferred_element_type=F32))
            if wb > LANES:
                dst[j, :, 0:wb - LANES] = rolled[:, 0:wb - LANES]
            dst[j, :, wb - LANES:wb] = jnp.where(tail, placed, rolled[:, wb - LANES:wb])


def _decode_kernel(sl_ref, q_ref, kn_ref, vn_ref, *refs, t_new, hps):
    n_c = 2 * len(ATT_GROUPS)
    _decode_body(sl_ref, q_ref, kn_ref, vn_ref, refs[0:n_c:2], refs[1:n_c:2], refs[n_c],
                 refs[n_c + 1::2], refs[n_c + 2::2], pl.program_id(0), t_new, hps)


def _decode_scan_kernel(sl_ref, q_ref, kn_ref, vn_ref, *refs, t_new, hps):
    n_c = 2 * len(ATT_GROUPS)
    n_pairs = RW_W // LANES
    caches_in, tok = refs[0:n_c], refs[n_c:n_c + 6]
    o_ref = refs[n_c + 6]
    caches_out = refs[n_c + 7:2 * n_c + 7]
    y_ref, zt_ref, z_scr = refs[2 * n_c + 7:]
    bi, c = pl.program_id(0), pl.program_id(1)
    _decode_body(sl_ref, q_ref, kn_ref, vn_ref, caches_in[0::2], caches_in[1::2], o_ref,
                 caches_out[0::2], caches_out[1::2], bi, t_new, hps)

    @pl.when(c == 0)
    def _():
        z_scr[bi] = jnp.zeros((n_pairs, LANES, LANES), F32)

    zs = [[z_scr[bi, p] for p in range(n_pairs)]]
    ys, z_news = _scan_chunk([tuple(ref[...] for ref in tok)], zs)
    y_ref[...] = ys[0]
    z_scr[bi] = jnp.stack(z_news[0], axis=0)
    zt_ref[...] = jnp.stack([z.T for z in z_news[0]], axis=0)


def _unpair_state(zt, batch):
    zt = zt.reshape(batch, RW_W // LANES, LANES, LANES)
    s_a = zt[:, :, 0:HEAD_DIM, 0:HEAD_DIM]
    s_b = zt[:, :, HEAD_DIM:, HEAD_DIM:]
    return jnp.stack([s_a, s_b], axis=2).reshape(batch, RWKV_HEADS, HEAD_DIM, HEAD_DIM)


def _decode_attention(proj, caches, dec_batch, t_new, scan_prep=None, batch=0, seq=0):
    n_g = len(ATT_GROUPS)
    hps = HPG // 2
    assert all(k.shape[1] == w for (k, _), (w, _) in zip(caches, ATT_GROUPS))

    def heads(col0):
        x = proj[:, col0:col0 + QKV].reshape(dec_batch, t_new, n_g, HPG, HEAD_DIM)
        return jnp.transpose(x, (0, 2, 3, 1, 4))

    q, kn, vn = heads(C_Q), heads(C_K), heads(C_V)
    native = [jnp.transpose(c, (0, 2, 3, 1)) for kv in caches for c in kv]
    slopes = jnp.concatenate([_alibi_slopes(g) for g in range(n_g)])

    new_q = pl.BlockSpec((None, n_g, hps, t_new, HEAD_DIM), lambda h, b: (b, 0, h, 0, 0))
    cache_specs = [pl.BlockSpec((None, hps, HEAD_DIM, c.shape[-1]), lambda h, b: (b, h, 0, 0))
                   for c in native]
    in_specs = [pl.BlockSpec(memory_space=pltpu.SMEM), new_q, new_q, new_q] + cache_specs
    out_specs = [pl.BlockSpec((None, hps, t_new, HEAD_DIM), lambda h, b: (b, h, 0, 0))] + cache_specs
    out_shape = ([jax.ShapeDtypeStruct((dec_batch, HPG, t_new, HEAD_DIM), F32)]
                 + [jax.ShapeDtypeStruct(c.shape, F32) for c in native])
    grid = (HPG // hps, dec_batch)
    if scan_prep is None:
        outs = pl.pallas_call(
            functools.partial(_decode_kernel, t_new=t_new, hps=hps),
            grid=grid, in_specs=in_specs, out_specs=out_specs, out_shape=out_shape,
            compiler_params=_cparams(("arbitrary", "arbitrary")),
            name="decode_attn",
        )(slopes, q, kn, vn, *native)
        scan_out = ()
    else:
        assert grid == (batch, seq // CHUNK)
        n_pairs = RW_W // LANES
        tok = pl.BlockSpec((None, CHUNK, RW_W), lambda bi, c: (bi, c, 0))
        st = pl.BlockSpec((n_pairs, LANES, LANES), lambda bi, c: (bi, 0, 0))
        outs = pl.pallas_call(
            functools.partial(_decode_scan_kernel, t_new=t_new, hps=hps),
            grid=grid,
            in_specs=in_specs + [tok] * 6,
            out_specs=out_specs + [tok, st],
            out_shape=out_shape + [jax.ShapeDtypeStruct((batch, seq, RW_W), F32),
                                   jax.ShapeDtypeStruct((batch * n_pairs, LANES, LANES), F32)],
            scratch_shapes=[pltpu.VMEM((batch, n_pairs, LANES, LANES), F32)],
            compiler_params=_cparams(("arbitrary", "arbitrary")),
            name="decode_attn_rwkv_scan",
        )(slopes, q, kn, vn, *native, *[x.reshape(batch, seq, RW_W) for x in scan_prep])
        scan_out = (outs[-2].reshape(batch * seq, RW_W), _unpair_state(outs[-1], batch))
        outs = outs[:-2]
    o_a = jnp.transpose(outs[0], (0, 2, 1, 3)).reshape(dec_batch * t_new, ATT_W)
    new = [jnp.transpose(c, (0, 3, 1, 2)) for c in outs[1:]]
    return (o_a, new[0::2], new[1::2]) + scan_out


def _prep_kernel(cur_ref, wa_ref, pcur_ref, pwa_ref, fcur_ref, fwa_ref, mu_ref, muwa_ref, lora_ref,
                 w0_ref, a0_ref, kk_ref, ka_ref, rk_ref, bd_ref,
                 r_o, lw_o, k_o, v_o, kkn_o, b_o, bonus_o, *, seq_len, tm):
    i = pl.program_id(0)
    row = lax.broadcasted_iota(jnp.int32, (tm, 1), 0)

    def shifted(cur, p8, first):
        rolled = pltpu.roll(cur, 1, 0)
        if seq_len >= tm:
            at_start = (i * tm) % seq_len == 0
            edge = jnp.where(at_start, first, p8[7:8, :])
            return jnp.where(row == 0, edge, rolled)
        return jnp.where(row % seq_len == 0, first, rolled)

    cur = cur_ref[...]
    wa = wa_ref[...]
    xs = cur + (shifted(cur, pcur_ref[...], fcur_ref[...]) - cur) * mu_ref[...]
    xwa = wa + (shifted(wa, pwa_ref[...], fwa_ref[...]) - wa) * muwa_ref[...]
    r = xs[:, 0:RW_W]
    kr = xs[:, RW_W:2 * RW_W]
    vr = xs[:, 2 * RW_W:3 * RW_W]
    lane = lax.broadcasted_iota(jnp.int32, (1, LANES), 1)
    lin = jnp.where(lane < LORA, jnp.tanh(xwa), xwa).astype(BF16)
    lo = jnp.dot(lin, lora_ref[...], preferred_element_type=F32)
    z = -(w0_ref[...] + lo[:, 0:RW_W])
    softplus = jnp.maximum(z, 0.0) + jnp.log1p(jnp.exp(-jnp.abs(z)))
    lw = -jnp.exp(-softplus - 0.5)
    a = jax.nn.sigmoid(a0_ref[...] + lo[:, RW_W:2 * RW_W])
    kk = kr * kk_ref[...]
    k_mod = kr * (1.0 + (a - 1.0) * ka_ref[...])
    rkk = r * k_mod * rk_ref[...]
    bd = bd_ref[...]
    r_o[...] = r
    lw_o[...] = lw
    k_o[...] = k_mod
    v_o[...] = vr
    for c in range(RW_W // LANES):
        cols = slice(c * LANES, (c + 1) * LANES)
        kc = kk[:, cols]
        nrm = jnp.maximum(jnp.sqrt(_segsum(kc * kc, bd)), L2_EPS)
        kkn = kc / nrm
        kkn_o[:, cols] = kkn
        b_o[:, cols] = kkn * a[:, cols]
        bonus_o[:, cols] = _segsum(rkk[:, cols], bd) * vr[:, cols]


def _rwkv_prep(proj, first, mu, lora_w, w0, a0, k_k, k_a, r_k, bd, seq_len, tm):
    n = proj.shape[0]
    main_w = 3 * RW_W
    cb, wb = C_SH // main_w, (C_SH + main_w) // LANES
    per_seq_first = seq_len >= tm
    if per_seq_first:
        f_main = first[:, None, 0:main_w]
        f_wa = first[:, None, main_w:]
        fm_spec = pl.BlockSpec((None, 1, main_w), lambda i: ((i * tm) // seq_len, 0, 0))
        fw_spec = pl.BlockSpec((None, 1, LANES), lambda i: ((i * tm) // seq_len, 0, 0))
    else:
        rep = jnp.repeat(first, seq_len, axis=0)
        f_main, f_wa = rep[:, 0:main_w], rep[:, main_w:]
        fm_spec = pl.BlockSpec((tm, main_w), lambda i: (i, 0))
        fw_spec = pl.BlockSpec((tm, LANES), lambda i: (i, 0))
    vec = lambda w: pl.BlockSpec((1, w), lambda i: (0, 0))
    out_spec = pl.BlockSpec((tm, RW_W), lambda i: (i, 0))
    out_sds = jax.ShapeDtypeStruct((n, RW_W), F32)
    return pl.pallas_call(
        functools.partial(_prep_kernel, seq_len=seq_len, tm=tm),
        grid=(n // tm,),
        in_specs=[
            pl.BlockSpec((tm, main_w), lambda i: (i, cb)),
            pl.BlockSpec((tm, LANES), lambda i: (i, wb)),
            pl.BlockSpec((8, main_w), lambda i: (jnp.maximum(i * (tm // 8) - 1, 0), cb)),
            pl.BlockSpec((8, LANES), lambda i: (jnp.maximum(i * (tm // 8) - 1, 0), wb)),
            fm_spec, fw_spec,
            vec(main_w), vec(LANES),
            pl.BlockSpec((LANES, 2 * RW_W), lambda i: (0, 0)),
            vec(RW_W), vec(RW_W), vec(RW_W), vec(RW_W), vec(RW_W),
            pl.BlockSpec((LANES, LANES), lambda i: (0, 0)),
        ],
        out_specs=[out_spec] * 7,
        out_shape=[out_sds] * 7,
        compiler_params=_cparams(("arbitrary",)),
        name="rwkv_prep",
    )(proj, proj, proj, proj, f_main, f_wa, mu[:, 0:main_w], mu[:, main_w:], lora_w,
      w0, a0, k_k, k_a, r_k, bd)


def _mm(a, b):
    return jnp.dot(a.astype(BF16), b.astype(BF16), preferred_element_type=F32)


def _mm_nt(a, b):
    return lax.dot_general(a.astype(BF16), b.astype(BF16), (((1,), (1,)), ((), ())),
                           preferred_element_type=F32)


def _mm_tn(a, b):
    return lax.dot_general(a.astype(BF16), b.astype(BF16), (((0,), (0,)), ((), ())),
                           preferred_element_type=F32)


def _scan_chunk(seqs, zs):
    n_pairs = RW_W // LANES
    C = CHUNK
    levels = (C - 1).bit_length()

    ti = lax.broadcasted_iota(jnp.int32, (C, C), 0)
    tj = lax.broadcasted_iota(jnp.int32, (C, C), 1)
    tri = (ti >= tj).astype(F32)
    lane = lax.broadcasted_iota(jnp.int32, (1, LANES), 1)
    in_a = lane < HEAD_DIM
    si = lax.broadcasted_iota(jnp.int32, (2 * C, 2 * C), 0)
    sj = lax.broadcasted_iota(jnp.int32, (2 * C, 2 * C), 1)
    same = (si // C) == (sj // C)
    strict = jnp.logical_and(same, (si % C) > (sj % C))
    incl = jnp.logical_and(same, (si % C) >= (sj % C))
    eye = (si == sj).astype(F32)

    def stack(x):
        return jnp.concatenate([jnp.where(in_a, x, 0.0), jnp.where(in_a, 0.0, x)], axis=0)

    chains = []
    for bi, (r, lw, k, v, kk, b) in enumerate(seqs):
        cum = jnp.dot(tri, lw, precision=lax.Precision.HIGHEST, preferred_element_type=F32)
        cum_end = cum[C - 1:C, :]
        w_in = jnp.exp(-cum)
        w_rest = jnp.exp(cum_end - cum)
        a_t = jnp.exp(cum - lw) * (-kk)
        b_t = b * w_in
        k_t = k * w_in
        r_t = r * jnp.exp(cum)
        b_h = b * w_rest
        k_h = k * w_rest
        w_end = jnp.exp(cum_end)
        for p in range(n_pairs):
            cols = slice(p * LANES, (p + 1) * LANES)
            ch = {name: stack(x[:, cols]) for name, x in
                  (("a", a_t), ("b", b_t), ("k", k_t), ("r", r_t), ("v", v), ("bh", b_h), ("kh", k_h))}
            ch["w_end"] = w_end[:, cols]
            ch["z"] = zs[bi][p]
            chains.append(ch)

    n = len(chains)
    g = [_mm_nt(jnp.concatenate([ch["a"], ch["r"]], axis=0), jnp.concatenate([ch["b"], ch["k"]], axis=0))
         for ch in chains]
    ab = [jnp.where(strict, x[0:2 * C, 0:2 * C], 0.0) for x in g]
    ak = [jnp.where(strict, x[0:2 * C, 2 * C:4 * C], 0.0) for x in g]
    rbk = [jnp.concatenate([jnp.where(incl, x[2 * C:4 * C, 0:2 * C], 0.0),
                            jnp.where(incl, x[2 * C:4 * C, 2 * C:4 * C], 0.0)], axis=1) for x in g]
    akv = [_mm(ak[i], chains[i]["v"]) for i in range(n)]
    tinv = [eye + x for x in ab]
    lp = ab
    for _ in range(levels - 1):
        lp = [_mm(x, x) for x in lp]
        tinv = [tinv[i] + _mm(tinv[i], lp[i]) for i in range(n)]
    au = [_mm(tinv[i], jnp.concatenate([chains[i]["a"], akv[i]], axis=1)) for i in range(n)]
    zeros = jnp.zeros((2 * C, LANES), F32)
    rhs = [jnp.concatenate([au[i], jnp.concatenate([zeros, chains[i]["v"]], axis=1)], axis=0)
           for i in range(n)]
    mn = [_mm_tn(jnp.concatenate([chains[i]["bh"], chains[i]["kh"]], axis=0), rhs[i])
          for i in range(n)]
    ry = [_mm(rbk[i], rhs[i]) for i in range(n)]
    yz = [_mm(jnp.concatenate([chains[i]["r"] + ry[i][:, 0:LANES],
                               mn[i][:, 0:LANES] + eye * chains[i]["w_end"]], axis=0), chains[i]["z"])
          for i in range(n)]
    z_news = [yz[i][2 * C:2 * C + LANES] + mn[i][:, LANES:2 * LANES] for i in range(n)]
    y_st = [yz[i][0:2 * C] + ry[i][:, LANES:2 * LANES] for i in range(n)]
    y_pair = [y[0:C] + y[C:2 * C] for y in y_st]
    ys = [jnp.concatenate(y_pair[bi * n_pairs:(bi + 1) * n_pairs], axis=1) for bi in range(len(seqs))]
    return ys, [z_news[bi * n_pairs:(bi + 1) * n_pairs] for bi in range(len(seqs))]


def _scan_kernel(r_ref, lw_ref, k_ref, v_ref, kk_ref, b_ref, y_ref, zt_ref, z_scr, *, nb):
    n_pairs = RW_W // LANES

    @pl.when(pl.program_id(0) == 0)
    def _():
        z_scr[...] = jnp.zeros_like(z_scr)

    seqs = [tuple(ref[bi] for ref in (r_ref, lw_ref, k_ref, v_ref, kk_ref, b_ref)) for bi in range(nb)]
    zs = [[z_scr[bi * n_pairs + p] for p in range(n_pairs)] for bi in range(nb)]
    ys, z_news = _scan_chunk(seqs, zs)
    flat = [z for zz in z_news for z in zz]
    y_ref[...] = jnp.stack(ys, axis=0)
    z_scr[...] = jnp.stack(flat, axis=0)
    zt_ref[...] = jnp.stack([z.T for z in flat], axis=0)


def _rwkv_scan(prep, batch, seq):
    assert seq % CHUNK == 0
    n_pairs = RW_W // LANES
    tok = pl.BlockSpec((batch, CHUNK, RW_W), lambda c: (0, c, 0))
    st = pl.BlockSpec((batch * n_pairs, LANES, LANES), lambda c: (0, 0, 0))
    y, zt = pl.pallas_call(
        functools.partial(_scan_kernel, nb=batch),
        grid=(seq // CHUNK,),
        in_specs=[tok] * 6,
        out_specs=[tok, st],
        out_shape=[jax.ShapeDtypeStruct((batch, seq, RW_W), F32),
                   jax.ShapeDtypeStruct((batch * n_pairs, LANES, LANES), F32)],
        scratch_shapes=[pltpu.VMEM((batch * n_pairs, LANES, LANES), F32)],
        compiler_params=_cparams(("arbitrary",)),
        name="rwkv_scan",
    )(*[x.reshape(batch, seq, RW_W) for x in prep])
    return y.reshape(batch * seq, RW_W), _unpair_state(zt, batch)


def _seq_scan_kernel(r_ref, lw_ref, k_ref, v_ref, kk_ref, b_ref, s_ref, y_ref, so_ref, w_scr, *, t_new):
    w_scr[...] = jnp.exp(lw_ref[...])

    def body(vi, carry):
        sv = s_ref[vi]
        for t in range(t_new):
            sa = jnp.sum(sv * kk_ref[t], axis=0, keepdims=True)
            sv = sv * w_scr[t] - sa * b_ref[t] + v_ref[t, pl.ds(vi, 1), :] * k_ref[t]
            y_ref[t, pl.ds(vi, 1), :] = jnp.sum(sv * r_ref[t], axis=0, keepdims=True)
        so_ref[vi] = sv
        return carry

    lax.fori_loop(0, HEAD_DIM, body, 0)


def _seq_scan(prep, state, dec_batch, t_new):
    def lanes(x):
        return jnp.transpose(x.reshape(dec_batch, t_new, RWKV_HEADS, HEAD_DIM), (1, 2, 3, 0))

    vec = pl.BlockSpec((t_new, None, HEAD_DIM, dec_batch), lambda h: (0, h, 0, 0))
    st = pl.BlockSpec((None, HEAD_DIM, HEAD_DIM, dec_batch), lambda h: (h, 0, 0, 0))
    s_native = jnp.transpose(state, (1, 2, 3, 0))
    y, s_new = pl.pallas_call(
        functools.partial(_seq_scan_kernel, t_new=t_new),
        grid=(RWKV_HEADS,),
        in_specs=[vec] * 6 + [st],
        out_specs=[vec, st],
        out_shape=[jax.ShapeDtypeStruct((t_new, RWKV_HEADS, HEAD_DIM, dec_batch), F32),
                   jax.ShapeDtypeStruct(s_native.shape, F32)],
        scratch_shapes=[pltpu.VMEM((t_new, HEAD_DIM, dec_batch), F32)],
        compiler_params=_cparams(("arbitrary",)),
        name="rwkv_seq_scan",
    )(*[lanes(x) for x in prep], s_native)
    y = jnp.transpose(y, (3, 0, 1, 2)).reshape(dec_batch * t_new, RW_W)
    return y, jnp.transpose(s_new, (3, 0, 1, 2))


def _final_kernel(*refs, n_parts):
    att = refs[0:3 * n_parts] if n_parts else refs[0:1]
    n_att = len(att)
    (y_ref, bonus_ref, gates_ref, x_ref, lnw_ref, lnb_ref, pa_ref, pb_ref, wo_ref, bd_ref,
     o_ref) = refs[n_att:]
    if n_parts:
        parts = [(att[3 * g][...], att[3 * g + 1][...], att[3 * g + 2][...]) for g in range(n_parts)]
        m_all = functools.reduce(jnp.maximum, [m for _, m, _ in parts])
        num = sum(jnp.exp(m - m_all) * a for a, m, _ in parts)
        den = sum(jnp.exp(m - m_all) * l for _, m, l in parts)
        o_a = num / den
    else:
        o_a = att[0][...]
    gates = gates_ref[...]
    z_a = gates[:, 0:ATT_W]
    z_b = gates[:, ATT_W:ATT_W + RW_W]
    g_a = gates[:, ATT_W + RW_W:ATT_W + RW_W + 1024]
    g_b = gates[:, ATT_W + RW_W + 1024:]
    y = y_ref[...]
    bd = bd_ref[...]
    cols_out = []
    for c in range(RW_W // LANES):
        cols = slice(c * LANES, (c + 1) * LANES)
        yc = y[:, cols]
        mu = _segsum(yc, bd) * (1.0 / HEAD_DIM)
        d = yc - mu
        var = _segsum(d * d, bd) * (1.0 / HEAD_DIM)
        cols_out.append(d * lax.rsqrt(var + GN_EPS))
    yn = jnp.concatenate(cols_out, axis=1)
    o_b = yn * lnw_ref[...] + lnb_ref[...] + bonus_ref[...]
    h_a = jnp.dot((o_a * jax.nn.silu(z_a)).astype(BF16), pa_ref[...], preferred_element_type=F32)
    h_b = jnp.dot((o_b * jax.nn.silu(z_b)).astype(BF16), pb_ref[...], preferred_element_type=F32)
    mixed = jax.nn.sigmoid(g_a) * h_a + jax.nn.sigmoid(g_b) * h_b
    o_ref[...] = x_ref[...] + jnp.dot(mixed.astype(BF16), wo_ref[...], preferred_element_type=F32)


def _final(att, y_raw, bonus, proj, x, ln_w, ln_b, p_a, p_b, w_out, bd, tm):
    n, d = x.shape
    tm = min(tm, n)
    n_parts = len(att) // 3 if len(att) > 1 else 0
    gates_w = C_Q - C_GATES
    row = lambda w: pl.BlockSpec((tm, w), lambda i: (i, 0))
    full = lambda a: pl.BlockSpec(a.shape, lambda i: (0, 0))
    return pl.pallas_call(
        functools.partial(_final_kernel, n_parts=n_parts),
        grid=(n // tm,),
        in_specs=([row(ATT_W)] * len(att)
                  + [row(RW_W), row(RW_W), row(gates_w), row(d),
                     full(ln_w), full(ln_b), full(p_a), full(p_b), full(w_out), full(bd)]),
        out_specs=row(d),
        out_shape=jax.ShapeDtypeStruct((n, d), F32),
        compiler_params=_cparams(("arbitrary",)),
        name="gated_out",
    )(*att, y_raw, bonus, proj, x, ln_w, ln_b, p_a, p_b, w_out, bd)


def _permute_w_in(w_in):
    q, k, v, sh, gates = (w_in[:, 0:QKV], w_in[:, QKV:2 * QKV], w_in[:, 2 * QKV:3 * QKV],
                          w_in[:, 3 * QKV:3 * QKV + SHIFT_COLS], w_in[:, 3 * QKV + SHIFT_COLS:])
    pad = jnp.zeros((w_in.shape[0], P_COLS - IN_COLS), w_in.dtype)
    return jnp.concatenate([gates, q, k, v, sh, pad], axis=1).astype(BF16)


def _layer(h_p, h_s, batch, seq, dec_batch, dec_seq, shift_s, state_s, caches, lw):
    (norm_g, w_perm, shift_mu, qg, kg, lora_w, w0, a0, k_k, k_a, r_k, ln_w, ln_b, p_a, p_b, w_out,
     bd) = lw
    n_s = dec_batch * dec_seq
    prep_args = (shift_mu, lora_w, w0, a0, k_k, k_a, r_k, bd)
    out_args = (ln_w, ln_b, p_a, p_b, w_out, bd)

    proj_p = _inproj(h_p, norm_g, w_perm, qg, kg, bd, tm=2048)
    prep_p = _rwkv_prep(proj_p, jnp.zeros((batch, SHIFT_COLS), F32), *prep_args, seq_len=seq, tm=512)
    att_p = []
    for g in range(len(ATT_GROUPS)):
        att_p += _band_attention(proj_p, batch, seq, g, tile=2048)
    pv = proj_p.reshape(batch, seq, P_COLS)
    st_p = []
    for g, (window, _) in enumerate(ATT_GROUPS):
        keep = min(window, seq)
        for c0 in (C_K, C_V):
            st_p.append(pv[:, seq - keep:, c0 + g * ATT_W:c0 + (g + 1) * ATT_W]
                        .reshape(batch, keep, HPG, HEAD_DIM))

    proj_s = _inproj(h_s, norm_g, w_perm, qg, kg, bd, tm=min(1024, n_s))
    prep_s = _rwkv_prep(proj_s, shift_s, *prep_args, seq_len=dec_seq, tm=min(512, n_s))
    if seq % CHUNK == 0 and (2, dec_batch) == (batch, seq // CHUNK):
        o_a, new_k, new_v, y_p, state_p = _decode_attention(
            proj_s, caches, dec_batch, dec_seq, scan_prep=prep_p[:6], batch=batch, seq=seq)
    else:
        o_a, new_k, new_v = _decode_attention(proj_s, caches, dec_batch, dec_seq)
        y_p, state_p = _rwkv_scan(prep_p[:6], batch, seq)
    y_s, state_s_new = _seq_scan(prep_s[:6], state_s, dec_batch, dec_seq)

    out_p = _final(att_p, y_p, prep_p[6], proj_p, h_p, *out_args, tm=256)
    out_s = _final([o_a], y_s, prep_s[6], proj_s, h_s, *out_args, tm=256)
    st_p += [state_p, pv[:, seq - 1, C_SH:C_SH + SHIFT_COLS]]
    st_s = [c for kv in zip(new_k, new_v) for c in kv]
    st_s += [state_s_new, proj_s.reshape(dec_batch, dec_seq, P_COLS)[:, dec_seq - 1, C_SH:C_SH + SHIFT_COLS]]
    return out_p, out_s, st_p, st_s


def kernel(x_prompt, x_sample, cache_k_g0, cache_v_g0, cache_k_g1, cache_v_g1, cache_k_g2, cache_v_g2, state_rwkv, state_shift, norm_g, w_in, shift_mu, q_norm_g, k_norm_g, w0, w2, a0, a2, k_k, k_a, r_k, ln_x_w, ln_x_b, p_a, p_b, w_out):
    depth = norm_g.shape[0]
    batch, seq, d_model = x_prompt.shape
    dec_batch, dec_seq, _ = x_sample.shape
    bd = _bd_ones()
    h_p = x_prompt.reshape(batch * seq, d_model)
    h_s = x_sample.reshape(dec_batch * dec_seq, d_model)
    prompt_states, sample_states = [], []
    for layer in range(depth):
        zero = jnp.zeros((LORA, RW_W), F32)
        lora_w = jnp.concatenate([jnp.concatenate([w2[layer], zero], axis=1),
                                  jnp.concatenate([zero, a2[layer]], axis=1)], axis=0).astype(BF16)
        row = lambda a: a.reshape(1, -1)
        lw = (row(norm_g[layer]), _permute_w_in(w_in[layer]), row(shift_mu[layer]),
              jnp.tile(row(q_norm_g[layer]), (1, LANES // HEAD_DIM)),
              jnp.tile(row(k_norm_g[layer]), (1, LANES // HEAD_DIM)),
              lora_w, row(w0[layer]), row(a0[layer]), row(k_k[layer]), row(k_a[layer]),
              row(r_k[layer]), row(ln_x_w[layer]), row(ln_x_b[layer]),
              p_a[layer].astype(BF16), p_b[layer].astype(BF16), w_out[layer].astype(BF16), bd)
        caches = [(cache_k_g0[layer], cache_v_g0[layer]), (cache_k_g1[layer], cache_v_g1[layer]),
                  (cache_k_g2[layer], cache_v_g2[layer])]
        h_p, h_s, st_p, st_s = _layer(h_p, h_s, batch, seq, dec_batch, dec_seq, state_shift[layer],
                                      state_rwkv[layer], caches, lw)
        prompt_states.append(st_p)
        sample_states.append(st_s)
    ps = [jnp.stack(t) for t in zip(*prompt_states)]
    ss = [jnp.stack(t) for t in zip(*sample_states)]
    return (h_p.reshape(batch, seq, d_model), h_s.reshape(dec_batch, dec_seq, d_model),
            *ps, *ss)
```

```python
import functools

import jax
import jax.numpy as jnp
from jax import lax
from jax.experimental import pallas as pl
from jax.experimental.pallas import tpu as pltpu

F32 = jnp.float32
BF16 = jnp.bfloat16

HEAD_DIM = 64
HPG = 8
ATT_GROUPS = ((128, 1), (512, 4), (2048, 16))
N_ATT_HEADS = 24
RWKV_HEADS = 8
LORA = 64
QBLOCK = 128
RMS_EPS = 1e-6
GN_EPS = 64e-5
L2_EPS = 1e-12
CHUNK = 64

LANES = 128
NEW_PAD = 16
VMEM_LIMIT = 52 * 1024 * 1024

QKV = N_ATT_HEADS * HEAD_DIM
ATT_W = HPG * HEAD_DIM
RW_W = RWKV_HEADS * HEAD_DIM
C_GATES = 0
C_Q = 3072
C_K = C_Q + QKV
C_V = C_K + QKV
C_SH = C_V + QKV
SHIFT_COLS = 3 * RW_W + 2 * LORA
IN_COLS = C_SH + SHIFT_COLS
TN = 512
P_COLS = pl.cdiv(IN_COLS, TN) * TN


def _cparams(sem):
    return pltpu.CompilerParams(dimension_semantics=sem, vmem_limit_bytes=VMEM_LIMIT)


def _segsum(x, bd):
    hi = x.astype(BF16)
    lo = (x - hi.astype(F32)).astype(BF16)
    return (jnp.dot(hi, bd, preferred_element_type=F32)
            + jnp.dot(lo, bd, preferred_element_type=F32))


def _bd_ones():
    i = jnp.arange(LANES)
    return (i[:, None] // HEAD_DIM == i[None, :] // HEAD_DIM).astype(BF16)


def _alibi_slopes(g):
    return jnp.exp2(-8.0 * (jnp.arange(HPG, dtype=F32) + (g * HPG + 1)) / N_ATT_HEADS)


def _inproj_kernel(x_ref, g_ref, w_ref, qg_ref, kg_ref, bd_ref, o_ref, xn_ref):
    j = pl.program_id(1)

    @pl.when(j == 0)
    def _():
        x = x_ref[...]
        ms = jnp.mean(x * x, axis=-1, keepdims=True)
        xn_ref[...] = (x * lax.rsqrt(ms + RMS_EPS) * g_ref[...]).astype(BF16)

    t = jnp.dot(xn_ref[...], w_ref[...], preferred_element_type=F32)
    q_lo, k_lo, k_hi = C_Q // TN, C_K // TN, C_V // TN

    @pl.when(jnp.logical_or(j < q_lo, j >= k_hi))
    def _():
        o_ref[...] = t

    @pl.when(jnp.logical_and(j >= q_lo, j < k_hi))
    def _():
        gain = jnp.where(j < k_lo, qg_ref[...], kg_ref[...])
        bd = bd_ref[...]
        for c in range(TN // LANES):
            tc = t[:, c * LANES:(c + 1) * LANES]
            ms = _segsum(tc * tc, bd) * (1.0 / HEAD_DIM)
            o_ref[:, c * LANES:(c + 1) * LANES] = tc * lax.rsqrt(ms + RMS_EPS) * gain


def _inproj(x, norm_g, w_perm, qg, kg, bd, tm):
    n, d = x.shape
    return pl.pallas_call(
        _inproj_kernel,
        grid=(n // tm, P_COLS // TN),
        in_specs=[
            pl.BlockSpec((tm, d), lambda i, j: (i, 0)),
            pl.BlockSpec((1, d), lambda i, j: (0, 0)),
            pl.BlockSpec((d, TN), lambda i, j: (0, j)),
            pl.BlockSpec((1, LANES), lambda i, j: (0, 0)),
            pl.BlockSpec((1, LANES), lambda i, j: (0, 0)),
            pl.BlockSpec((LANES, LANES), lambda i, j: (0, 0)),
        ],
        out_specs=pl.BlockSpec((tm, TN), lambda i, j: (i, j)),
        out_shape=jax.ShapeDtypeStruct((n, P_COLS), F32),
        scratch_shapes=[pltpu.VMEM((tm, d), BF16)],
        compiler_params=_cparams(("arbitrary", "arbitrary")),
        name="inproj",
    )(x, norm_g, w_perm, qg, kg, bd)


def _band_kernel(sl_ref, *refs, tile):
    n_g = len(ATT_GROUPS)
    ins, o_ref, scr = refs[0:5 * n_g], refs[5 * n_g], refs[5 * n_g + 1:]
    n = pl.program_id(1)
    p = pl.program_id(2)
    lane = lax.broadcasted_iota(jnp.int32, (1, LANES), 1)
    mask_a = lane < HEAD_DIM
    qi = lax.broadcasted_iota(jnp.int32, (QBLOCK, 2 * QBLOCK), 0)
    ki = lax.broadcasted_iota(jnp.int32, (QBLOCK, 2 * QBLOCK), 1)
    rel = qi + QBLOCK - ki
    band = jnp.logical_and(rel >= 0, rel <= QBLOCK)
    band_first = jnp.logical_and(band, jnp.logical_or(ki >= QBLOCK, n > 0))

    for g, (_, dil) in enumerate(ATT_GROUPS):
        q_ref, kp_ref, kc_ref, vp_ref, vc_ref = ins[5 * g:5 * g + 5]
        acc_scr, m_scr, l_scr = scr[3 * g:3 * g + 3]
        dist = (rel * dil).astype(F32)

        def rows(r, s, dil=dil):
            if dil == 1:
                return pl.ds(s * QBLOCK, QBLOCK)
            return pl.ds(r + s * QBLOCK * dil, QBLOCK, stride=dil)

        for r in range(dil):
            for s in range(tile // (dil * QBLOCK)):
                valid = band_first if s == 0 else band
                q = q_ref[rows(r, s), :]
                if s == 0:
                    k_prev, v_prev = kp_ref[rows(r, 0), :], vp_ref[rows(r, 0), :]
                else:
                    k_prev, v_prev = kc_ref[rows(r, s - 1), :], vc_ref[rows(r, s - 1), :]
                k = jnp.concatenate([k_prev, kc_ref[rows(r, s), :]], axis=0).astype(BF16)
                v = jnp.concatenate([v_prev, vc_ref[rows(r, s), :]], axis=0).astype(BF16)
                outs = []
                for h in range(2):
                    hm = mask_a if h == 0 else jnp.logical_not(mask_a)
                    qh = jnp.where(hm, q, 0.0).astype(BF16)
                    sc = lax.dot_general(qh, k, (((1,), (1,)), ((), ())), preferred_element_type=F32)
                    sc = sc * (1.0 / 8.0) - sl_ref[g * HPG + 2 * p + h] * dist
                    sc = jnp.where(valid, sc, -jnp.inf)
                    m = jnp.max(sc, axis=-1, keepdims=True)
                    pr = jnp.exp(sc - m)
                    l = jnp.sum(pr, axis=-1, keepdims=True)
                    acc = jnp.dot(pr.astype(BF16), v, preferred_element_type=F32)
                    outs.append((acc, m, l))
                (acc0, m0, l0), (acc1, m1, l1) = outs
                acc_scr[rows(r, s), :] = jnp.where(mask_a, acc0, acc1)
                m_scr[rows(r, s), :] = jnp.where(mask_a, m0, m1)
                l_scr[rows(r, s), :] = jnp.where(mask_a, l0, l1)

    parts = [(scr[3 * g][...], scr[3 * g + 1][...], scr[3 * g + 2][...]) for g in range(n_g)]
    m_all = functools.reduce(jnp.maximum, [m for _, m, _ in parts])
    num = sum(jnp.exp(m - m_all) * a for a, m, _ in parts)
    den = sum(jnp.exp(m - m_all) * l for _, m, l in parts)
    o_ref[...] = num / den


def _band_attention(proj, batch, seq, tile):
    n_g = len(ATT_GROUPS)
    tile = min(tile, seq)
    assert seq % tile == 0
    nt = seq // tile
    in_specs = [pl.BlockSpec(memory_space=pltpu.SMEM)]
    for g, (window, dil) in enumerate(ATT_GROUPS):
        assert window // dil == QBLOCK
        halo = QBLOCK * dil
        assert tile % halo == 0
        nh = seq // halo
        cq, ck, cv = ((c + g * ATT_W) // LANES for c in (C_Q, C_K, C_V))

        def cur(col):
            return pl.BlockSpec((tile, LANES), lambda b, n, p: (b * nt + n, col + p))

        def prev(col, halo=halo, nh=nh):
            return pl.BlockSpec(
                (halo, LANES),
                lambda b, n, p: (b * nh + jnp.maximum(n * (tile // halo) - 1, 0), col + p))

        in_specs += [cur(cq), prev(ck), cur(ck), prev(cv), cur(cv)]
    slopes = jnp.concatenate([_alibi_slopes(g) for g in range(n_g)])
    return pl.pallas_call(
        functools.partial(_band_kernel, tile=tile),
        grid=(batch, nt, ATT_W // LANES),
        in_specs=in_specs,
        out_specs=pl.BlockSpec((tile, LANES), lambda b, n, p: (b * nt + n, p)),
        out_shape=jax.ShapeDtypeStruct((batch * seq, ATT_W), F32),
        scratch_shapes=[pltpu.VMEM((tile, LANES), F32)] * (3 * n_g),
        compiler_params=_cparams(("arbitrary", "arbitrary", "arbitrary")),
        name="band_attn",
    )(slopes, *([proj] * (5 * n_g)))


def _decode_body(sl_ref, q_ref, kn_ref, vn_ref, kb, vb, o_ref, ko, vo, hh, t_new, hps):
    n_g = len(ATT_GROUPS)
    ti_n = lax.broadcasted_iota(jnp.int32, (t_new, NEW_PAD), 0)
    tj_n = lax.broadcasted_iota(jnp.int32, (t_new, NEW_PAD), 1)
    pi = lax.broadcasted_iota(jnp.int32, (NEW_PAD, LANES), 0)
    pj = lax.broadcasted_iota(jnp.int32, (NEW_PAD, LANES), 1)
    place = jnp.logical_and(pj == pi + (LANES - t_new), pi < t_new).astype(BF16)
    tail = lax.broadcasted_iota(jnp.int32, (HEAD_DIM, LANES), 1) >= LANES - t_new
    nt = (((1,), (1,)), ((), ()))
    tn = (((0,), (0,)), ((), ()))
    zpad = jnp.zeros((NEW_PAD - t_new, HEAD_DIM), F32)

    combos = [(j, g) for j in range(hps) for g in range(n_g)]
    d_new = ti_n - tj_n
    kn = {c: jnp.concatenate([kn_ref[c[1], c[0]], zpad], axis=0) for c in combos}
    vn = {c: jnp.concatenate([vn_ref[c[1], c[0]], zpad], axis=0) for c in combos}
    q = {c: q_ref[c[1], c[0]].astype(BF16) for c in combos}
    s_buf = {(j, g): jnp.dot(q[j, g], kb[g][j].astype(BF16), preferred_element_type=F32)
             for j, g in combos}
    s_new = {c: lax.dot_general(q[c], kn[c].astype(BF16), nt, preferred_element_type=F32)
             for c in combos}
    p_buf, p_new, stats = {}, {}, {}
    for j, g in combos:
        wb, dil = ATT_GROUPS[g]
        slope = sl_ref[g * HPG + hh * hps + j]
        wi = lax.broadcasted_iota(jnp.int32, (t_new, wb), 1)
        ti = lax.broadcasted_iota(jnp.int32, (t_new, wb), 0)
        d_buf = wb + ti - wi
        ok_buf = wi >= ti
        ok_new = d_new >= 0
        if dil > 1:
            ok_buf = jnp.logical_and(ok_buf, (d_buf & (dil - 1)) == 0)
            ok_new = jnp.logical_and(ok_new, (d_new & (dil - 1)) == 0)
        sb = jnp.where(ok_buf, s_buf[j, g] * (1.0 / 8.0) - slope * d_buf.astype(F32), -jnp.inf)
        sn = jnp.where(ok_new, s_new[j, g] * (1.0 / 8.0) - slope * d_new.astype(F32), -jnp.inf)
        m = jnp.maximum(jnp.max(sb, axis=-1, keepdims=True), jnp.max(sn, axis=-1, keepdims=True))
        pb, pn = jnp.exp(sb - m), jnp.exp(sn - m)
        p_buf[j, g], p_new[j, g] = pb.astype(BF16), pn.astype(BF16)
        stats[j, g] = (m, jnp.sum(pb, axis=-1, keepdims=True) + jnp.sum(pn, axis=-1, keepdims=True))
    acc = {(j, g): lax.dot_general(p_buf[j, g], vb[g][j].astype(BF16), nt, preferred_element_type=F32)
           + jnp.dot(p_new[j, g], vn[j, g].astype(BF16), preferred_element_type=F32)
           for j, g in combos}
    for j in range(hps):
        m_all = functools.reduce(jnp.maximum, [stats[j, g][0] for g in range(n_g)])
        wts = [jnp.exp(stats[j, g][0] - m_all) for g in range(n_g)]
        num = sum(wts[g] * acc[j, g] for g in range(n_g))
        den = sum(wts[g] * stats[j, g][1] for g in range(n_g))
        o_ref[j] = num / den

    for j, g in combos:
        wb = ATT_GROUPS[g][0]
        for src, new, dst in ((kb[g], kn[j, g], ko[g]), (vb[g], vn[j, g], vo[g])):
            rolled = pltpu.roll(src[j], wb - t_new, 1)
            hi = new.astype(BF16)
            lo = (new - hi.astype(F32)).astype(BF16)
            placed = (lax.dot_general(hi, place, tn, preferred_element_type=F32)
                      + lax.dot_general(lo, place, tn, preferred_element_type=F32))
            if wb > LANES:
                dst[j, :, 0:wb - LANES] = rolled[:, 0:wb - LANES]
            dst[j, :, wb - LANES:wb] = jnp.where(tail, placed, rolled[:, wb - LANES:wb])


def _decode_kernel(sl_ref, q_ref, kn_ref, vn_ref, *refs, t_new, hps):
    n_c = 2 * len(ATT_GROUPS)
    _decode_body(sl_ref, q_ref, kn_ref, vn_ref, refs[0:n_c:2], refs[1:n_c:2], refs[n_c],
                 refs[n_c + 1::2], refs[n_c + 2::2], pl.program_id(1), t_new, hps)


def _decode_scan_kernel(sl_ref, q_ref, kn_ref, vn_ref, *refs, t_new, hps, nb):
    n_c = 2 * len(ATT_GROUPS)
    caches_in, tok = refs[0:n_c], refs[n_c:n_c + 6]
    o_ref = refs[n_c + 6]
    caches_out = refs[n_c + 7:2 * n_c + 7]
    y_ref, zt_ref, z_scr, tinv_scr, lp_scr, akv_scr, rbk_scr = refs[2 * n_c + 7:]
    c, half = pl.program_id(0), pl.program_id(1)
    _decode_body(sl_ref, q_ref, kn_ref, vn_ref, caches_in[0::2], caches_in[1::2], o_ref,
                 caches_out[0::2], caches_out[1::2], half, t_new, hps)
    seqs = [tuple(ref[bi] for ref in tok) for bi in range(nb)]
    n = nb * (RW_W // LANES)

    @pl.when(jnp.logical_and(c == 0, half == 0))
    def _():
        z_scr[...] = jnp.zeros_like(z_scr)

    @pl.when(half == 0)
    def _():
        tinv, lp, akv, rbk = _scan_first_half(_scan_chains(seqs))
        tinv_scr[...] = jnp.stack(tinv, axis=0)
        lp_scr[...] = jnp.stack(lp, axis=0)
        akv_scr[...] = jnp.stack(akv, axis=0)
        rbk_scr[...] = jnp.stack(rbk, axis=0)

    @pl.when(half == 1)
    def _():
        carried = [[scr[i] for i in range(n)] for scr in (tinv_scr, lp_scr, akv_scr, rbk_scr)]
        ys, z_news = _scan_second_half(_scan_chains(seqs), *carried, [z_scr[i] for i in range(n)])
        _store_scan(ys, z_news, y_ref, zt_ref, z_scr, nb)


def _unpair_state(zt, batch):
    zt = zt.reshape(batch, RW_W // LANES, LANES, LANES)
    s_a = zt[:, :, 0:HEAD_DIM, 0:HEAD_DIM]
    s_b = zt[:, :, HEAD_DIM:, HEAD_DIM:]
    return jnp.stack([s_a, s_b], axis=2).reshape(batch, RWKV_HEADS, HEAD_DIM, HEAD_DIM)


def _decode_attention(proj, caches, dec_batch, t_new, scan_prep=None, batch=0, seq=0):
    n_g = len(ATT_GROUPS)
    hps = HPG // 2
    assert all(k.shape[1] == w for (k, _), (w, _) in zip(caches, ATT_GROUPS))

    def heads(col0):
        x = proj[:, col0:col0 + QKV].reshape(dec_batch, t_new, n_g, HPG, HEAD_DIM)
        return jnp.transpose(x, (0, 2, 3, 1, 4))

    q, kn, vn = heads(C_Q), heads(C_K), heads(C_V)
    native = [jnp.transpose(c, (0, 2, 3, 1)) for kv in caches for c in kv]
    slopes = jnp.concatenate([_alibi_slopes(g) for g in range(n_g)])

    new_q = pl.BlockSpec((None, n_g, hps, t_new, HEAD_DIM), lambda b, h: (b, 0, h, 0, 0))
    cache_specs = [pl.BlockSpec((None, hps, HEAD_DIM, c.shape[-1]), lambda b, h: (b, h, 0, 0))
                   for c in native]
    in_specs = [pl.BlockSpec(memory_space=pltpu.SMEM), new_q, new_q, new_q] + cache_specs
    out_specs = [pl.BlockSpec((None, hps, t_new, HEAD_DIM), lambda b, h: (b, h, 0, 0))] + cache_specs
    out_shape = ([jax.ShapeDtypeStruct((dec_batch, HPG, t_new, HEAD_DIM), F32)]
                 + [jax.ShapeDtypeStruct(c.shape, F32) for c in native])
    grid = (dec_batch, HPG // hps)
    if scan_prep is None:
        outs = pl.pallas_call(
            functools.partial(_decode_kernel, t_new=t_new, hps=hps),
            grid=grid, in_specs=in_specs, out_specs=out_specs, out_shape=out_shape,
            compiler_params=_cparams(("arbitrary", "arbitrary")),
            name="decode_attn",
        )(slopes, q, kn, vn, *native)
        scan_out = ()
    else:
        assert grid == (seq // CHUNK, 2)
        n = batch * (RW_W // LANES)
        tok = pl.BlockSpec((batch, CHUNK, RW_W), lambda c, h: (0, c, 0))
        st = pl.BlockSpec((n, LANES, LANES), lambda c, h: (0, 0, 0))
        sq = pltpu.VMEM((n, LANES, LANES), F32)
        outs = pl.pallas_call(
            functools.partial(_decode_scan_kernel, t_new=t_new, hps=hps, nb=batch),
            grid=grid,
            in_specs=in_specs + [tok] * 6,
            out_specs=out_specs + [tok, st],
            out_shape=out_shape + [jax.ShapeDtypeStruct((batch, seq, RW_W), F32),
                                   jax.ShapeDtypeStruct((n, LANES, LANES), F32)],
            scratch_shapes=[sq, sq, sq, sq, pltpu.VMEM((n, LANES, 2 * LANES), F32)],
            compiler_params=_cparams(("arbitrary", "arbitrary")),
            name="decode_attn_rwkv_scan",
        )(slopes, q, kn, vn, *native, *[x.reshape(batch, seq, RW_W) for x in scan_prep])
        scan_out = (outs[-2].reshape(batch * seq, RW_W), _unpair_state(outs[-1], batch))
        outs = outs[:-2]
    o_a = jnp.transpose(outs[0], (0, 2, 1, 3)).reshape(dec_batch * t_new, ATT_W)
    new = [jnp.transpose(c, (0, 3, 1, 2)) for c in outs[1:]]
    return (o_a, new[0::2], new[1::2]) + scan_out


def _prep_kernel(cur_ref, wa_ref, pcur_ref, pwa_ref, fcur_ref, fwa_ref, mu_ref, muwa_ref, lora_ref,
                 w0_ref, a0_ref, kk_ref, ka_ref, rk_ref, bd_ref,
                 r_o, lw_o, k_o, v_o, kkn_o, b_o, bonus_o, *, seq_len, tm):
    i = pl.program_id(0)
    row = lax.broadcasted_iota(jnp.int32, (tm, 1), 0)

    def shifted(cur, p8, first):
        rolled = pltpu.roll(cur, 1, 0)
        if seq_len >= tm:
            at_start = (i * tm) % seq_len == 0
            edge = jnp.where(at_start, first, p8[7:8, :])
            return jnp.where(row == 0, edge, rolled)
        return jnp.where(row % seq_len == 0, first, rolled)

    cur = cur_ref[...]
    wa = wa_ref[...]
    xs = cur + (shifted(cur, pcur_ref[...], fcur_ref[...]) - cur) * mu_ref[...]
    xwa = wa + (shifted(wa, pwa_ref[...], fwa_ref[...]) - wa) * muwa_ref[...]
    r = xs[:, 0:RW_W]
    kr = xs[:, RW_W:2 * RW_W]
    vr = xs[:, 2 * RW_W:3 * RW_W]
    lane = lax.broadcasted_iota(jnp.int32, (1, LANES), 1)
    lin = jnp.where(lane < LORA, jnp.tanh(xwa), xwa).astype(BF16)
    lo = jnp.dot(lin, lora_ref[...], preferred_element_type=F32)
    z = -(w0_ref[...] + lo[:, 0:RW_W])
    softplus = jnp.maximum(z, 0.0) + jnp.log1p(jnp.exp(-jnp.abs(z)))
    lw = -jnp.exp(-softplus - 0.5)
    a = jax.nn.sigmoid(a0_ref[...] + lo[:, RW_W:2 * RW_W])
    kk = kr * kk_ref[...]
    k_mod = kr * (1.0 + (a - 1.0) * ka_ref[...])
    rkk = r * k_mod * rk_ref[...]
    bd = bd_ref[...]
    r_o[...] = r
    lw_o[...] = lw
    k_o[...] = k_mod
    v_o[...] = vr
    for c in range(RW_W // LANES):
        cols = slice(c * LANES, (c + 1) * LANES)
        kc = kk[:, cols]
        nrm = jnp.maximum(jnp.sqrt(_segsum(kc * kc, bd)), L2_EPS)
        kkn = kc / nrm
        kkn_o[:, cols] = kkn
        b_o[:, cols] = kkn * a[:, cols]
        bonus_o[:, cols] = _segsum(rkk[:, cols], bd) * vr[:, cols]


def _rwkv_prep(proj, first, mu, lora_w, w0, a0, k_k, k_a, r_k, bd, seq_len, tm):
    n = proj.shape[0]
    main_w = 3 * RW_W
    cb, wb = C_SH // main_w, (C_SH + main_w) // LANES
    per_seq_first = seq_len >= tm
    if per_seq_first:
        f_main = first[:, None, 0:main_w]
        f_wa = first[:, None, main_w:]
        fm_spec = pl.BlockSpec((None, 1, main_w), lambda i: ((i * tm) // seq_len, 0, 0))
        fw_spec = pl.BlockSpec((None, 1, LANES), lambda i: ((i * tm) // seq_len, 0, 0))
    else:
        rep = jnp.repeat(first, seq_len, axis=0)
        f_main, f_wa = rep[:, 0:main_w], rep[:, main_w:]
        fm_spec = pl.BlockSpec((tm, main_w), lambda i: (i, 0))
        fw_spec = pl.BlockSpec((tm, LANES), lambda i: (i, 0))
    vec = lambda w: pl.BlockSpec((1, w), lambda i: (0, 0))
    out_spec = pl.BlockSpec((tm, RW_W), lambda i: (i, 0))
    out_sds = jax.ShapeDtypeStruct((n, RW_W), F32)
    return pl.pallas_call(
        functools.partial(_prep_kernel, seq_len=seq_len, tm=tm),
        grid=(n // tm,),
        in_specs=[
            pl.BlockSpec((tm, main_w), lambda i: (i, cb)),
            pl.BlockSpec((tm, LANES), lambda i: (i, wb)),
            pl.BlockSpec((8, main_w), lambda i: (jnp.maximum(i * (tm // 8) - 1, 0), cb)),
            pl.BlockSpec((8, LANES), lambda i: (jnp.maximum(i * (tm // 8) - 1, 0), wb)),
            fm_spec, fw_spec,
            vec(main_w), vec(LANES),
            pl.BlockSpec((LANES, 2 * RW_W), lambda i: (0, 0)),
            vec(RW_W), vec(RW_W), vec(RW_W), vec(RW_W), vec(RW_W),
            pl.BlockSpec((LANES, LANES), lambda i: (0, 0)),
        ],
        out_specs=[out_spec] * 7,
        out_shape=[out_sds] * 7,
        compiler_params=_cparams(("arbitrary",)),
        name="rwkv_prep",
    )(proj, proj, proj, proj, f_main, f_wa, mu[:, 0:main_w], mu[:, main_w:], lora_w,
      w0, a0, k_k, k_a, r_k, bd)


def _mm(a, b):
    return jnp.dot(a.astype(BF16), b.astype(BF16), preferred_element_type=F32)


def _mm_nt(a, b):
    return lax.dot_general(a.astype(BF16), b.astype(BF16), (((1,), (1,)), ((), ())),
                           preferred_element_type=F32)


def _mm_tn(a, b):
    return lax.dot_general(a.astype(BF16), b.astype(BF16), (((0,), (0,)), ((), ())),
                           preferred_element_type=F32)


SPLIT_LEVELS = 3


def _scan_chains(seqs):
    C = CHUNK
    ti = lax.broadcasted_iota(jnp.int32, (C, C), 0)
    tj = lax.broadcasted_iota(jnp.int32, (C, C), 1)
    tri = (ti >= tj).astype(F32)
    in_a = lax.broadcasted_iota(jnp.int32, (1, LANES), 1) < HEAD_DIM

    def stack(x):
        return jnp.concatenate([jnp.where(in_a, x, 0.0), jnp.where(in_a, 0.0, x)], axis=0)

    chains = []
    for r, lw, k, v, kk, b in seqs:
        cum = jnp.dot(tri, lw, precision=lax.Precision.HIGHEST, preferred_element_type=F32)
        cum_end = cum[C - 1:C, :]
        w_in = jnp.exp(-cum)
        w_rest = jnp.exp(cum_end - cum)
        vals = (("a", jnp.exp(cum - lw) * (-kk)), ("b", b * w_in), ("k", k * w_in), ("r", r * jnp.exp(cum)),
                ("v", v), ("bh", b * w_rest), ("kh", k * w_rest))
        w_end = jnp.exp(cum_end)
        for p in range(RW_W // LANES):
            cols = slice(p * LANES, (p + 1) * LANES)
            ch = {name: stack(x[:, cols]) for name, x in vals}
            ch["w_end"] = w_end[:, cols]
            chains.append(ch)
    return chains


def _scan_masks():
    C = CHUNK
    si = lax.broadcasted_iota(jnp.int32, (2 * C, 2 * C), 0)
    sj = lax.broadcasted_iota(jnp.int32, (2 * C, 2 * C), 1)
    same = (si // C) == (sj // C)
    strict = jnp.logical_and(same, (si % C) > (sj % C))
    incl = jnp.logical_and(same, (si % C) >= (sj % C))
    return strict, incl, (si == sj).astype(F32)


def _double(tinv, lp, levels):
    for _ in range(levels):
        lp = [_mm(x, x) for x in lp]
        tinv = [t + _mm(t, x) for t, x in zip(tinv, lp)]
    return tinv, lp


def _scan_first_half(chains):
    C = CHUNK
    strict, incl, eye = _scan_masks()
    g = [_mm_nt(jnp.concatenate([ch["a"], ch["r"]], axis=0), jnp.concatenate([ch["b"], ch["k"]], axis=0))
         for ch in chains]
    ab = [jnp.where(strict, x[0:2 * C, 0:2 * C], 0.0) for x in g]
    ak = [jnp.where(strict, x[0:2 * C, 2 * C:4 * C], 0.0) for x in g]
    rbk = [jnp.concatenate([jnp.where(incl, x[2 * C:4 * C, 0:2 * C], 0.0),
                            jnp.where(incl, x[2 * C:4 * C, 2 * C:4 * C], 0.0)], axis=1) for x in g]
    akv = [_mm(x, ch["v"]) for x, ch in zip(ak, chains)]
    tinv, lp = _double([eye + x for x in ab], ab, SPLIT_LEVELS)
    return tinv, lp, akv, rbk


def _scan_second_half(chains, tinv, lp, akv, rbk, zs):
    C = CHUNK
    eye = _scan_masks()[2]
    n = len(chains)
    levels = (C - 1).bit_length()
    tinv, _ = _double(tinv, lp, levels - 1 - SPLIT_LEVELS)
    au = [_mm(tinv[i], jnp.concatenate([chains[i]["a"], akv[i]], axis=1)) for i in range(n)]
    zeros = jnp.zeros((2 * C, LANES), F32)
    rhs = [jnp.concatenate([au[i], jnp.concatenate([zeros, chains[i]["v"]], axis=1)], axis=0)
           for i in range(n)]
    mn = [_mm_tn(jnp.concatenate([chains[i]["bh"], chains[i]["kh"]], axis=0), rhs[i])
          for i in range(n)]
    ry = [_mm(rbk[i], rhs[i]) for i in range(n)]
    yz = [_mm(jnp.concatenate([chains[i]["r"] + ry[i][:, 0:LANES],
                               mn[i][:, 0:LANES] + eye * chains[i]["w_end"]], axis=0), zs[i])
          for i in range(n)]
    z_news = [yz[i][2 * C:2 * C + LANES] + mn[i][:, LANES:2 * LANES] for i in range(n)]
    y_st = [yz[i][0:2 * C] + ry[i][:, LANES:2 * LANES] for i in range(n)]
    return [y[0:C] + y[C:2 * C] for y in y_st], z_news


def _store_scan(ys, z_news, y_ref, zt_ref, z_scr, nb):
    n_pairs = RW_W // LANES
    y_ref[...] = jnp.stack([jnp.concatenate(ys[bi * n_pairs:(bi + 1) * n_pairs], axis=1)
                            for bi in range(nb)], axis=0)
    z_scr[...] = jnp.stack(z_news, axis=0)
    zt_ref[...] = jnp.stack([z.T for z in z_news], axis=0)


def _scan_kernel(r_ref, lw_ref, k_ref, v_ref, kk_ref, b_ref, y_ref, zt_ref, z_scr, *, nb):
    @pl.when(pl.program_id(0) == 0)
    def _():
        z_scr[...] = jnp.zeros_like(z_scr)

    chains = _scan_chains([tuple(ref[bi] for ref in (r_ref, lw_ref, k_ref, v_ref, kk_ref, b_ref))
                           for bi in range(nb)])
    ys, z_news = _scan_second_half(chains, *_scan_first_half(chains),
                                   [z_scr[i] for i in range(len(chains))])
    _store_scan(ys, z_news, y_ref, zt_ref, z_scr, nb)


def _rwkv_scan(prep, batch, seq):
    assert seq % CHUNK == 0
    n_pairs = RW_W // LANES
    tok = pl.BlockSpec((batch, CHUNK, RW_W), lambda c: (0, c, 0))
    st = pl.BlockSpec((batch * n_pairs, LANES, LANES), lambda c: (0, 0, 0))
    y, zt = pl.pallas_call(
        functools.partial(_scan_kernel, nb=batch),
        grid=(seq // CHUNK,),
        in_specs=[tok] * 6,
        out_specs=[tok, st],
        out_shape=[jax.ShapeDtypeStruct((batch, seq, RW_W), F32),
                   jax.ShapeDtypeStruct((batch * n_pairs, LANES, LANES), F32)],
        scratch_shapes=[pltpu.VMEM((batch * n_pairs, LANES, LANES), F32)],
        compiler_params=_cparams(("arbitrary",)),
        name="rwkv_scan",
    )(*[x.reshape(batch, seq, RW_W) for x in prep])
    return y.reshape(batch * seq, RW_W), _unpair_state(zt, batch)


def _seq_scan_kernel(r_ref, lw_ref, k_ref, v_ref, kk_ref, b_ref, s_ref, y_ref, so_ref, w_scr, *, t_new):
    w_scr[...] = jnp.exp(lw_ref[...])

    def body(vi, carry):
        sv = s_ref[vi]
        for t in range(t_new):
            sa = jnp.sum(sv * kk_ref[t], axis=0, keepdims=True)
            sv = sv * w_scr[t] - sa * b_ref[t] + v_ref[t, pl.ds(vi, 1), :] * k_ref[t]
            y_ref[t, pl.ds(vi, 1), :] = jnp.sum(sv * r_ref[t], axis=0, keepdims=True)
        so_ref[vi] = sv
        return carry

    lax.fori_loop(0, HEAD_DIM, body, 0)


def _seq_scan(prep, state, dec_batch, t_new):
    def lanes(x):
        return jnp.transpose(x.reshape(dec_batch, t_new, RWKV_HEADS, HEAD_DIM), (1, 2, 3, 0))

    vec = pl.BlockSpec((t_new, None, HEAD_DIM, dec_batch), lambda h: (0, h, 0, 0))
    st = pl.BlockSpec((None, HEAD_DIM, HEAD_DIM, dec_batch), lambda h: (h, 0, 0, 0))
    s_native = jnp.transpose(state, (1, 2, 3, 0))
    y, s_new = pl.pallas_call(
        functools.partial(_seq_scan_kernel, t_new=t_new),
        grid=(RWKV_HEADS,),
        in_specs=[vec] * 6 + [st],
        out_specs=[vec, st],
        out_shape=[jax.ShapeDtypeStruct((t_new, RWKV_HEADS, HEAD_DIM, dec_batch), F32),
                   jax.ShapeDtypeStruct(s_native.shape, F32)],
        scratch_shapes=[pltpu.VMEM((t_new, HEAD_DIM, dec_batch), F32)],
        compiler_params=_cparams(("arbitrary",)),
        name="rwkv_seq_scan",
    )(*[lanes(x) for x in prep], s_native)
    y = jnp.transpose(y, (3, 0, 1, 2)).reshape(dec_batch * t_new, RW_W)
    return y, jnp.transpose(s_new, (3, 0, 1, 2))


def _final_kernel(oa_ref, y_ref, bonus_ref, gates_ref, x_ref, lnw_ref, lnb_ref, pa_ref, pb_ref, wo_ref, bd_ref,
                  o_ref):
    o_a = oa_ref[...]
    gates = gates_ref[...]
    z_a = gates[:, 0:ATT_W]
    z_b = gates[:, ATT_W:ATT_W + RW_W]
    g_a = gates[:, ATT_W + RW_W:ATT_W + RW_W + 1024]
    g_b = gates[:, ATT_W + RW_W + 1024:]
    y = y_ref[...]
    bd = bd_ref[...]
    cols_out = []
    for c in range(RW_W // LANES):
        cols = slice(c * LANES, (c + 1) * LANES)
        yc = y[:, cols]
        mu = _segsum(yc, bd) * (1.0 / HEAD_DIM)
        d = yc - mu
        var = _segsum(d * d, bd) * (1.0 / HEAD_DIM)
        cols_out.append(d * lax.rsqrt(var + GN_EPS))
    yn = jnp.concatenate(cols_out, axis=1)
    o_b = yn * lnw_ref[...] + lnb_ref[...] + bonus_ref[...]
    h_a = jnp.dot((o_a * jax.nn.silu(z_a)).astype(BF16), pa_ref[...], preferred_element_type=F32)
    h_b = jnp.dot((o_b * jax.nn.silu(z_b)).astype(BF16), pb_ref[...], preferred_element_type=F32)
    mixed = jax.nn.sigmoid(g_a) * h_a + jax.nn.sigmoid(g_b) * h_b
    o_ref[...] = x_ref[...] + jnp.dot(mixed.astype(BF16), wo_ref[...], preferred_element_type=F32)


def _final(o_a, y_raw, bonus, proj, x, ln_w, ln_b, p_a, p_b, w_out, bd, tm):
    n, d = x.shape
    tm = min(tm, n)
    gates_w = C_Q - C_GATES
    row = lambda w: pl.BlockSpec((tm, w), lambda i: (i, 0))
    full = lambda a: pl.BlockSpec(a.shape, lambda i: (0, 0))
    return pl.pallas_call(
        _final_kernel,
        grid=(n // tm,),
        in_specs=[row(ATT_W), row(RW_W), row(RW_W), row(gates_w), row(d),
                  full(ln_w), full(ln_b), full(p_a), full(p_b), full(w_out), full(bd)],
        out_specs=row(d),
        out_shape=jax.ShapeDtypeStruct((n, d), F32),
        compiler_params=_cparams(("arbitrary",)),
        name="gated_out",
    )(o_a, y_raw, bonus, proj, x, ln_w, ln_b, p_a, p_b, w_out, bd)


def _permute_w_in(w_in):
    q, k, v, sh, gates = (w_in[:, 0:QKV], w_in[:, QKV:2 * QKV], w_in[:, 2 * QKV:3 * QKV],
                          w_in[:, 3 * QKV:3 * QKV + SHIFT_COLS], w_in[:, 3 * QKV + SHIFT_COLS:])
    pad = jnp.zeros((w_in.shape[0], P_COLS - IN_COLS), w_in.dtype)
    return jnp.concatenate([gates, q, k, v, sh, pad], axis=1).astype(BF16)


def _layer(h_p, h_s, batch, seq, dec_batch, dec_seq, shift_s, state_s, caches, lw):
    (norm_g, w_perm, shift_mu, qg, kg, lora_w, w0, a0, k_k, k_a, r_k, ln_w, ln_b, p_a, p_b, w_out,
     bd) = lw
    n_s = dec_batch * dec_seq
    prep_args = (shift_mu, lora_w, w0, a0, k_k, k_a, r_k, bd)
    out_args = (ln_w, ln_b, p_a, p_b, w_out, bd)

    proj_p = _inproj(h_p, norm_g, w_perm, qg, kg, bd, tm=2048)
    prep_p = _rwkv_prep(proj_p, jnp.zeros((batch, SHIFT_COLS), F32), *prep_args, seq_len=seq, tm=512)
    o_a_p = _band_attention(proj_p, batch, seq, tile=2048)
    pv = proj_p.reshape(batch, seq, P_COLS)
    st_p = []
    for g, (window, _) in enumerate(ATT_GROUPS):
        keep = min(window, seq)
        for c0 in (C_K, C_V):
            st_p.append(pv[:, seq - keep:, c0 + g * ATT_W:c0 + (g + 1) * ATT_W]
                        .reshape(batch, keep, HPG, HEAD_DIM))

    proj_s = _inproj(h_s, norm_g, w_perm, qg, kg, bd, tm=min(1024, n_s))
    prep_s = _rwkv_prep(proj_s, shift_s, *prep_args, seq_len=dec_seq, tm=min(512, n_s))
    if seq % CHUNK == 0 and dec_batch == seq // CHUNK:
        o_a, new_k, new_v, y_p, state_p = _decode_attention(
            proj_s, caches, dec_batch, dec_seq, scan_prep=prep_p[:6], batch=batch, seq=seq)
    else:
        o_a, new_k, new_v = _decode_attention(proj_s, caches, dec_batch, dec_seq)
        y_p, state_p = _rwkv_scan(prep_p[:6], batch, seq)
    y_s, state_s_new = _seq_scan(prep_s[:6], state_s, dec_batch, dec_seq)

    out_p = _final(o_a_p, y_p, prep_p[6], proj_p, h_p, *out_args, tm=512)
    out_s = _final(o_a, y_s, prep_s[6], proj_s, h_s, *out_args, tm=512)
    st_p += [state_p, pv[:, seq - 1, C_SH:C_SH + SHIFT_COLS]]
    st_s = [c for kv in zip(new_k, new_v) for c in kv]
    st_s += [state_s_new, proj_s.reshape(dec_batch, dec_seq, P_COLS)[:, dec_seq - 1, C_SH:C_SH + SHIFT_COLS]]
    return out_p, out_s, st_p, st_s


def kernel(x_prompt, x_sample, cache_k_g0, cache_v_g0, cache_k_g1, cache_v_g1, cache_k_g2, cache_v_g2, state_rwkv, state_shift, norm_g, w_in, shift_mu, q_norm_g, k_norm_g, w0, w2, a0, a2, k_k, k_a, r_k, ln_x_w, ln_x_b, p_a, p_b, w_out):
    depth = norm_g.shape[0]
    batch, seq, d_model = x_prompt.shape
    dec_batch, dec_seq, _ = x_sample.shape
    bd = _bd_ones()
    h_p = x_prompt.reshape(batch * seq, d_model)
    h_s = x_sample.reshape(dec_batch * dec_seq, d_model)
    prompt_states, sample_states = [], []
    for layer in range(depth):
        zero = jnp.zeros((LORA, RW_W), F32)
        lora_w = jnp.concatenate([jnp.concatenate([w2[layer], zero], axis=1),
                                  jnp.concatenate([zero, a2[layer]], axis=1)], axis=0).astype(BF16)
        row = lambda a: a.reshape(1, -1)
        lw = (row(norm_g[layer]), _permute_w_in(w_in[layer]), row(shift_mu[layer]),
              jnp.tile(row(q_norm_g[layer]), (1, LANES // HEAD_DIM)),
              jnp.tile(row(k_norm_g[layer]), (1, LANES // HEAD_DIM)),
              lora_w, row(w0[layer]), row(a0[layer]), row(k_k[layer]), row(k_a[layer]),
              row(r_k[layer]), row(ln_x_w[layer]), row(ln_x_b[layer]),
              p_a[layer].astype(BF16), p_b[layer].astype(BF16), w_out[layer].astype(BF16), bd)
        caches = [(cache_k_g0[layer], cache_v_g0[layer]), (cache_k_g1[layer], cache_v_g1[layer]),
                  (cache_k_g2[layer], cache_v_g2[layer])]
        h_p, h_s, st_p, st_s = _layer(h_p, h_s, batch, seq, dec_batch, dec_seq, state_shift[layer],
                                      state_rwkv[layer], caches, lw)
        prompt_states.append(st_p)
        sample_states.append(st_s)
    ps = [jnp.stack(t) for t in zip(*prompt_states)]
    ss = [jnp.stack(t) for t in zip(*sample_states)]
    return (h_p.reshape(batch, seq, d_model), h_s.reshape(dec_batch, dec_seq, d_model),
            *ps, *ss)
```

```python
import functools

import jax
import jax.numpy as jnp
from jax import lax
from jax.experimental import pallas as pl
from jax.experimental.pallas import tpu as pltpu

F32 = jnp.float32
BF16 = jnp.bfloat16

HEAD_DIM = 64
HPG = 8
ATT_GROUPS = ((128, 1), (512, 4), (2048, 16))
N_ATT_HEADS = 24
RWKV_HEADS = 8
LORA = 64
QBLOCK = 128
RMS_EPS = 1e-6
GN_EPS = 64e-5
L2_EPS = 1e-12
CHUNK = 64

LANES = 128
NEW_PAD = 16
VMEM_LIMIT = 52 * 1024 * 1024

QKV = N_ATT_HEADS * HEAD_DIM
ATT_W = HPG * HEAD_DIM
RW_W = RWKV_HEADS * HEAD_DIM
C_GATES = 0
C_Q = 3072
C_K = C_Q + QKV
C_V = C_K + QKV
C_SH = C_V + QKV
SHIFT_COLS = 3 * RW_W + 2 * LORA
IN_COLS = C_SH + SHIFT_COLS
TN = 512
P_COLS = pl.cdiv(IN_COLS, TN) * TN


def _cparams(sem):
    return pltpu.CompilerParams(dimension_semantics=sem, vmem_limit_bytes=VMEM_LIMIT)


def _segsum(x, bd):
    hi = x.astype(BF16)
    lo = (x - hi.astype(F32)).astype(BF16)
    return (jnp.dot(hi, bd, preferred_element_type=F32)
            + jnp.dot(lo, bd, preferred_element_type=F32))


def _bd_ones():
    i = jnp.arange(LANES)
    return (i[:, None] // HEAD_DIM == i[None, :] // HEAD_DIM).astype(BF16)


def _alibi_slopes(g):
    return jnp.exp2(-8.0 * (jnp.arange(HPG, dtype=F32) + (g * HPG + 1)) / N_ATT_HEADS)


def _inproj_kernel(x_ref, g_ref, w_ref, qg_ref, kg_ref, bd_ref, o_ref, xn_ref):
    j = pl.program_id(1)

    @pl.when(j == 0)
    def _():
        x = x_ref[...]
        ms = jnp.mean(x * x, axis=-1, keepdims=True)
        xn_ref[...] = (x * lax.rsqrt(ms + RMS_EPS) * g_ref[...]).astype(BF16)

    t = jnp.dot(xn_ref[...], w_ref[...], preferred_element_type=F32)
    q_lo, k_lo, k_hi = C_Q // TN, C_K // TN, C_V // TN

    @pl.when(jnp.logical_or(j < q_lo, j >= k_hi))
    def _():
        o_ref[...] = t

    @pl.when(jnp.logical_and(j >= q_lo, j < k_hi))
    def _():
        gain = jnp.where(j < k_lo, qg_ref[...], kg_ref[...])
        bd = bd_ref[...]
        for c in range(TN // LANES):
            tc = t[:, c * LANES:(c + 1) * LANES]
            ms = _segsum(tc * tc, bd) * (1.0 / HEAD_DIM)
            o_ref[:, c * LANES:(c + 1) * LANES] = tc * lax.rsqrt(ms + RMS_EPS) * gain


def _inproj(x, norm_g, w_perm, qg, kg, bd, tm):
    n, d = x.shape
    return pl.pallas_call(
        _inproj_kernel,
        grid=(n // tm, P_COLS // TN),
        in_specs=[
            pl.BlockSpec((tm, d), lambda i, j: (i, 0)),
            pl.BlockSpec((1, d), lambda i, j: (0, 0)),
            pl.BlockSpec((d, TN), lambda i, j: (0, j)),
            pl.BlockSpec((1, LANES), lambda i, j: (0, 0)),
            pl.BlockSpec((1, LANES), lambda i, j: (0, 0)),
            pl.BlockSpec((LANES, LANES), lambda i, j: (0, 0)),
        ],
        out_specs=pl.BlockSpec((tm, TN), lambda i, j: (i, j)),
        out_shape=jax.ShapeDtypeStruct((n, P_COLS), F32),
        scratch_shapes=[pltpu.VMEM((tm, d), BF16)],
        compiler_params=_cparams(("arbitrary", "arbitrary")),
        name="inproj",
    )(x, norm_g, w_perm, qg, kg, bd)


def _band_kernel(sl_ref, *refs, tile):
    n_g = len(ATT_GROUPS)
    ins, o_ref, scr = refs[0:5 * n_g], refs[5 * n_g], refs[5 * n_g + 1:]
    n = pl.program_id(1)
    p = pl.program_id(2)
    lane = lax.broadcasted_iota(jnp.int32, (1, LANES), 1)
    mask_a = lane < HEAD_DIM
    qi = lax.broadcasted_iota(jnp.int32, (QBLOCK, 2 * QBLOCK), 0)
    ki = lax.broadcasted_iota(jnp.int32, (QBLOCK, 2 * QBLOCK), 1)
    rel = qi + QBLOCK - ki
    band = jnp.logical_and(rel >= 0, rel <= QBLOCK)
    band_first = jnp.logical_and(band, jnp.logical_or(ki >= QBLOCK, n > 0))

    for g, (_, dil) in enumerate(ATT_GROUPS):
        q_ref, kp_ref, kc_ref, vp_ref, vc_ref = ins[5 * g:5 * g + 5]
        acc_scr, m_scr, l_scr = scr[3 * g:3 * g + 3]
        dist = (rel * dil).astype(F32)
        alibi = [-(sl_ref[g * HPG + 2 * p + h] * dist) for h in range(2)]
        bias = [jnp.where(band, a, -jnp.inf) for a in alibi]
        bias_first = [jnp.where(band_first, a, -jnp.inf) for a in alibi]

        def rows(r, s, dil=dil):
            if dil == 1:
                return pl.ds(s * QBLOCK, QBLOCK)
            return pl.ds(r + s * QBLOCK * dil, QBLOCK, stride=dil)

        for r in range(dil):
            for s in range(tile // (dil * QBLOCK)):
                blk_bias = bias_first if s == 0 else bias
                q = q_ref[rows(r, s), :] * (1.0 / 8.0)
                if s == 0:
                    k_prev, v_prev = kp_ref[rows(r, 0), :], vp_ref[rows(r, 0), :]
                else:
                    k_prev, v_prev = kc_ref[rows(r, s - 1), :], vc_ref[rows(r, s - 1), :]
                k = jnp.concatenate([k_prev, kc_ref[rows(r, s), :]], axis=0).astype(BF16)
                v = jnp.concatenate([v_prev, vc_ref[rows(r, s), :]], axis=0).astype(BF16)
                outs = []
                for h in range(2):
                    hm = mask_a if h == 0 else jnp.logical_not(mask_a)
                    qh = jnp.where(hm, q, 0.0).astype(BF16)
                    sc = lax.dot_general(qh, k, (((1,), (1,)), ((), ())), preferred_element_type=F32)
                    sc = sc + blk_bias[h]
                    m = jnp.max(sc, axis=-1, keepdims=True)
                    pr = jnp.exp(sc - m)
                    l = jnp.sum(pr, axis=-1, keepdims=True)
                    acc = jnp.dot(pr.astype(BF16), v, preferred_element_type=F32)
                    outs.append((acc, m, l))
                (acc0, m0, l0), (acc1, m1, l1) = outs
                acc_scr[rows(r, s), :] = jnp.where(mask_a, acc0, acc1)
                m_scr[rows(r, s), :] = jnp.where(mask_a, m0, m1)
                l_scr[rows(r, s), :] = jnp.where(mask_a, l0, l1)

    parts = [(scr[3 * g][...], scr[3 * g + 1][...], scr[3 * g + 2][...]) for g in range(n_g)]
    m_all = functools.reduce(jnp.maximum, [m for _, m, _ in parts])
    num = sum(jnp.exp(m - m_all) * a for a, m, _ in parts)
    den = sum(jnp.exp(m - m_all) * l for _, m, l in parts)
    o_ref[...] = num / den


def _band_attention(proj, batch, seq, tile):
    n_g = len(ATT_GROUPS)
    tile = min(tile, seq)
    assert seq % tile == 0
    nt = seq // tile
    in_specs = [pl.BlockSpec(memory_space=pltpu.SMEM)]
    for g, (window, dil) in enumerate(ATT_GROUPS):
        assert window // dil == QBLOCK
        halo = QBLOCK * dil
        assert tile % halo == 0
        nh = seq // halo
        cq, ck, cv = ((c + g * ATT_W) // LANES for c in (C_Q, C_K, C_V))

        def cur(col):
            return pl.BlockSpec((tile, LANES), lambda b, n, p: (b * nt + n, col + p))

        def prev(col, halo=halo, nh=nh):
            return pl.BlockSpec(
                (halo, LANES),
                lambda b, n, p: (b * nh + jnp.maximum(n * (tile // halo) - 1, 0), col + p))

        in_specs += [cur(cq), prev(ck), cur(ck), prev(cv), cur(cv)]
    slopes = jnp.concatenate([_alibi_slopes(g) for g in range(n_g)])
    return pl.pallas_call(
        functools.partial(_band_kernel, tile=tile),
        grid=(batch, nt, ATT_W // LANES),
        in_specs=in_specs,
        out_specs=pl.BlockSpec((tile, LANES), lambda b, n, p: (b * nt + n, p)),
        out_shape=jax.ShapeDtypeStruct((batch * seq, ATT_W), F32),
        scratch_shapes=[pltpu.VMEM((tile, LANES), F32)] * (3 * n_g),
        compiler_params=_cparams(("arbitrary", "arbitrary", "arbitrary")),
        name="band_attn",
    )(slopes, *([proj] * (5 * n_g)))


def _decode_body(sl_ref, q_ref, kn_ref, vn_ref, kb, vb, o_ref, ko, vo, hh, t_new, hps):
    n_g = len(ATT_GROUPS)
    ti_n = lax.broadcasted_iota(jnp.int32, (t_new, NEW_PAD), 0)
    tj_n = lax.broadcasted_iota(jnp.int32, (t_new, NEW_PAD), 1)
    pi = lax.broadcasted_iota(jnp.int32, (NEW_PAD, LANES), 0)
    pj = lax.broadcasted_iota(jnp.int32, (NEW_PAD, LANES), 1)
    place = jnp.logical_and(pj == pi + (LANES - t_new), pi < t_new).astype(BF16)
    tail = lax.broadcasted_iota(jnp.int32, (HEAD_DIM, LANES), 1) >= LANES - t_new
    nt = (((1,), (1,)), ((), ()))
    tn = (((0,), (0,)), ((), ()))
    zpad = jnp.zeros((NEW_PAD - t_new, HEAD_DIM), F32)

    combos = [(j, g) for j in range(hps) for g in range(n_g)]
    d_new = ti_n - tj_n
    kn = {c: jnp.concatenate([kn_ref[c[1], c[0]], zpad], axis=0) for c in combos}
    vn = {c: jnp.concatenate([vn_ref[c[1], c[0]], zpad], axis=0) for c in combos}
    q = {c: q_ref[c[1], c[0]].astype(BF16) for c in combos}
    s_buf = {(j, g): jnp.dot(q[j, g], kb[g][j].astype(BF16), preferred_element_type=F32)
             for j, g in combos}
    s_new = {c: lax.dot_general(q[c], kn[c].astype(BF16), nt, preferred_element_type=F32)
             for c in combos}
    p_buf, p_new, stats = {}, {}, {}
    for j, g in combos:
        wb, dil = ATT_GROUPS[g]
        slope = sl_ref[g * HPG + hh * hps + j]
        wi = lax.broadcasted_iota(jnp.int32, (t_new, wb), 1)
        ti = lax.broadcasted_iota(jnp.int32, (t_new, wb), 0)
        d_buf = wb + ti - wi
        ok_buf = wi >= ti
        ok_new = d_new >= 0
        if dil > 1:
            ok_buf = jnp.logical_and(ok_buf, (d_buf & (dil - 1)) == 0)
            ok_new = jnp.logical_and(ok_new, (d_new & (dil - 1)) == 0)
        sb = jnp.where(ok_buf, s_buf[j, g] * (1.0 / 8.0) - slope * d_buf.astype(F32), -jnp.inf)
        sn = jnp.where(ok_new, s_new[j, g] * (1.0 / 8.0) - slope * d_new.astype(F32), -jnp.inf)
        m = jnp.maximum(jnp.max(sb, axis=-1, keepdims=True), jnp.max(sn, axis=-1, keepdims=True))
        pb, pn = jnp.exp(sb - m), jnp.exp(sn - m)
        p_buf[j, g], p_new[j, g] = pb.astype(BF16), pn.astype(BF16)
        stats[j, g] = (m, jnp.sum(pb, axis=-1, keepdims=True) + jnp.sum(pn, axis=-1, keepdims=True))
    acc = {(j, g): lax.dot_general(p_buf[j, g], vb[g][j].astype(BF16), nt, preferred_element_type=F32)
           + jnp.dot(p_new[j, g], vn[j, g].astype(BF16), preferred_element_type=F32)
           for j, g in combos}
    for j in range(hps):
        m_all = functools.reduce(jnp.maximum, [stats[j, g][0] for g in range(n_g)])
        wts = [jnp.exp(stats[j, g][0] - m_all) for g in range(n_g)]
        num = sum(wts[g] * acc[j, g] for g in range(n_g))
        den = sum(wts[g] * stats[j, g][1] for g in range(n_g))
        o_ref[j] = num / den

    for j, g in combos:
        wb = ATT_GROUPS[g][0]
        for src, new, dst in ((kb[g], kn[j, g], ko[g]), (vb[g], vn[j, g], vo[g])):
            rolled = pltpu.roll(src[j], wb - t_new, 1)
            hi = new.astype(BF16)
            lo = (new - hi.astype(F32)).astype(BF16)
            placed = (lax.dot_general(hi, place, tn, preferred_element_type=F32)
                      + lax.dot_general(lo, place, tn, preferred_element_type=F32))
            if wb > LANES:
                dst[j, :, 0:wb - LANES] = rolled[:, 0:wb - LANES]
            dst[j, :, wb - LANES:wb] = jnp.where(tail, placed, rolled[:, wb - LANES:wb])


def _decode_kernel(sl_ref, q_ref, kn_ref, vn_ref, *refs, t_new, hps):
    n_c = 2 * len(ATT_GROUPS)
    _decode_body(sl_ref, q_ref, kn_ref, vn_ref, refs[0:n_c:2], refs[1:n_c:2], refs[n_c],
                 refs[n_c + 1::2], refs[n_c + 2::2], pl.program_id(1), t_new, hps)


def _decode_scan_kernel(sl_ref, q_ref, kn_ref, vn_ref, *refs, t_new, hps, nb):
    n_c = 2 * len(ATT_GROUPS)
    caches_in, tok = refs[0:n_c], refs[n_c:n_c + 6]
    o_ref = refs[n_c + 6]
    caches_out = refs[n_c + 7:2 * n_c + 7]
    y_ref, zt_ref, z_scr, tinv_scr, lp_scr, akv_scr, rbk_scr = refs[2 * n_c + 7:]
    c, half = pl.program_id(0), pl.program_id(1)
    _decode_body(sl_ref, q_ref, kn_ref, vn_ref, caches_in[0::2], caches_in[1::2], o_ref,
                 caches_out[0::2], caches_out[1::2], half, t_new, hps)
    seqs = [tuple(ref[bi] for ref in tok) for bi in range(nb)]
    n = nb * (RW_W // LANES)

    @pl.when(jnp.logical_and(c == 0, half == 0))
    def _():
        z_scr[...] = jnp.zeros_like(z_scr)

    @pl.when(half == 0)
    def _():
        tinv, lp, akv, rbk = _scan_first_half(_scan_chains(seqs))
        tinv_scr[...] = jnp.stack(tinv, axis=0)
        lp_scr[...] = jnp.stack(lp, axis=0)
        akv_scr[...] = jnp.stack(akv, axis=0)
        rbk_scr[...] = jnp.stack(rbk, axis=0)

    @pl.when(half == 1)
    def _():
        carried = [[scr[i] for i in range(n)] for scr in (tinv_scr, lp_scr, akv_scr, rbk_scr)]
        ys, z_news = _scan_second_half(_scan_chains(seqs), *carried, [z_scr[i] for i in range(n)])
        _store_scan(ys, z_news, y_ref, zt_ref, z_scr, nb)


def _unpair_state(zt, batch):
    zt = zt.reshape(batch, RW_W // LANES, LANES, LANES)
    s_a = zt[:, :, 0:HEAD_DIM, 0:HEAD_DIM]
    s_b = zt[:, :, HEAD_DIM:, HEAD_DIM:]
    return jnp.stack([s_a, s_b], axis=2).reshape(batch, RWKV_HEADS, HEAD_DIM, HEAD_DIM)


def _decode_attention(proj, caches, dec_batch, t_new, scan_prep=None, batch=0, seq=0):
    n_g = len(ATT_GROUPS)
    hps = HPG // 2
    assert all(k.shape[1] == w for (k, _), (w, _) in zip(caches, ATT_GROUPS))

    def heads(col0):
        x = proj[:, col0:col0 + QKV].reshape(dec_batch, t_new, n_g, HPG, HEAD_DIM)
        return jnp.transpose(x, (0, 2, 3, 1, 4))

    q, kn, vn = heads(C_Q), heads(C_K), heads(C_V)
    native = [jnp.transpose(c, (0, 2, 3, 1)) for kv in caches for c in kv]
    slopes = jnp.concatenate([_alibi_slopes(g) for g in range(n_g)])

    new_q = pl.BlockSpec((None, n_g, hps, t_new, HEAD_DIM), lambda b, h: (b, 0, h, 0, 0))
    cache_specs = [pl.BlockSpec((None, hps, HEAD_DIM, c.shape[-1]), lambda b, h: (b, h, 0, 0))
                   for c in native]
    in_specs = [pl.BlockSpec(memory_space=pltpu.SMEM), new_q, new_q, new_q] + cache_specs
    out_specs = [pl.BlockSpec((None, hps, t_new, HEAD_DIM), lambda b, h: (b, h, 0, 0))] + cache_specs
    out_shape = ([jax.ShapeDtypeStruct((dec_batch, HPG, t_new, HEAD_DIM), F32)]
                 + [jax.ShapeDtypeStruct(c.shape, F32) for c in native])
    grid = (dec_batch, HPG // hps)
    if scan_prep is None:
        outs = pl.pallas_call(
            functools.partial(_decode_kernel, t_new=t_new, hps=hps),
            grid=grid, in_specs=in_specs, out_specs=out_specs, out_shape=out_shape,
            compiler_params=_cparams(("arbitrary", "arbitrary")),
            name="decode_attn",
        )(slopes, q, kn, vn, *native)
        scan_out = ()
    else:
        assert grid == (seq // CHUNK, 2)
        n = batch * (RW_W // LANES)
        tok = pl.BlockSpec((batch, CHUNK, RW_W), lambda c, h: (0, c, 0))
        st = pl.BlockSpec((n, LANES, LANES), lambda c, h: (0, 0, 0))
        sq = pltpu.VMEM((n, LANES, LANES), F32)
        outs = pl.pallas_call(
            functools.partial(_decode_scan_kernel, t_new=t_new, hps=hps, nb=batch),
            grid=grid,
            in_specs=in_specs + [tok] * 6,
            out_specs=out_specs + [tok, st],
            out_shape=out_shape + [jax.ShapeDtypeStruct((batch, seq, RW_W), F32),
                                   jax.ShapeDtypeStruct((n, LANES, LANES), F32)],
            scratch_shapes=[sq, sq, sq, sq, pltpu.VMEM((n, LANES, 2 * LANES), F32)],
            compiler_params=_cparams(("arbitrary", "arbitrary")),
            name="decode_attn_rwkv_scan",
        )(slopes, q, kn, vn, *native, *[x.reshape(batch, seq, RW_W) for x in scan_prep])
        scan_out = (outs[-2].reshape(batch * seq, RW_W), _unpair_state(outs[-1], batch))
        outs = outs[:-2]
    o_a = jnp.transpose(outs[0], (0, 2, 1, 3)).reshape(dec_batch * t_new, ATT_W)
    new = [jnp.transpose(c, (0, 3, 1, 2)) for c in outs[1:]]
    return (o_a, new[0::2], new[1::2]) + scan_out


def _prep_kernel(cur_ref, wa_ref, pcur_ref, pwa_ref, fcur_ref, fwa_ref, mu_ref, muwa_ref, lora_ref,
                 w0_ref, a0_ref, kk_ref, ka_ref, rk_ref, bd_ref,
                 r_o, lw_o, k_o, v_o, kkn_o, b_o, bonus_o, *, seq_len, tm):
    i = pl.program_id(0)
    row = lax.broadcasted_iota(jnp.int32, (tm, 1), 0)

    def shifted(cur, p8, first):
        rolled = pltpu.roll(cur, 1, 0)
        if seq_len >= tm:
            at_start = (i * tm) % seq_len == 0
            edge = jnp.where(at_start, first, p8[7:8, :])
            return jnp.where(row == 0, edge, rolled)
        return jnp.where(row % seq_len == 0, first, rolled)

    cur = cur_ref[...]
    wa = wa_ref[...]
    xs = cur + (shifted(cur, pcur_ref[...], fcur_ref[...]) - cur) * mu_ref[...]
    xwa = wa + (shifted(wa, pwa_ref[...], fwa_ref[...]) - wa) * muwa_ref[...]
    r = xs[:, 0:RW_W]
    kr = xs[:, RW_W:2 * RW_W]
    vr = xs[:, 2 * RW_W:3 * RW_W]
    lane = lax.broadcasted_iota(jnp.int32, (1, LANES), 1)
    lin = jnp.where(lane < LORA, jnp.tanh(xwa), xwa).astype(BF16)
    lo = jnp.dot(lin, lora_ref[...], preferred_element_type=F32)
    z = -(w0_ref[...] + lo[:, 0:RW_W])
    softplus = jnp.maximum(z, 0.0) + jnp.log1p(jnp.exp(-jnp.abs(z)))
    lw = -jnp.exp(-softplus - 0.5)
    a = jax.nn.sigmoid(a0_ref[...] + lo[:, RW_W:2 * RW_W])
    kk = kr * kk_ref[...]
    k_mod = kr * (1.0 + (a - 1.0) * ka_ref[...])
    rkk = r * k_mod * rk_ref[...]
    bd = bd_ref[...]
    r_o[...] = r
    lw_o[...] = lw
    k_o[...] = k_mod
    v_o[...] = vr
    for c in range(RW_W // LANES):
        cols = slice(c * LANES, (c + 1) * LANES)
        kc = kk[:, cols]
        nrm = jnp.maximum(jnp.sqrt(_segsum(kc * kc, bd)), L2_EPS)
        kkn = kc / nrm
        kkn_o[:, cols] = kkn
        b_o[:, cols] = kkn * a[:, cols]
        bonus_o[:, cols] = _segsum(rkk[:, cols], bd) * vr[:, cols]


def _rwkv_prep(proj, first, mu, lora_w, w0, a0, k_k, k_a, r_k, bd, seq_len, tm):
    n = proj.shape[0]
    main_w = 3 * RW_W
    cb, wb = C_SH // main_w, (C_SH + main_w) // LANES
    per_seq_first = seq_len >= tm
    if per_seq_first:
        f_main = first[:, None, 0:main_w]
        f_wa = first[:, None, main_w:]
        fm_spec = pl.BlockSpec((None, 1, main_w), lambda i: ((i * tm) // seq_len, 0, 0))
        fw_spec = pl.BlockSpec((None, 1, LANES), lambda i: ((i * tm) // seq_len, 0, 0))
    else:
        rep = jnp.repeat(first, seq_len, axis=0)
        f_main, f_wa = rep[:, 0:main_w], rep[:, main_w:]
        fm_spec = pl.BlockSpec((tm, main_w), lambda i: (i, 0))
        fw_spec = pl.BlockSpec((tm, LANES), lambda i: (i, 0))
    vec = lambda w: pl.BlockSpec((1, w), lambda i: (0, 0))
    out_spec = pl.BlockSpec((tm, RW_W), lambda i: (i, 0))
    out_sds = jax.ShapeDtypeStruct((n, RW_W), F32)
    return pl.pallas_call(
        functools.partial(_prep_kernel, seq_len=seq_len, tm=tm),
        grid=(n // tm,),
        in_specs=[
            pl.BlockSpec((tm, main_w), lambda i: (i, cb)),
            pl.BlockSpec((tm, LANES), lambda i: (i, wb)),
            pl.BlockSpec((8, main_w), lambda i: (jnp.maximum(i * (tm // 8) - 1, 0), cb)),
            pl.BlockSpec((8, LANES), lambda i: (jnp.maximum(i * (tm // 8) - 1, 0), wb)),
            fm_spec, fw_spec,
            vec(main_w), vec(LANES),
            pl.BlockSpec((LANES, 2 * RW_W), lambda i: (0, 0)),
            vec(RW_W), vec(RW_W), vec(RW_W), vec(RW_W), vec(RW_W),
            pl.BlockSpec((LANES, LANES), lambda i: (0, 0)),
        ],
        out_specs=[out_spec] * 7,
        out_shape=[out_sds] * 7,
        compiler_params=_cparams(("arbitrary",)),
        name="rwkv_prep",
    )(proj, proj, proj, proj, f_main, f_wa, mu[:, 0:main_w], mu[:, main_w:], lora_w,
      w0, a0, k_k, k_a, r_k, bd)


def _mm(a, b):
    return jnp.dot(a.astype(BF16), b.astype(BF16), preferred_element_type=F32)


def _mm_nt(a, b):
    return lax.dot_general(a.astype(BF16), b.astype(BF16), (((1,), (1,)), ((), ())),
                           preferred_element_type=F32)


def _mm_tn(a, b):
    return lax.dot_general(a.astype(BF16), b.astype(BF16), (((0,), (0,)), ((), ())),
                           preferred_element_type=F32)


SPLIT_STEPS = 3


def _scan_chains(seqs):
    C = CHUNK
    ti = lax.broadcasted_iota(jnp.int32, (C, C), 0)
    tj = lax.broadcasted_iota(jnp.int32, (C, C), 1)
    tri = (ti >= tj).astype(F32)
    in_a = lax.broadcasted_iota(jnp.int32, (1, LANES), 1) < HEAD_DIM

    def stack(x):
        return jnp.concatenate([jnp.where(in_a, x, 0.0), jnp.where(in_a, 0.0, x)], axis=0)

    chains = []
    for r, lw, k, v, kk, b in seqs:
        cum = jnp.dot(tri, lw, precision=lax.Precision.HIGHEST, preferred_element_type=F32)
        cum_end = cum[C - 1:C, :]
        w_in = jnp.exp(-cum)
        w_rest = jnp.exp(cum_end - cum)
        vals = (("a", jnp.exp(cum - lw) * (-kk)), ("b", b * w_in), ("k", k * w_in), ("r", r * jnp.exp(cum)),
                ("v", v), ("bh", b * w_rest), ("kh", k * w_rest))
        w_end = jnp.exp(cum_end)
        for p in range(RW_W // LANES):
            cols = slice(p * LANES, (p + 1) * LANES)
            ch = {name: stack(x[:, cols]) for name, x in vals}
            ch["w_end"] = w_end[:, cols]
            chains.append(ch)
    return chains


def _scan_masks():
    C = CHUNK
    si = lax.broadcasted_iota(jnp.int32, (2 * C, 2 * C), 0)
    sj = lax.broadcasted_iota(jnp.int32, (2 * C, 2 * C), 1)
    same = (si // C) == (sj // C)
    strict = jnp.logical_and(same, (si % C) > (sj % C))
    incl = jnp.logical_and(same, (si % C) >= (sj % C))
    return strict, incl, (si == sj).astype(F32)


def _double(prod, lp, steps):
    n = 2 * CHUNK
    for _ in range(steps):
        res = [_mm(jnp.concatenate([a, x], axis=0), x) for a, x in zip(prod, lp)]
        prod = [a + r[0:n] for a, r in zip(prod, res)]
        lp = [r[n:2 * n] for r in res]
    return prod, lp


def _scan_first_half(chains):
    C = CHUNK
    strict, incl, eye = _scan_masks()
    g = [_mm_nt(jnp.concatenate([ch["a"], ch["r"]], axis=0), jnp.concatenate([ch["b"], ch["k"]], axis=0))
         for ch in chains]
    ab = [jnp.where(strict, x[0:2 * C, 0:2 * C], 0.0) for x in g]
    ak = [jnp.where(strict, x[0:2 * C, 2 * C:4 * C], 0.0) for x in g]
    rbk = [jnp.concatenate([jnp.where(incl, x[2 * C:4 * C, 0:2 * C], 0.0),
                            jnp.where(incl, x[2 * C:4 * C, 2 * C:4 * C], 0.0)], axis=1) for x in g]
    akv = [_mm(x, ch["v"]) for x, ch in zip(ak, chains)]
    tinv, lp = _double([eye + x for x in ab], [_mm(x, x) for x in ab], SPLIT_STEPS)
    return tinv, lp, akv, rbk


def _scan_second_half(chains, tinv, lp, akv, rbk, zs):
    C = CHUNK
    eye = _scan_masks()[2]
    n = len(chains)
    levels = (C - 1).bit_length()
    tinv, lp = _double(tinv, lp, levels - 2 - SPLIT_STEPS)
    tinv = [a + _mm(a, x) for a, x in zip(tinv, lp)]
    au = [_mm(tinv[i], jnp.concatenate([chains[i]["a"], akv[i]], axis=1)) for i in range(n)]
    zeros = jnp.zeros((2 * C, LANES), F32)
    rhs = [jnp.concatenate([au[i], jnp.concatenate([zeros, chains[i]["v"]], axis=1)], axis=0)
           for i in range(n)]
    mn = [_mm_tn(jnp.concatenate([chains[i]["bh"], chains[i]["kh"]], axis=0), rhs[i])
          for i in range(n)]
    ry = [_mm(rbk[i], rhs[i]) for i in range(n)]
    yz = [_mm(jnp.concatenate([chains[i]["r"] + ry[i][:, 0:LANES],
                               mn[i][:, 0:LANES] + eye * chains[i]["w_end"]], axis=0), zs[i])
          for i in range(n)]
    z_news = [yz[i][2 * C:2 * C + LANES] + mn[i][:, LANES:2 * LANES] for i in range(n)]
    y_st = [yz[i][0:2 * C] + ry[i][:, LANES:2 * LANES] for i in range(n)]
    return [y[0:C] + y[C:2 * C] for y in y_st], z_news


def _store_scan(ys, z_news, y_ref, zt_ref, z_scr, nb):
    n_pairs = RW_W // LANES
    y_ref[...] = jnp.stack([jnp.concatenate(ys[bi * n_pairs:(bi + 1) * n_pairs], axis=1)
                            for bi in range(nb)], axis=0)
    z_scr[...] = jnp.stack(z_news, axis=0)
    zt_ref[...] = jnp.stack([z.T for z in z_news], axis=0)


def _scan_kernel(r_ref, lw_ref, k_ref, v_ref, kk_ref, b_ref, y_ref, zt_ref, z_scr, *, nb):
    @pl.when(pl.program_id(0) == 0)
    def _():
        z_scr[...] = jnp.zeros_like(z_scr)

    chains = _scan_chains([tuple(ref[bi] for ref in (r_ref, lw_ref, k_ref, v_ref, kk_ref, b_ref))
                           for bi in range(nb)])
    ys, z_news = _scan_second_half(chains, *_scan_first_half(chains),
                                   [z_scr[i] for i in range(len(chains))])
    _store_scan(ys, z_news, y_ref, zt_ref, z_scr, nb)


def _rwkv_scan(prep, batch, seq):
    assert seq % CHUNK == 0
    n_pairs = RW_W // LANES
    tok = pl.BlockSpec((batch, CHUNK, RW_W), lambda c: (0, c, 0))
    st = pl.BlockSpec((batch * n_pairs, LANES, LANES), lambda c: (0, 0, 0))
    y, zt = pl.pallas_call(
        functools.partial(_scan_kernel, nb=batch),
        grid=(seq // CHUNK,),
        in_specs=[tok] * 6,
        out_specs=[tok, st],
        out_shape=[jax.ShapeDtypeStruct((batch, seq, RW_W), F32),
                   jax.ShapeDtypeStruct((batch * n_pairs, LANES, LANES), F32)],
        scratch_shapes=[pltpu.VMEM((batch * n_pairs, LANES, LANES), F32)],
        compiler_params=_cparams(("arbitrary",)),
        name="rwkv_scan",
    )(*[x.reshape(batch, seq, RW_W) for x in prep])
    return y.reshape(batch * seq, RW_W), _unpair_state(zt, batch)


def _seq_scan_kernel(r_ref, lw_ref, k_ref, v_ref, kk_ref, b_ref, s_ref, y_ref, so_ref, w_scr, *, t_new):
    w_scr[...] = jnp.exp(lw_ref[...])

    def body(vi, carry):
        sv = s_ref[vi]
        for t in range(t_new):
            sa = jnp.sum(sv * kk_ref[t], axis=0, keepdims=True)
            sv = sv * w_scr[t] - sa * b_ref[t] + v_ref[t, pl.ds(vi, 1), :] * k_ref[t]
            y_ref[t, pl.ds(vi, 1), :] = jnp.sum(sv * r_ref[t], axis=0, keepdims=True)
        so_ref[vi] = sv
        return carry

    lax.fori_loop(0, HEAD_DIM, body, 0)


def _seq_scan(prep, state, dec_batch, t_new):
    def lanes(x):
        return jnp.transpose(x.reshape(dec_batch, t_new, RWKV_HEADS, HEAD_DIM), (1, 2, 3, 0))

    vec = pl.BlockSpec((t_new, None, HEAD_DIM, dec_batch), lambda h: (0, h, 0, 0))
    st = pl.BlockSpec((None, HEAD_DIM, HEAD_DIM, dec_batch), lambda h: (h, 0, 0, 0))
    s_native = jnp.transpose(state, (1, 2, 3, 0))
    y, s_new = pl.pallas_call(
        functools.partial(_seq_scan_kernel, t_new=t_new),
        grid=(RWKV_HEADS,),
        in_specs=[vec] * 6 + [st],
        out_specs=[vec, st],
        out_shape=[jax.ShapeDtypeStruct((t_new, RWKV_HEADS, HEAD_DIM, dec_batch), F32),
                   jax.ShapeDtypeStruct(s_native.shape, F32)],
        scratch_shapes=[pltpu.VMEM((t_new, HEAD_DIM, dec_batch), F32)],
        compiler_params=_cparams(("arbitrary",)),
        name="rwkv_seq_scan",
    )(*[lanes(x) for x in prep], s_native)
    y = jnp.transpose(y, (3, 0, 1, 2)).reshape(dec_batch * t_new, RW_W)
    return y, jnp.transpose(s_new, (3, 0, 1, 2))


def _final_kernel(oa_ref, y_ref, bonus_ref, gates_ref, x_ref, lnw_ref, lnb_ref, pa_ref, pb_ref, wo_ref, bd_ref,
                  o_ref):
    o_a = oa_ref[...]
    gates = gates_ref[...]
    z_a = gates[:, 0:ATT_W]
    z_b = gates[:, ATT_W:ATT_W + RW_W]
    g_a = gates[:, ATT_W + RW_W:ATT_W + RW_W + 1024]
    g_b = gates[:, ATT_W + RW_W + 1024:]
    y = y_ref[...]
    bd = bd_ref[...]
    cols_out = []
    for c in range(RW_W // LANES):
        cols = slice(c * LANES, (c + 1) * LANES)
        yc = y[:, cols]
        mu = _segsum(yc, bd) * (1.0 / HEAD_DIM)
        d = yc - mu
        var = _segsum(d * d, bd) * (1.0 / HEAD_DIM)
        cols_out.append(d * lax.rsqrt(var + GN_EPS))
    yn = jnp.concatenate(cols_out, axis=1)
    o_b = yn * lnw_ref[...] + lnb_ref[...] + bonus_ref[...]
    h_a = jnp.dot((o_a * jax.nn.silu(z_a)).astype(BF16), pa_ref[...], preferred_element_type=F32)
    h_b = jnp.dot((o_b * jax.nn.silu(z_b)).astype(BF16), pb_ref[...], preferred_element_type=F32)
    mixed = jax.nn.sigmoid(g_a) * h_a + jax.nn.sigmoid(g_b) * h_b
    o_ref[...] = x_ref[...] + jnp.dot(mixed.astype(BF16), wo_ref[...], preferred_element_type=F32)


def _final(o_a, y_raw, bonus, proj, x, ln_w, ln_b, p_a, p_b, w_out, bd, tm):
    n, d = x.shape
    tm = min(tm, n)
    gates_w = C_Q - C_GATES
    row = lambda w: pl.BlockSpec((tm, w), lambda i: (i, 0))
    full = lambda a: pl.BlockSpec(a.shape, lambda i: (0, 0))
    return pl.pallas_call(
        _final_kernel,
        grid=(n // tm,),
        in_specs=[row(ATT_W), row(RW_W), row(RW_W), row(gates_w), row(d),
                  full(ln_w), full(ln_b), full(p_a), full(p_b), full(w_out), full(bd)],
        out_specs=row(d),
        out_shape=jax.ShapeDtypeStruct((n, d), F32),
        compiler_params=_cparams(("arbitrary",)),
        name="gated_out",
    )(o_a, y_raw, bonus, proj, x, ln_w, ln_b, p_a, p_b, w_out, bd)


def _permute_w_in(w_in):
    q, k, v, sh, gates = (w_in[:, 0:QKV], w_in[:, QKV:2 * QKV], w_in[:, 2 * QKV:3 * QKV],
                          w_in[:, 3 * QKV:3 * QKV + SHIFT_COLS], w_in[:, 3 * QKV + SHIFT_COLS:])
    pad = jnp.zeros((w_in.shape[0], P_COLS - IN_COLS), w_in.dtype)
    return jnp.concatenate([gates, q, k, v, sh, pad], axis=1).astype(BF16)


def _layer(h_p, h_s, batch, seq, dec_batch, dec_seq, shift_s, state_s, caches, lw):
    (norm_g, w_perm, shift_mu, qg, kg, lora_w, w0, a0, k_k, k_a, r_k, ln_w, ln_b, p_a, p_b, w_out,
     bd) = lw
    n_s = dec_batch * dec_seq
    prep_args = (shift_mu, lora_w, w0, a0, k_k, k_a, r_k, bd)
    out_args = (ln_w, ln_b, p_a, p_b, w_out, bd)

    proj_p = _inproj(h_p, norm_g, w_perm, qg, kg, bd, tm=2048)
    prep_p = _rwkv_prep(proj_p, jnp.zeros((batch, SHIFT_COLS), F32), *prep_args, seq_len=seq, tm=512)
    o_a_p = _band_attention(proj_p, batch, seq, tile=2048)
    pv = proj_p.reshape(batch, seq, P_COLS)
    st_p = []
    for g, (window, _) in enumerate(ATT_GROUPS):
        keep = min(window, seq)
        for c0 in (C_K, C_V):
            st_p.append(pv[:, seq - keep:, c0 + g * ATT_W:c0 + (g + 1) * ATT_W]
                        .reshape(batch, keep, HPG, HEAD_DIM))

    proj_s = _inproj(h_s, norm_g, w_perm, qg, kg, bd, tm=min(1024, n_s))
    prep_s = _rwkv_prep(proj_s, shift_s, *prep_args, seq_len=dec_seq, tm=min(512, n_s))
    if seq % CHUNK == 0 and dec_batch == seq // CHUNK:
        o_a, new_k, new_v, y_p, state_p = _decode_attention(
            proj_s, caches, dec_batch, dec_seq, scan_prep=prep_p[:6], batch=batch, seq=seq)
    else:
        o_a, new_k, new_v = _decode_attention(proj_s, caches, dec_batch, dec_seq)
        y_p, state_p = _rwkv_scan(prep_p[:6], batch, seq)
    y_s, state_s_new = _seq_scan(prep_s[:6], state_s, dec_batch, dec_seq)

    out_p = _final(o_a_p, y_p, prep_p[6], proj_p, h_p, *out_args, tm=512)
    out_s = _final(o_a, y_s, prep_s[6], proj_s, h_s, *out_args, tm=512)
    st_p += [state_p, pv[:, seq - 1, C_SH:C_SH + SHIFT_COLS]]
    st_s = [c for kv in zip(new_k, new_v) for c in kv]
    st_s += [state_s_new, proj_s.reshape(dec_batch, dec_seq, P_COLS)[:, dec_seq - 1, C_SH:C_SH + SHIFT_COLS]]
    return out_p, out_s, st_p, st_s


def kernel(x_prompt, x_sample, cache_k_g0, cache_v_g0, cache_k_g1, cache_v_g1, cache_k_g2, cache_v_g2, state_rwkv, state_shift, norm_g, w_in, shift_mu, q_norm_g, k_norm_g, w0, w2, a0, a2, k_k, k_a, r_k, ln_x_w, ln_x_b, p_a, p_b, w_out):
    depth = norm_g.shape[0]
    batch, seq, d_model = x_prompt.shape
    dec_batch, dec_seq, _ = x_sample.shape
    bd = _bd_ones()
    h_p = x_prompt.reshape(batch * seq, d_model)
    h_s = x_sample.reshape(dec_batch * dec_seq, d_model)
    prompt_states, sample_states = [], []
    for layer in range(depth):
        zero = jnp.zeros((LORA, RW_W), F32)
        lora_w = jnp.concatenate([jnp.concatenate([w2[layer], zero], axis=1),
                                  jnp.concatenate([zero, a2[layer]], axis=1)], axis=0).astype(BF16)
        row = lambda a: a.reshape(1, -1)
        lw = (row(norm_g[layer]), _permute_w_in(w_in[layer]), row(shift_mu[layer]),
              jnp.tile(row(q_norm_g[layer]), (1, LANES // HEAD_DIM)),
              jnp.tile(row(k_norm_g[layer]), (1, LANES // HEAD_DIM)),
              lora_w, row(w0[layer]), row(a0[layer]), row(k_k[layer]), row(k_a[layer]),
              row(r_k[layer]), row(ln_x_w[layer]), row(ln_x_b[layer]),
              p_a[layer].astype(BF16), p_b[layer].astype(BF16), w_out[layer].astype(BF16), bd)
        caches = [(cache_k_g0[layer], cache_v_g0[layer]), (cache_k_g1[layer], cache_v_g1[layer]),
                  (cache_k_g2[layer], cache_v_g2[layer])]
        h_p, h_s, st_p, st_s = _layer(h_p, h_s, batch, seq, dec_batch, dec_seq, state_shift[layer],
                                      state_rwkv[layer], caches, lw)
        prompt_states.append(st_p)
        sample_states.append(st_s)
    ps = [jnp.stack(t) for t in zip(*prompt_states)]
    ss = [jnp.stack(t) for t in zip(*sample_states)]
    return (h_p.reshape(batch, seq, d_model), h_s.reshape(dec_batch, dec_seq, d_model),
            *ps, *ss)
```

```python
import functools

import jax
import jax.numpy as jnp
from jax import lax
from jax.experimental import pallas as pl
from jax.experimental.pallas import tpu as pltpu

F32 = jnp.float32
BF16 = jnp.bfloat16

HEAD_DIM = 64
HPG = 8
ATT_GROUPS = ((128, 1), (512, 4), (2048, 16))
N_ATT_HEADS = 24
RWKV_HEADS = 8
LORA = 64
QBLOCK = 128
RMS_EPS = 1e-6
GN_EPS = 64e-5
L2_EPS = 1e-12
CHUNK = 64

LANES = 128
NEW_PAD = 16
VMEM_LIMIT = 52 * 1024 * 1024

QKV = N_ATT_HEADS * HEAD_DIM
ATT_W = HPG * HEAD_DIM
RW_W = RWKV_HEADS * HEAD_DIM
C_GATES = 0
C_Q = 3072
C_K = C_Q + QKV
C_V = C_K + QKV
C_SH = C_V + QKV
SHIFT_COLS = 3 * RW_W + 2 * LORA
IN_COLS = C_SH + SHIFT_COLS
TN = 512
P_COLS = pl.cdiv(IN_COLS, TN) * TN


def _cparams(sem):
    return pltpu.CompilerParams(dimension_semantics=sem, vmem_limit_bytes=VMEM_LIMIT)


def _segsum(x, bd):
    hi = x.astype(BF16)
    lo = (x - hi.astype(F32)).astype(BF16)
    return (jnp.dot(hi, bd, preferred_element_type=F32)
            + jnp.dot(lo, bd, preferred_element_type=F32))


def _bd_ones():
    i = jnp.arange(LANES)
    return (i[:, None] // HEAD_DIM == i[None, :] // HEAD_DIM).astype(BF16)


def _alibi_slopes(g):
    return jnp.exp2(-8.0 * (jnp.arange(HPG, dtype=F32) + (g * HPG + 1)) / N_ATT_HEADS)


def _inproj_kernel(x_ref, g_ref, w_ref, qg_ref, kg_ref, bd_ref, o_ref, xn_ref):
    j = pl.program_id(1)

    @pl.when(j == 0)
    def _():
        x = x_ref[...]
        ms = jnp.mean(x * x, axis=-1, keepdims=True)
        xn_ref[...] = (x * lax.rsqrt(ms + RMS_EPS) * g_ref[...]).astype(BF16)

    t = jnp.dot(xn_ref[...], w_ref[...], preferred_element_type=F32)
    q_lo, k_lo, k_hi = C_Q // TN, C_K // TN, C_V // TN

    @pl.when(jnp.logical_or(j < q_lo, j >= k_hi))
    def _():
        o_ref[...] = t

    @pl.when(jnp.logical_and(j >= q_lo, j < k_hi))
    def _():
        gain = jnp.where(j < k_lo, qg_ref[...], kg_ref[...])
        bd = bd_ref[...]
        for c in range(TN // LANES):
            tc = t[:, c * LANES:(c + 1) * LANES]
            ms = _segsum(tc * tc, bd) * (1.0 / HEAD_DIM)
            o_ref[:, c * LANES:(c + 1) * LANES] = tc * lax.rsqrt(ms + RMS_EPS) * gain


def _inproj(x, norm_g, w_perm, qg, kg, bd, tm):
    n, d = x.shape
    return pl.pallas_call(
        _inproj_kernel,
        grid=(n // tm, P_COLS // TN),
        in_specs=[
            pl.BlockSpec((tm, d), lambda i, j: (i, 0)),
            pl.BlockSpec((1, d), lambda i, j: (0, 0)),
            pl.BlockSpec((d, TN), lambda i, j: (0, j)),
            pl.BlockSpec((1, LANES), lambda i, j: (0, 0)),
            pl.BlockSpec((1, LANES), lambda i, j: (0, 0)),
            pl.BlockSpec((LANES, LANES), lambda i, j: (0, 0)),
        ],
        out_specs=pl.BlockSpec((tm, TN), lambda i, j: (i, j)),
        out_shape=jax.ShapeDtypeStruct((n, P_COLS), F32),
        scratch_shapes=[pltpu.VMEM((tm, d), BF16)],
        compiler_params=_cparams(("arbitrary", "arbitrary")),
        name="inproj",
    )(x, norm_g, w_perm, qg, kg, bd)


def _band_kernel(sl_ref, *refs, tile):
    n_g = len(ATT_GROUPS)
    ins, o_ref, scr = refs[0:5 * n_g], refs[5 * n_g], refs[5 * n_g + 1:]
    n = pl.program_id(1)
    p = pl.program_id(2)
    lane = lax.broadcasted_iota(jnp.int32, (1, LANES), 1)
    mask_a = lane < HEAD_DIM
    qi = lax.broadcasted_iota(jnp.int32, (QBLOCK, 2 * QBLOCK), 0)
    ki = lax.broadcasted_iota(jnp.int32, (QBLOCK, 2 * QBLOCK), 1)
    rel = qi + QBLOCK - ki
    band = jnp.logical_and(rel >= 0, rel <= QBLOCK)
    band_first = jnp.logical_and(band, jnp.logical_or(ki >= QBLOCK, n > 0))

    for g, (_, dil) in enumerate(ATT_GROUPS):
        q_ref, kp_ref, kc_ref, vp_ref, vc_ref = ins[5 * g:5 * g + 5]
        acc_scr, m_scr, l_scr = scr[3 * g:3 * g + 3]
        dist = (rel * dil).astype(F32)
        alibi = [-(sl_ref[g * HPG + 2 * p + h] * dist) for h in range(2)]
        bias = [jnp.where(band, a, -jnp.inf) for a in alibi]
        bias_first = [jnp.where(band_first, a, -jnp.inf) for a in alibi]

        def rows(r, s, dil=dil):
            if dil == 1:
                return pl.ds(s * QBLOCK, QBLOCK)
            return pl.ds(r + s * QBLOCK * dil, QBLOCK, stride=dil)

        for r in range(dil):
            for s in range(tile // (dil * QBLOCK)):
                blk_bias = bias_first if s == 0 else bias
                q = q_ref[rows(r, s), :] * (1.0 / 8.0)
                if s == 0:
                    k_prev, v_prev = kp_ref[rows(r, 0), :], vp_ref[rows(r, 0), :]
                else:
                    k_prev, v_prev = kc_ref[rows(r, s - 1), :], vc_ref[rows(r, s - 1), :]
                k = jnp.concatenate([k_prev, kc_ref[rows(r, s), :]], axis=0).astype(BF16)
                v = jnp.concatenate([v_prev, vc_ref[rows(r, s), :]], axis=0).astype(BF16)
                outs = []
                for h in range(2):
                    hm = mask_a if h == 0 else jnp.logical_not(mask_a)
                    qh = jnp.where(hm, q, 0.0).astype(BF16)
                    sc = lax.dot_general(qh, k, (((1,), (1,)), ((), ())), preferred_element_type=F32)
                    sc = sc + blk_bias[h]
                    m = jnp.max(sc, axis=-1, keepdims=True)
                    pr = jnp.exp(sc - m)
                    l = jnp.sum(pr, axis=-1, keepdims=True)
                    acc = jnp.dot(pr.astype(BF16), v, preferred_element_type=F32)
                    outs.append((acc, m, l))
                (acc0, m0, l0), (acc1, m1, l1) = outs
                acc_scr[rows(r, s), :] = jnp.where(mask_a, acc0, acc1)
                m_scr[rows(r, s), :] = jnp.where(mask_a, m0, m1)
                l_scr[rows(r, s), :] = jnp.where(mask_a, l0, l1)

    parts = [(scr[3 * g][...], scr[3 * g + 1][...], scr[3 * g + 2][...]) for g in range(n_g)]
    m_all = functools.reduce(jnp.maximum, [m for _, m, _ in parts])
    num = sum(jnp.exp(m - m_all) * a for a, m, _ in parts)
    den = sum(jnp.exp(m - m_all) * l for _, m, l in parts)
    o_ref[...] = num / den


def _band_attention(proj, batch, seq, tile):
    n_g = len(ATT_GROUPS)
    tile = min(tile, seq)
    assert seq % tile == 0
    nt = seq // tile
    in_specs = [pl.BlockSpec(memory_space=pltpu.SMEM)]
    for g, (window, dil) in enumerate(ATT_GROUPS):
        assert window // dil == QBLOCK
        halo = QBLOCK * dil
        assert tile % halo == 0
        nh = seq // halo
        cq, ck, cv = ((c + g * ATT_W) // LANES for c in (C_Q, C_K, C_V))

        def cur(col):
            return pl.BlockSpec((tile, LANES), lambda b, n, p: (b * nt + n, col + p))

        def prev(col, halo=halo, nh=nh):
            return pl.BlockSpec(
                (halo, LANES),
                lambda b, n, p: (b * nh + jnp.maximum(n * (tile // halo) - 1, 0), col + p))

        in_specs += [cur(cq), prev(ck), cur(ck), prev(cv), cur(cv)]
    slopes = jnp.concatenate([_alibi_slopes(g) for g in range(n_g)])
    return pl.pallas_call(
        functools.partial(_band_kernel, tile=tile),
        grid=(batch, nt, ATT_W // LANES),
        in_specs=in_specs,
        out_specs=pl.BlockSpec((tile, LANES), lambda b, n, p: (b * nt + n, p)),
        out_shape=jax.ShapeDtypeStruct((batch * seq, ATT_W), F32),
        scratch_shapes=[pltpu.VMEM((tile, LANES), F32)] * (3 * n_g),
        compiler_params=_cparams(("arbitrary", "arbitrary", "arbitrary")),
        name="band_attn",
    )(slopes, *([proj] * (5 * n_g)))


def _decode_body(sl_ref, q_ref, kn_ref, vn_ref, kb, vb, o_ref, ko, vo, hh, t_new, hps):
    n_g = len(ATT_GROUPS)
    ti_n = lax.broadcasted_iota(jnp.int32, (t_new, NEW_PAD), 0)
    tj_n = lax.broadcasted_iota(jnp.int32, (t_new, NEW_PAD), 1)
    pi = lax.broadcasted_iota(jnp.int32, (NEW_PAD, LANES), 0)
    pj = lax.broadcasted_iota(jnp.int32, (NEW_PAD, LANES), 1)
    place = jnp.logical_and(pj == pi + (LANES - t_new), pi < t_new).astype(BF16)
    tail = lax.broadcasted_iota(jnp.int32, (HEAD_DIM, LANES), 1) >= LANES - t_new
    nt = (((1,), (1,)), ((), ()))
    tn = (((0,), (0,)), ((), ()))
    zpad = jnp.zeros((NEW_PAD - t_new, HEAD_DIM), F32)

    combos = [(j, g) for j in range(hps) for g in range(n_g)]
    d_new = ti_n - tj_n
    kn = {c: jnp.concatenate([kn_ref[c[1], c[0]], zpad], axis=0) for c in combos}
    vn = {c: jnp.concatenate([vn_ref[c[1], c[0]], zpad], axis=0) for c in combos}
    q = {c: q_ref[c[1], c[0]].astype(BF16) for c in combos}
    s_buf = {(j, g): jnp.dot(q[j, g], kb[g][j].astype(BF16), preferred_element_type=F32)
             for j, g in combos}
    s_new = {c: lax.dot_general(q[c], kn[c].astype(BF16), nt, preferred_element_type=F32)
             for c in combos}
    p_buf, p_new, stats = {}, {}, {}
    for j, g in combos:
        wb, dil = ATT_GROUPS[g]
        slope = sl_ref[g * HPG + hh * hps + j]
        wi = lax.broadcasted_iota(jnp.int32, (t_new, wb), 1)
        ti = lax.broadcasted_iota(jnp.int32, (t_new, wb), 0)
        d_buf = wb + ti - wi
        ok_buf = wi >= ti
        ok_new = d_new >= 0
        if dil > 1:
            ok_buf = jnp.logical_and(ok_buf, (d_buf & (dil - 1)) == 0)
            ok_new = jnp.logical_and(ok_new, (d_new & (dil - 1)) == 0)
        sb = jnp.where(ok_buf, s_buf[j, g] * (1.0 / 8.0) - slope * d_buf.astype(F32), -jnp.inf)
        sn = jnp.where(ok_new, s_new[j, g] * (1.0 / 8.0) - slope * d_new.astype(F32), -jnp.inf)
        m = jnp.maximum(jnp.max(sb, axis=-1, keepdims=True), jnp.max(sn, axis=-1, keepdims=True))
        pb, pn = jnp.exp(sb - m), jnp.exp(sn - m)
        p_buf[j, g], p_new[j, g] = pb.astype(BF16), pn.astype(BF16)
        stats[j, g] = (m, jnp.sum(pb, axis=-1, keepdims=True) + jnp.sum(pn, axis=-1, keepdims=True))
    acc = {(j, g): lax.dot_general(p_buf[j, g], vb[g][j].astype(BF16), nt, preferred_element_type=F32)
           + jnp.dot(p_new[j, g], vn[j, g].astype(BF16), preferred_element_type=F32)
           for j, g in combos}
    for j in range(hps):
        m_all = functools.reduce(jnp.maximum, [stats[j, g][0] for g in range(n_g)])
        wts = [jnp.exp(stats[j, g][0] - m_all) for g in range(n_g)]
        num = sum(wts[g] * acc[j, g] for g in range(n_g))
        den = sum(wts[g] * stats[j, g][1] for g in range(n_g))
        o_ref[j] = num / den

    for j, g in combos:
        wb = ATT_GROUPS[g][0]
        for src, new, dst in ((kb[g], kn[j, g], ko[g]), (vb[g], vn[j, g], vo[g])):
            rolled = pltpu.roll(src[j], wb - t_new, 1)
            hi = new.astype(BF16)
            lo = (new - hi.astype(F32)).astype(BF16)
            placed = (lax.dot_general(hi, place, tn, preferred_element_type=F32)
                      + lax.dot_general(lo, place, tn, preferred_element_type=F32))
            if wb > LANES:
                dst[j, :, 0:wb - LANES] = rolled[:, 0:wb - LANES]
            dst[j, :, wb - LANES:wb] = jnp.where(tail, placed, rolled[:, wb - LANES:wb])


def _decode_kernel(sl_ref, q_ref, kn_ref, vn_ref, *refs, t_new, hps):
    n_c = 2 * len(ATT_GROUPS)
    _decode_body(sl_ref, q_ref, kn_ref, vn_ref, refs[0:n_c:2], refs[1:n_c:2], refs[n_c],
                 refs[n_c + 1::2], refs[n_c + 2::2], pl.program_id(1), t_new, hps)


def _decode_scan_kernel(sl_ref, q_ref, kn_ref, vn_ref, *refs, t_new, hps, nb):
    n_c = 2 * len(ATT_GROUPS)
    caches_in, tok = refs[0:n_c], refs[n_c:n_c + 6]
    o_ref = refs[n_c + 6]
    caches_out = refs[n_c + 7:2 * n_c + 7]
    y_ref, zt_ref, z_scr, tinv_scr, lp_scr, akv_scr, rbk_scr = refs[2 * n_c + 7:]
    c, half = pl.program_id(0), pl.program_id(1)
    seqs = [tuple(ref[bi] for ref in tok) for bi in range(nb)]
    n = nb * (RW_W // LANES)

    def decode(hh):
        _decode_body(sl_ref, q_ref, kn_ref, vn_ref, caches_in[0::2], caches_in[1::2], o_ref,
                     caches_out[0::2], caches_out[1::2], hh, t_new, hps)

    @pl.when(jnp.logical_and(c == 0, half == 0))
    def _():
        z_scr[...] = jnp.zeros_like(z_scr)

    @pl.when(half == 0)
    def _():
        decode(0)
        tinv, lp, akv, rbk = _scan_first_half(_scan_chains(seqs))
        tinv_scr[...] = jnp.stack(tinv, axis=0)
        lp_scr[...] = jnp.stack(lp, axis=0)
        akv_scr[...] = jnp.stack(akv, axis=0)
        rbk_scr[...] = jnp.stack(rbk, axis=0)

    @pl.when(half == 1)
    def _():
        decode(1)
        carried = [[scr[i] for i in range(n)] for scr in (tinv_scr, lp_scr, akv_scr, rbk_scr)]
        ys, z_news = _scan_second_half(_scan_chains(seqs), *carried, [z_scr[i] for i in range(n)])
        _store_scan(ys, z_news, y_ref, zt_ref, z_scr, nb)


def _unpair_state(zt, batch):
    zt = zt.reshape(batch, RW_W // LANES, LANES, LANES)
    s_a = zt[:, :, 0:HEAD_DIM, 0:HEAD_DIM]
    s_b = zt[:, :, HEAD_DIM:, HEAD_DIM:]
    return jnp.stack([s_a, s_b], axis=2).reshape(batch, RWKV_HEADS, HEAD_DIM, HEAD_DIM)


def _decode_attention(proj, caches, dec_batch, t_new, scan_prep=None, batch=0, seq=0):
    n_g = len(ATT_GROUPS)
    hps = HPG // 2
    assert all(k.shape[1] == w for (k, _), (w, _) in zip(caches, ATT_GROUPS))

    def heads(col0):
        x = proj[:, col0:col0 + QKV].reshape(dec_batch, t_new, n_g, HPG, HEAD_DIM)
        return jnp.transpose(x, (0, 2, 3, 1, 4))

    q, kn, vn = heads(C_Q), heads(C_K), heads(C_V)
    native = [jnp.transpose(c, (0, 2, 3, 1)) for kv in caches for c in kv]
    slopes = jnp.concatenate([_alibi_slopes(g) for g in range(n_g)])

    new_q = pl.BlockSpec((None, n_g, hps, t_new, HEAD_DIM), lambda b, h: (b, 0, h, 0, 0))
    cache_specs = [pl.BlockSpec((None, hps, HEAD_DIM, c.shape[-1]), lambda b, h: (b, h, 0, 0))
                   for c in native]
    in_specs = [pl.BlockSpec(memory_space=pltpu.SMEM), new_q, new_q, new_q] + cache_specs
    out_specs = [pl.BlockSpec((None, hps, t_new, HEAD_DIM), lambda b, h: (b, h, 0, 0))] + cache_specs
    out_shape = ([jax.ShapeDtypeStruct((dec_batch, HPG, t_new, HEAD_DIM), F32)]
                 + [jax.ShapeDtypeStruct(c.shape, F32) for c in native])
    grid = (dec_batch, HPG // hps)
    if scan_prep is None:
        outs = pl.pallas_call(
            functools.partial(_decode_kernel, t_new=t_new, hps=hps),
            grid=grid, in_specs=in_specs, out_specs=out_specs, out_shape=out_shape,
            compiler_params=_cparams(("arbitrary", "arbitrary")),
            name="decode_attn",
        )(slopes, q, kn, vn, *native)
        scan_out = ()
    else:
        assert grid == (seq // CHUNK, 2)
        n = batch * (RW_W // LANES)
        tok = pl.BlockSpec((batch, CHUNK, RW_W), lambda c, h: (0, c, 0))
        st = pl.BlockSpec((n, LANES, LANES), lambda c, h: (0, 0, 0))
        sq = pltpu.VMEM((n, LANES, LANES), F32)
        outs = pl.pallas_call(
            functools.partial(_decode_scan_kernel, t_new=t_new, hps=hps, nb=batch),
            grid=grid,
            in_specs=in_specs + [tok] * 6,
            out_specs=out_specs + [tok, st],
            out_shape=out_shape + [jax.ShapeDtypeStruct((batch, seq, RW_W), F32),
                                   jax.ShapeDtypeStruct((n, LANES, LANES), F32)],
            scratch_shapes=[sq, sq, sq, sq, pltpu.VMEM((n, LANES, 2 * LANES), F32)],
            compiler_params=_cparams(("arbitrary", "arbitrary")),
            name="decode_attn_rwkv_scan",
        )(slopes, q, kn, vn, *native, *[x.reshape(batch, seq, RW_W) for x in scan_prep])
        scan_out = (outs[-2].reshape(batch * seq, RW_W), _unpair_state(outs[-1], batch))
        outs = outs[:-2]
    o_a = jnp.transpose(outs[0], (0, 2, 1, 3)).reshape(dec_batch * t_new, ATT_W)
    new = [jnp.transpose(c, (0, 3, 1, 2)) for c in outs[1:]]
    return (o_a, new[0::2], new[1::2]) + scan_out


def _prep_kernel(cur_ref, wa_ref, pcur_ref, pwa_ref, fcur_ref, fwa_ref, mu_ref, muwa_ref, lora_ref,
                 w0_ref, a0_ref, kk_ref, ka_ref, rk_ref, bd_ref,
                 r_o, lw_o, k_o, v_o, kkn_o, b_o, bonus_o, *, seq_len, tm):
    i = pl.program_id(0)
    row = lax.broadcasted_iota(jnp.int32, (tm, 1), 0)

    def shifted(cur, p8, first):
        rolled = pltpu.roll(cur, 1, 0)
        if seq_len >= tm:
            at_start = (i * tm) % seq_len == 0
            edge = jnp.where(at_start, first, p8[7:8, :])
            return jnp.where(row == 0, edge, rolled)
        return jnp.where(row % seq_len == 0, first, rolled)

    cur = cur_ref[...]
    wa = wa_ref[...]
    xs = cur + (shifted(cur, pcur_ref[...], fcur_ref[...]) - cur) * mu_ref[...]
    xwa = wa + (shifted(wa, pwa_ref[...], fwa_ref[...]) - wa) * muwa_ref[...]
    r = xs[:, 0:RW_W]
    kr = xs[:, RW_W:2 * RW_W]
    vr = xs[:, 2 * RW_W:3 * RW_W]
    lane = lax.broadcasted_iota(jnp.int32, (1, LANES), 1)
    lin = jnp.where(lane < LORA, jnp.tanh(xwa), xwa).astype(BF16)
    lo = jnp.dot(lin, lora_ref[...], preferred_element_type=F32)
    z = -(w0_ref[...] + lo[:, 0:RW_W])
    softplus = jnp.maximum(z, 0.0) + jnp.log1p(jnp.exp(-jnp.abs(z)))
    lw = -jnp.exp(-softplus - 0.5)
    a = jax.nn.sigmoid(a0_ref[...] + lo[:, RW_W:2 * RW_W])
    kk = kr * kk_ref[...]
    k_mod = kr * (1.0 + (a - 1.0) * ka_ref[...])
    rkk = r * k_mod * rk_ref[...]
    bd = bd_ref[...]
    r_o[...] = r
    lw_o[...] = lw
    k_o[...] = k_mod
    v_o[...] = vr
    for c in range(RW_W // LANES):
        cols = slice(c * LANES, (c + 1) * LANES)
        kc = kk[:, cols]
        nrm = jnp.maximum(jnp.sqrt(_segsum(kc * kc, bd)), L2_EPS)
        kkn = kc / nrm
        kkn_o[:, cols] = kkn
        b_o[:, cols] = kkn * a[:, cols]
        bonus_o[:, cols] = _segsum(rkk[:, cols], bd) * vr[:, cols]


def _rwkv_prep(proj, first, mu, lora_w, w0, a0, k_k, k_a, r_k, bd, seq_len, tm):
    n = proj.shape[0]
    main_w = 3 * RW_W
    cb, wb = C_SH // main_w, (C_SH + main_w) // LANES
    per_seq_first = seq_len >= tm
    if per_seq_first:
        f_main = first[:, None, 0:main_w]
        f_wa = first[:, None, main_w:]
        fm_spec = pl.BlockSpec((None, 1, main_w), lambda i: ((i * tm) // seq_len, 0, 0))
        fw_spec = pl.BlockSpec((None, 1, LANES), lambda i: ((i * tm) // seq_len, 0, 0))
    else:
        rep = jnp.repeat(first, seq_len, axis=0)
        f_main, f_wa = rep[:, 0:main_w], rep[:, main_w:]
        fm_spec = pl.BlockSpec((tm, main_w), lambda i: (i, 0))
        fw_spec = pl.BlockSpec((tm, LANES), lambda i: (i, 0))
    vec = lambda w: pl.BlockSpec((1, w), lambda i: (0, 0))
    out_spec = pl.BlockSpec((tm, RW_W), lambda i: (i, 0))
    out_sds = jax.ShapeDtypeStruct((n, RW_W), F32)
    return pl.pallas_call(
        functools.partial(_prep_kernel, seq_len=seq_len, tm=tm),
        grid=(n // tm,),
        in_specs=[
            pl.BlockSpec((tm, main_w), lambda i: (i, cb)),
            pl.BlockSpec((tm, LANES), lambda i: (i, wb)),
            pl.BlockSpec((8, main_w), lambda i: (jnp.maximum(i * (tm // 8) - 1, 0), cb)),
            pl.BlockSpec((8, LANES), lambda i: (jnp.maximum(i * (tm // 8) - 1, 0), wb)),
            fm_spec, fw_spec,
            vec(main_w), vec(LANES),
            pl.BlockSpec((LANES, 2 * RW_W), lambda i: (0, 0)),
            vec(RW_W), vec(RW_W), vec(RW_W), vec(RW_W), vec(RW_W),
            pl.BlockSpec((LANES, LANES), lambda i: (0, 0)),
        ],
        out_specs=[out_spec] * 7,
        out_shape=[out_sds] * 7,
        compiler_params=_cparams(("arbitrary",)),
        name="rwkv_prep",
    )(proj, proj, proj, proj, f_main, f_wa, mu[:, 0:main_w], mu[:, main_w:], lora_w,
      w0, a0, k_k, k_a, r_k, bd)


def _mm(a, b):
    return jnp.dot(a.astype(BF16), b.astype(BF16), preferred_element_type=F32)


def _mm_nt(a, b):
    return lax.dot_general(a.astype(BF16), b.astype(BF16), (((1,), (1,)), ((), ())),
                           preferred_element_type=F32)


def _mm_tn(a, b):
    return lax.dot_general(a.astype(BF16), b.astype(BF16), (((0,), (0,)), ((), ())),
                           preferred_element_type=F32)


SPLIT_STEPS = 3


def _scan_chains(seqs):
    C = CHUNK
    ti = lax.broadcasted_iota(jnp.int32, (C, C), 0)
    tj = lax.broadcasted_iota(jnp.int32, (C, C), 1)
    tri = (ti >= tj).astype(F32)
    in_a = lax.broadcasted_iota(jnp.int32, (1, LANES), 1) < HEAD_DIM

    def stack(x):
        return jnp.concatenate([jnp.where(in_a, x, 0.0), jnp.where(in_a, 0.0, x)], axis=0)

    chains = []
    for r, lw, k, v, kk, b in seqs:
        cum = jnp.dot(tri, lw, precision=lax.Precision.HIGHEST, preferred_element_type=F32)
        cum_end = cum[C - 1:C, :]
        w_in = jnp.exp(-cum)
        w_rest = jnp.exp(cum_end - cum)
        vals = (("a", jnp.exp(cum - lw) * (-kk)), ("b", b * w_in), ("k", k * w_in), ("r", r * jnp.exp(cum)),
                ("v", v), ("bh", b * w_rest), ("kh", k * w_rest))
        w_end = jnp.exp(cum_end)
        for p in range(RW_W // LANES):
            cols = slice(p * LANES, (p + 1) * LANES)
            ch = {name: stack(x[:, cols]) for name, x in vals}
            ch["w_end"] = w_end[:, cols]
            chains.append(ch)
    return chains


def _scan_masks():
    C = CHUNK
    si = lax.broadcasted_iota(jnp.int32, (2 * C, 2 * C), 0)
    sj = lax.broadcasted_iota(jnp.int32, (2 * C, 2 * C), 1)
    same = (si // C) == (sj // C)
    strict = jnp.logical_and(same, (si % C) > (sj % C))
    incl = jnp.logical_and(same, (si % C) >= (sj % C))
    return strict, incl, (si == sj).astype(F32)


def _double(prod, lp, steps):
    n = 2 * CHUNK
    for _ in range(steps):
        res = [_mm(jnp.concatenate([a, x], axis=0), x) for a, x in zip(prod, lp)]
        prod = [a + r[0:n] for a, r in zip(prod, res)]
        lp = [r[n:2 * n] for r in res]
    return prod, lp


def _scan_first_half(chains):
    C = CHUNK
    strict, incl, eye = _scan_masks()
    g = [_mm_nt(jnp.concatenate([ch["a"], ch["r"]], axis=0), jnp.concatenate([ch["b"], ch["k"]], axis=0))
         for ch in chains]
    ab = [jnp.where(strict, x[0:2 * C, 0:2 * C], 0.0) for x in g]
    ak = [jnp.where(strict, x[0:2 * C, 2 * C:4 * C], 0.0) for x in g]
    rbk = [jnp.concatenate([jnp.where(incl, x[2 * C:4 * C, 0:2 * C], 0.0),
                            jnp.where(incl, x[2 * C:4 * C, 2 * C:4 * C], 0.0)], axis=1) for x in g]
    akv = [_mm(x, ch["v"]) for x, ch in zip(ak, chains)]
    tinv, lp = _double([eye + x for x in ab], [_mm(x, x) for x in ab], SPLIT_STEPS)
    return tinv, lp, akv, rbk


def _scan_second_half(chains, tinv, lp, akv, rbk, zs):
    C = CHUNK
    eye = _scan_masks()[2]
    n = len(chains)
    levels = (C - 1).bit_length()
    tinv, lp = _double(tinv, lp, levels - 2 - SPLIT_STEPS)
    tinv = [a + _mm(a, x) for a, x in zip(tinv, lp)]
    au = [_mm(tinv[i], jnp.concatenate([chains[i]["a"], akv[i]], axis=1)) for i in range(n)]
    zeros = jnp.zeros((2 * C, LANES), F32)
    rhs = [jnp.concatenate([au[i], jnp.concatenate([zeros, chains[i]["v"]], axis=1)], axis=0)
           for i in range(n)]
    mn = [_mm_tn(jnp.concatenate([chains[i]["bh"], chains[i]["kh"]], axis=0), rhs[i])
          for i in range(n)]
    ry = [_mm(rbk[i], rhs[i]) for i in range(n)]
    yz = [_mm(jnp.concatenate([chains[i]["r"] + ry[i][:, 0:LANES],
                               mn[i][:, 0:LANES] + eye * chains[i]["w_end"]], axis=0), zs[i])
          for i in range(n)]
    z_news = [yz[i][2 * C:2 * C + LANES] + mn[i][:, LANES:2 * LANES] for i in range(n)]
    y_st = [yz[i][0:2 * C] + ry[i][:, LANES:2 * LANES] for i in range(n)]
    return [y[0:C] + y[C:2 * C] for y in y_st], z_news


def _store_scan(ys, z_news, y_ref, zt_ref, z_scr, nb):
    n_pairs = RW_W // LANES
    y_ref[...] = jnp.stack([jnp.concatenate(ys[bi * n_pairs:(bi + 1) * n_pairs], axis=1)
                            for bi in range(nb)], axis=0)
    z_scr[...] = jnp.stack(z_news, axis=0)
    zt_ref[...] = jnp.stack([z.T for z in z_news], axis=0)


def _scan_kernel(r_ref, lw_ref, k_ref, v_ref, kk_ref, b_ref, y_ref, zt_ref, z_scr, *, nb):
    @pl.when(pl.program_id(0) == 0)
    def _():
        z_scr[...] = jnp.zeros_like(z_scr)

    chains = _scan_chains([tuple(ref[bi] for ref in (r_ref, lw_ref, k_ref, v_ref, kk_ref, b_ref))
                           for bi in range(nb)])
    ys, z_news = _scan_second_half(chains, *_scan_first_half(chains),
                                   [z_scr[i] for i in range(len(chains))])
    _store_scan(ys, z_news, y_ref, zt_ref, z_scr, nb)


def _rwkv_scan(prep, batch, seq):
    assert seq % CHUNK == 0
    n_pairs = RW_W // LANES
    tok = pl.BlockSpec((batch, CHUNK, RW_W), lambda c: (0, c, 0))
    st = pl.BlockSpec((batch * n_pairs, LANES, LANES), lambda c: (0, 0, 0))
    y, zt = pl.pallas_call(
        functools.partial(_scan_kernel, nb=batch),
        grid=(seq // CHUNK,),
        in_specs=[tok] * 6,
        out_specs=[tok, st],
        out_shape=[jax.ShapeDtypeStruct((batch, seq, RW_W), F32),
                   jax.ShapeDtypeStruct((batch * n_pairs, LANES, LANES), F32)],
        scratch_shapes=[pltpu.VMEM((batch * n_pairs, LANES, LANES), F32)],
        compiler_params=_cparams(("arbitrary",)),
        name="rwkv_scan",
    )(*[x.reshape(batch, seq, RW_W) for x in prep])
    return y.reshape(batch * seq, RW_W), _unpair_state(zt, batch)


def _seq_scan_kernel(r_ref, lw_ref, k_ref, v_ref, kk_ref, b_ref, s_ref, y_ref, so_ref, w_scr, *, t_new):
    w_scr[...] = jnp.exp(lw_ref[...])

    def body(vi, carry):
        sv = s_ref[vi]
        for t in range(t_new):
            sa = jnp.sum(sv * kk_ref[t], axis=0, keepdims=True)
            sv = sv * w_scr[t] - sa * b_ref[t] + v_ref[t, pl.ds(vi, 1), :] * k_ref[t]
            y_ref[t, pl.ds(vi, 1), :] = jnp.sum(sv * r_ref[t], axis=0, keepdims=True)
        so_ref[vi] = sv
        return carry

    lax.fori_loop(0, HEAD_DIM, body, 0)


def _seq_scan(prep, state, dec_batch, t_new):
    def lanes(x):
        return jnp.transpose(x.reshape(dec_batch, t_new, RWKV_HEADS, HEAD_DIM), (1, 2, 3, 0))

    vec = pl.BlockSpec((t_new, None, HEAD_DIM, dec_batch), lambda h: (0, h, 0, 0))
    st = pl.BlockSpec((None, HEAD_DIM, HEAD_DIM, dec_batch), lambda h: (h, 0, 0, 0))
    s_native = jnp.transpose(state, (1, 2, 3, 0))
    y, s_new = pl.pallas_call(
        functools.partial(_seq_scan_kernel, t_new=t_new),
        grid=(RWKV_HEADS,),
        in_specs=[vec] * 6 + [st],
        out_specs=[vec, st],
        out_shape=[jax.ShapeDtypeStruct((t_new, RWKV_HEADS, HEAD_DIM, dec_batch), F32),
                   jax.ShapeDtypeStruct(s_native.shape, F32)],
        scratch_shapes=[pltpu.VMEM((t_new, HEAD_DIM, dec_batch), F32)],
        compiler_params=_cparams(("arbitrary",)),
        name="rwkv_seq_scan",
    )(*[lanes(x) for x in prep], s_native)
    y = jnp.transpose(y, (3, 0, 1, 2)).reshape(dec_batch * t_new, RW_W)
    return y, jnp.transpose(s_new, (3, 0, 1, 2))


def _final_kernel(oa_ref, y_ref, bonus_ref, gates_ref, x_ref, lnw_ref, lnb_ref, pa_ref, pb_ref, wo_ref, bd_ref,
                  o_ref):
    o_a = oa_ref[...]
    gates = gates_ref[...]
    z_a = gates[:, 0:ATT_W]
    z_b = gates[:, ATT_W:ATT_W + RW_W]
    g_a = gates[:, ATT_W + RW_W:ATT_W + RW_W + 1024]
    g_b = gates[:, ATT_W + RW_W + 1024:]
    y = y_ref[...]
    bd = bd_ref[...]
    cols_out = []
    for c in range(RW_W // LANES):
        cols = slice(c * LANES, (c + 1) * LANES)
        yc = y[:, cols]
        mu = _segsum(yc, bd) * (1.0 / HEAD_DIM)
        d = yc - mu
        var = _segsum(d * d, bd) * (1.0 / HEAD_DIM)
        cols_out.append(d * lax.rsqrt(var + GN_EPS))
    yn = jnp.concatenate(cols_out, axis=1)
    o_b = yn * lnw_ref[...] + lnb_ref[...] + bonus_ref[...]
    h_a = jnp.dot((o_a * jax.nn.silu(z_a)).astype(BF16), pa_ref[...], preferred_element_type=F32)
    h_b = jnp.dot((o_b * jax.nn.silu(z_b)).astype(BF16), pb_ref[...], preferred_element_type=F32)
    mixed = jax.nn.sigmoid(g_a) * h_a + jax.nn.sigmoid(g_b) * h_b
    o_ref[...] = x_ref[...] + jnp.dot(mixed.astype(BF16), wo_ref[...], preferred_element_type=F32)


def _final(o_a, y_raw, bonus, proj, x, ln_w, ln_b, p_a, p_b, w_out, bd, tm):
    n, d = x.shape
    tm = min(tm, n)
    gates_w = C_Q - C_GATES
    row = lambda w: pl.BlockSpec((tm, w), lambda i: (i, 0))
    full = lambda a: pl.BlockSpec(a.shape, lambda i: (0, 0))
    return pl.pallas_call(
        _final_kernel,
        grid=(n // tm,),
        in_specs=[row(ATT_W), row(RW_W), row(RW_W), row(gates_w), row(d),
                  full(ln_w), full(ln_b), full(p_a), full(p_b), full(w_out), full(bd)],
        out_specs=row(d),
        out_shape=jax.ShapeDtypeStruct((n, d), F32),
        compiler_params=_cparams(("arbitrary",)),
        name="gated_out",
    )(o_a, y_raw, bonus, proj, x, ln_w, ln_b, p_a, p_b, w_out, bd)


def _permute_w_in(w_in):
    q, k, v, sh, gates = (w_in[:, 0:QKV], w_in[:, QKV:2 * QKV], w_in[:, 2 * QKV:3 * QKV],
                          w_in[:, 3 * QKV:3 * QKV + SHIFT_COLS], w_in[:, 3 * QKV + SHIFT_COLS:])
    pad = jnp.zeros((w_in.shape[0], P_COLS - IN_COLS), w_in.dtype)
    return jnp.concatenate([gates, q, k, v, sh, pad], axis=1).astype(BF16)


def _layer(h_p, h_s, batch, seq, dec_batch, dec_seq, shift_s, state_s, caches, lw):
    (norm_g, w_perm, shift_mu, qg, kg, lora_w, w0, a0, k_k, k_a, r_k, ln_w, ln_b, p_a, p_b, w_out,
     bd) = lw
    n_s = dec_batch * dec_seq
    prep_args = (shift_mu, lora_w, w0, a0, k_k, k_a, r_k, bd)
    out_args = (ln_w, ln_b, p_a, p_b, w_out, bd)

    proj_p = _inproj(h_p, norm_g, w_perm, qg, kg, bd, tm=2048)
    prep_p = _rwkv_prep(proj_p, jnp.zeros((batch, SHIFT_COLS), F32), *prep_args, seq_len=seq, tm=512)
    o_a_p = _band_attention(proj_p, batch, seq, tile=2048)
    pv = proj_p.reshape(batch, seq, P_COLS)
    st_p = []
    for g, (window, _) in enumerate(ATT_GROUPS):
        keep = min(window, seq)
        for c0 in (C_K, C_V):
            st_p.append(pv[:, seq - keep:, c0 + g * ATT_W:c0 + (g + 1) * ATT_W]
                        .reshape(batch, keep, HPG, HEAD_DIM))

    proj_s = _inproj(h_s, norm_g, w_perm, qg, kg, bd, tm=min(1024, n_s))
    prep_s = _rwkv_prep(proj_s, shift_s, *prep_args, seq_len=dec_seq, tm=min(512, n_s))
    if seq % CHUNK == 0 and dec_batch == seq // CHUNK:
        o_a, new_k, new_v, y_p, state_p = _decode_attention(
            proj_s, caches, dec_batch, dec_seq, scan_prep=prep_p[:6], batch=batch, seq=seq)
    else:
        o_a, new_k, new_v = _decode_attention(proj_s, caches, dec_batch, dec_seq)
        y_p, state_p = _rwkv_scan(prep_p[:6], batch, seq)
    y_s, state_s_new = _seq_scan(prep_s[:6], state_s, dec_batch, dec_seq)

    out_p = _final(o_a_p, y_p, prep_p[6], proj_p, h_p, *out_args, tm=512)
    out_s = _final(o_a, y_s, prep_s[6], proj_s, h_s, *out_args, tm=512)
    st_p += [state_p, pv[:, seq - 1, C_SH:C_SH + SHIFT_COLS]]
    st_s = [c for kv in zip(new_k, new_v) for c in kv]
    st_s += [state_s_new, proj_s.reshape(dec_batch, dec_seq, P_COLS)[:, dec_seq - 1, C_SH:C_SH + SHIFT_COLS]]
    return out_p, out_s, st_p, st_s


def kernel(x_prompt, x_sample, cache_k_g0, cache_v_g0, cache_k_g1, cache_v_g1, cache_k_g2, cache_v_g2, state_rwkv, state_shift, norm_g, w_in, shift_mu, q_norm_g, k_norm_g, w0, w2, a0, a2, k_k, k_a, r_k, ln_x_w, ln_x_b, p_a, p_b, w_out):
    depth = norm_g.shape[0]
    batch, seq, d_model = x_prompt.shape
    dec_batch, dec_seq, _ = x_sample.shape
    bd = _bd_ones()
    h_p = x_prompt.reshape(batch * seq, d_model)
    h_s = x_sample.reshape(dec_batch * dec_seq, d_model)
    prompt_states, sample_states = [], []
    for layer in range(depth):
        zero = jnp.zeros((LORA, RW_W), F32)
        lora_w = jnp.concatenate([jnp.concatenate([w2[layer], zero], axis=1),
                                  jnp.concatenate([zero, a2[layer]], axis=1)], axis=0).astype(BF16)
        row = lambda a: a.reshape(1, -1)
        lw = (row(norm_g[layer]), _permute_w_in(w_in[layer]), row(shift_mu[layer]),
              jnp.tile(row(q_norm_g[layer]), (1, LANES // HEAD_DIM)),
              jnp.tile(row(k_norm_g[layer]), (1, LANES // HEAD_DIM)),
              lora_w, row(w0[layer]), row(a0[layer]), row(k_k[layer]), row(k_a[layer]),
              row(r_k[layer]), row(ln_x_w[layer]), row(ln_x_b[layer]),
              p_a[layer].astype(BF16), p_b[layer].astype(BF16), w_out[layer].astype(BF16), bd)
        caches = [(cache_k_g0[layer], cache_v_g0[layer]), (cache_k_g1[layer], cache_v_g1[layer]),
                  (cache_k_g2[layer], cache_v_g2[layer])]
        h_p, h_s, st_p, st_s = _layer(h_p, h_s, batch, seq, dec_batch, dec_seq, state_shift[layer],
                                      state_rwkv[layer], caches, lw)
        prompt_states.append(st_p)
        sample_states.append(st_s)
    ps = [jnp.stack(t) for t in zip(*prompt_states)]
    ss = [jnp.stack(t) for t in zip(*sample_states)]
    return (h_p.reshape(batch, seq, d_model), h_s.reshape(dec_batch, dec_seq, d_model),
            *ps, *ss)
```

```python
import functools

import jax
import jax.numpy as jnp
from jax import lax
from jax.experimental import pallas as pl
from jax.experimental.pallas import tpu as pltpu

F32 = jnp.float32
BF16 = jnp.bfloat16

HEAD_DIM = 64
HPG = 8
ATT_GROUPS = ((128, 1), (512, 4), (2048, 16))
N_ATT_HEADS = 24
RWKV_HEADS = 8
LORA = 64
QBLOCK = 128
RMS_EPS = 1e-6
GN_EPS = 64e-5
L2_EPS = 1e-12
CHUNK = 64

LANES = 128
NEW_PAD = 16
VMEM_LIMIT = 52 * 1024 * 1024

QKV = N_ATT_HEADS * HEAD_DIM
ATT_W = HPG * HEAD_DIM
RW_W = RWKV_HEADS * HEAD_DIM
C_GATES = 0
C_Q = 3072
C_K = C_Q + QKV
C_V = C_K + QKV
C_SH = C_V + QKV
SHIFT_COLS = 3 * RW_W + 2 * LORA
IN_COLS = C_SH + SHIFT_COLS
TN = 512
P_COLS = pl.cdiv(IN_COLS, TN) * TN


def _cparams(sem):
    return pltpu.CompilerParams(dimension_semantics=sem, vmem_limit_bytes=VMEM_LIMIT)


def _segsum(x, bd):
    hi = x.astype(BF16)
    lo = (x - hi.astype(F32)).astype(BF16)
    return (jnp.dot(hi, bd, preferred_element_type=F32)
            + jnp.dot(lo, bd, preferred_element_type=F32))


def _bd_ones():
    i = jnp.arange(LANES)
    return (i[:, None] // HEAD_DIM == i[None, :] // HEAD_DIM).astype(BF16)


def _alibi_slopes(g):
    return jnp.exp2(-8.0 * (jnp.arange(HPG, dtype=F32) + (g * HPG + 1)) / N_ATT_HEADS)


def _inproj_kernel(x_ref, g_ref, w_ref, qg_ref, kg_ref, bd_ref, o_ref, xn_ref):
    j = pl.program_id(1)

    @pl.when(j == 0)
    def _():
        x = x_ref[...]
        ms = jnp.mean(x * x, axis=-1, keepdims=True)
        xn_ref[...] = (x * lax.rsqrt(ms + RMS_EPS) * g_ref[...]).astype(BF16)

    t = jnp.dot(xn_ref[...], w_ref[...], preferred_element_type=F32)
    q_lo, k_lo, k_hi = C_Q // TN, C_K // TN, C_V // TN

    @pl.when(jnp.logical_or(j < q_lo, j >= k_hi))
    def _():
        o_ref[...] = t

    @pl.when(jnp.logical_and(j >= q_lo, j < k_hi))
    def _():
        gain = jnp.where(j < k_lo, qg_ref[...], kg_ref[...])
        bd = bd_ref[...]
        for c in range(TN // LANES):
            tc = t[:, c * LANES:(c + 1) * LANES]
            ms = _segsum(tc * tc, bd) * (1.0 / HEAD_DIM)
            o_ref[:, c * LANES:(c + 1) * LANES] = tc * lax.rsqrt(ms + RMS_EPS) * gain


def _inproj(x, norm_g, w_perm, qg, kg, bd, tm):
    n, d = x.shape
    return pl.pallas_call(
        _inproj_kernel,
        grid=(n // tm, P_COLS // TN),
        in_specs=[
            pl.BlockSpec((tm, d), lambda i, j: (i, 0)),
            pl.BlockSpec((1, d), lambda i, j: (0, 0)),
            pl.BlockSpec((d, TN), lambda i, j: (0, j)),
            pl.BlockSpec((1, LANES), lambda i, j: (0, 0)),
            pl.BlockSpec((1, LANES), lambda i, j: (0, 0)),
            pl.BlockSpec((LANES, LANES), lambda i, j: (0, 0)),
        ],
        out_specs=pl.BlockSpec((tm, TN), lambda i, j: (i, j)),
        out_shape=jax.ShapeDtypeStruct((n, P_COLS), F32),
        scratch_shapes=[pltpu.VMEM((tm, d), BF16)],
        compiler_params=_cparams(("arbitrary", "arbitrary")),
        name="inproj",
    )(x, norm_g, w_perm, qg, kg, bd)


def _band_kernel(sl_ref, *refs, tile):
    n_g = len(ATT_GROUPS)
    ins, o_ref, scr = refs[0:5 * n_g], refs[5 * n_g], refs[5 * n_g + 1:]
    n = pl.program_id(1)
    p = pl.program_id(2)
    lane = lax.broadcasted_iota(jnp.int32, (1, LANES), 1)
    mask_a = lane < HEAD_DIM
    qi = lax.broadcasted_iota(jnp.int32, (QBLOCK, 2 * QBLOCK), 0)
    ki = lax.broadcasted_iota(jnp.int32, (QBLOCK, 2 * QBLOCK), 1)
    rel = qi + QBLOCK - ki
    band = jnp.logical_and(rel >= 0, rel <= QBLOCK)
    band_first = jnp.logical_and(band, jnp.logical_or(ki >= QBLOCK, n > 0))

    for g, (_, dil) in enumerate(ATT_GROUPS):
        q_ref, kp_ref, kc_ref, vp_ref, vc_ref = ins[5 * g:5 * g + 5]
        acc_scr, m_scr, l_scr = scr[3 * g:3 * g + 3]
        dist = (rel * dil).astype(F32)
        alibi = [-(sl_ref[g * HPG + 2 * p + h] * dist) for h in range(2)]
        bias = [jnp.where(band, a, -jnp.inf) for a in alibi]
        bias_first = [jnp.where(band_first, a, -jnp.inf) for a in alibi]

        def rows(r, s, dil=dil):
            if dil == 1:
                return pl.ds(s * QBLOCK, QBLOCK)
            return pl.ds(r + s * QBLOCK * dil, QBLOCK, stride=dil)

        for r in range(dil):
            for s in range(tile // (dil * QBLOCK)):
                blk_bias = bias_first if s == 0 else bias
                q = q_ref[rows(r, s), :] * (1.0 / 8.0)
                if s == 0:
                    k_prev, v_prev = kp_ref[rows(r, 0), :], vp_ref[rows(r, 0), :]
                else:
                    k_prev, v_prev = kc_ref[rows(r, s - 1), :], vc_ref[rows(r, s - 1), :]
                k = jnp.concatenate([k_prev, kc_ref[rows(r, s), :]], axis=0).astype(BF16)
                v = jnp.concatenate([v_prev, vc_ref[rows(r, s), :]], axis=0).astype(BF16)
                outs = []
                for h in range(2):
                    hm = mask_a if h == 0 else jnp.logical_not(mask_a)
                    qh = jnp.where(hm, q, 0.0).astype(BF16)
                    sc = lax.dot_general(qh, k, (((1,), (1,)), ((), ())), preferred_element_type=F32)
                    sc = sc + blk_bias[h]
                    m = jnp.max(sc, axis=-1, keepdims=True)
                    pr = jnp.exp(sc - m)
                    l = jnp.sum(pr, axis=-1, keepdims=True)
                    acc = jnp.dot(pr.astype(BF16), v, preferred_element_type=F32)
                    outs.append((acc, m, l))
                (acc0, m0, l0), (acc1, m1, l1) = outs
                acc_scr[rows(r, s), :] = jnp.where(mask_a, acc0, acc1)
                m_scr[rows(r, s), :] = jnp.where(mask_a, m0, m1)
                l_scr[rows(r, s), :] = jnp.where(mask_a, l0, l1)

    parts = [(scr[3 * g][...], scr[3 * g + 1][...], scr[3 * g + 2][...]) for g in range(n_g)]
    m_all = functools.reduce(jnp.maximum, [m for _, m, _ in parts])
    num = sum(jnp.exp(m - m_all) * a for a, m, _ in parts)
    den = sum(jnp.exp(m - m_all) * l for _, m, l in parts)
    o_ref[...] = num / den


def _band_attention(proj, batch, seq, tile):
    n_g = len(ATT_GROUPS)
    tile = min(tile, seq)
    assert seq % tile == 0
    nt = seq // tile
    in_specs = [pl.BlockSpec(memory_space=pltpu.SMEM)]
    for g, (window, dil) in enumerate(ATT_GROUPS):
        assert window // dil == QBLOCK
        halo = QBLOCK * dil
        assert tile % halo == 0
        nh = seq // halo
        cq, ck, cv = ((c + g * ATT_W) // LANES for c in (C_Q, C_K, C_V))

        def cur(col):
            return pl.BlockSpec((tile, LANES), lambda b, n, p: (b * nt + n, col + p))

        def prev(col, halo=halo, nh=nh):
            return pl.BlockSpec(
                (halo, LANES),
                lambda b, n, p: (b * nh + jnp.maximum(n * (tile // halo) - 1, 0), col + p))

        in_specs += [cur(cq), prev(ck), cur(ck), prev(cv), cur(cv)]
    slopes = jnp.concatenate([_alibi_slopes(g) for g in range(n_g)])
    return pl.pallas_call(
        functools.partial(_band_kernel, tile=tile),
        grid=(batch, nt, ATT_W // LANES),
        in_specs=in_specs,
        out_specs=pl.BlockSpec((tile, LANES), lambda b, n, p: (b * nt + n, p)),
        out_shape=jax.ShapeDtypeStruct((batch * seq, ATT_W), F32),
        scratch_shapes=[pltpu.VMEM((tile, LANES), F32)] * (3 * n_g),
        compiler_params=_cparams(("arbitrary", "arbitrary", "arbitrary")),
        name="band_attn",
    )(slopes, *([proj] * (5 * n_g)))


def _decode_body(sl_ref, q_ref, kn_ref, vn_ref, kb, vb, o_ref, ko, vo, hh, t_new, hps):
    n_g = len(ATT_GROUPS)
    ti_n = lax.broadcasted_iota(jnp.int32, (t_new, NEW_PAD), 0)
    tj_n = lax.broadcasted_iota(jnp.int32, (t_new, NEW_PAD), 1)
    pi = lax.broadcasted_iota(jnp.int32, (NEW_PAD, LANES), 0)
    pj = lax.broadcasted_iota(jnp.int32, (NEW_PAD, LANES), 1)
    place = jnp.logical_and(pj == pi + (LANES - t_new), pi < t_new).astype(BF16)
    tail = lax.broadcasted_iota(jnp.int32, (HEAD_DIM, LANES), 1) >= LANES - t_new
    nt = (((1,), (1,)), ((), ()))
    tn = (((0,), (0,)), ((), ()))
    zpad = jnp.zeros((NEW_PAD - t_new, HEAD_DIM), F32)

    combos = [(j, g) for j in range(hps) for g in range(n_g)]
    d_new = ti_n - tj_n
    kn = {c: jnp.concatenate([kn_ref[c[1], c[0]], zpad], axis=0) for c in combos}
    vn = {c: jnp.concatenate([vn_ref[c[1], c[0]], zpad], axis=0) for c in combos}
    q = {c: q_ref[c[1], c[0]].astype(BF16) for c in combos}
    s_buf = {(j, g): jnp.dot(q[j, g], kb[g][j].astype(BF16), preferred_element_type=F32)
             for j, g in combos}
    s_new = {c: lax.dot_general(q[c], kn[c].astype(BF16), nt, preferred_element_type=F32)
             for c in combos}
    p_buf, p_new, stats = {}, {}, {}
    for j, g in combos:
        wb, dil = ATT_GROUPS[g]
        slope = sl_ref[g * HPG + hh * hps + j]
        wi = lax.broadcasted_iota(jnp.int32, (t_new, wb), 1)
        ti = lax.broadcasted_iota(jnp.int32, (t_new, wb), 0)
        d_buf = wb + ti - wi
        ok_buf = wi >= ti
        ok_new = d_new >= 0
        if dil > 1:
            ok_buf = jnp.logical_and(ok_buf, (d_buf & (dil - 1)) == 0)
            ok_new = jnp.logical_and(ok_new, (d_new & (dil - 1)) == 0)
        sb = jnp.where(ok_buf, s_buf[j, g] * (1.0 / 8.0) - slope * d_buf.astype(F32), -jnp.inf)
        sn = jnp.where(ok_new, s_new[j, g] * (1.0 / 8.0) - slope * d_new.astype(F32), -jnp.inf)
        m = jnp.maximum(jnp.max(sb, axis=-1, keepdims=True), jnp.max(sn, axis=-1, keepdims=True))
        pb, pn = jnp.exp(sb - m), jnp.exp(sn - m)
        p_buf[j, g], p_new[j, g] = pb.astype(BF16), pn.astype(BF16)
        stats[j, g] = (m, jnp.sum(pb, axis=-1, keepdims=True) + jnp.sum(pn, axis=-1, keepdims=True))
    acc = {(j, g): lax.dot_general(p_buf[j, g], vb[g][j].astype(BF16), nt, preferred_element_type=F32)
           + jnp.dot(p_new[j, g], vn[j, g].astype(BF16), preferred_element_type=F32)
           for j, g in combos}
    for j in range(hps):
        m_all = functools.reduce(jnp.maximum, [stats[j, g][0] for g in range(n_g)])
        wts = [jnp.exp(stats[j, g][0] - m_all) for g in range(n_g)]
        num = sum(wts[g] * acc[j, g] for g in range(n_g))
        den = sum(wts[g] * stats[j, g][1] for g in range(n_g))
        o_ref[j] = num / den

    for j, g in combos:
        wb = ATT_GROUPS[g][0]
        for src, new, dst in ((kb[g], kn[j, g], ko[g]), (vb[g], vn[j, g], vo[g])):
            rolled = pltpu.roll(src[j], wb - t_new, 1)
            hi = new.astype(BF16)
            lo = (new - hi.astype(F32)).astype(BF16)
            placed = (lax.dot_general(hi, place, tn, preferred_element_type=F32)
                      + lax.dot_general(lo, place, tn, preferred_element_type=F32))
            if wb > LANES:
                dst[j, :, 0:wb - LANES] = rolled[:, 0:wb - LANES]
            dst[j, :, wb - LANES:wb] = jnp.where(tail, placed, rolled[:, wb - LANES:wb])


def _decode_kernel(sl_ref, q_ref, kn_ref, vn_ref, *refs, t_new, hps):
    n_c = 2 * len(ATT_GROUPS)
    _decode_body(sl_ref, q_ref, kn_ref, vn_ref, refs[0:n_c:2], refs[1:n_c:2], refs[n_c],
                 refs[n_c + 1::2], refs[n_c + 2::2], pl.program_id(1), t_new, hps)


def _decode_scan_kernel(sl_ref, q_ref, kn_ref, vn_ref, *refs, t_new, hps, nb):
    n_c = 2 * len(ATT_GROUPS)
    caches_in = refs[0:n_c]
    cur_ref, wa_ref, pcur_ref, pwa_ref = refs[n_c:n_c + 4]
    param_refs = refs[n_c + 4:n_c + 13]
    o_ref = refs[n_c + 13]
    caches_out = refs[n_c + 14:2 * n_c + 14]
    y_ref, zt_ref, bonus_ref, z_scr, tinv_scr, lp_scr, akv_scr, rbk_scr, tok_scr = refs[2 * n_c + 14:]
    c, half = pl.program_id(0), pl.program_id(1)
    n = nb * (RW_W // LANES)

    def decode(hh):
        _decode_body(sl_ref, q_ref, kn_ref, vn_ref, caches_in[0::2], caches_in[1::2], o_ref,
                     caches_out[0::2], caches_out[1::2], hh, t_new, hps)

    @pl.when(jnp.logical_and(c == 0, half == 0))
    def _():
        z_scr[...] = jnp.zeros_like(z_scr)

    @pl.when(half == 0)
    def _():
        decode(0)
        params = tuple(ref[...] for ref in param_refs)
        row0 = lax.broadcasted_iota(jnp.int32, (CHUNK, 1), 0) == 0
        seqs = []
        for bi in range(nb):
            def prev(x, p8):
                edge = jnp.where(c == 0, 0.0, p8[7:8, :])
                return jnp.where(row0, edge, pltpu.roll(x, 1, 0))
            cur, wa = cur_ref[bi], wa_ref[bi]
            outs = _prep_math(cur, wa, prev(cur, pcur_ref[bi]), prev(wa, pwa_ref[bi]), *params)
            seqs.append(outs[:6])
            bonus_ref[bi] = outs[6]
            for j in range(6):
                tok_scr[j, bi] = outs[j]
        tinv, lp, akv, rbk = _scan_first_half(_scan_chains(seqs))
        tinv_scr[...] = jnp.stack(tinv, axis=0)
        lp_scr[...] = jnp.stack(lp, axis=0)
        akv_scr[...] = jnp.stack(akv, axis=0)
        rbk_scr[...] = jnp.stack(rbk, axis=0)

    @pl.when(half == 1)
    def _():
        decode(1)
        seqs = [tuple(tok_scr[j, bi] for j in range(6)) for bi in range(nb)]
        carried = [[scr[i] for i in range(n)] for scr in (tinv_scr, lp_scr, akv_scr, rbk_scr)]
        ys, z_news = _scan_second_half(_scan_chains(seqs), *carried, [z_scr[i] for i in range(n)])
        _store_scan(ys, z_news, y_ref, zt_ref, z_scr, nb)


def _unpair_state(zt, batch):
    zt = zt.reshape(batch, RW_W // LANES, LANES, LANES)
    s_a = zt[:, :, 0:HEAD_DIM, 0:HEAD_DIM]
    s_b = zt[:, :, HEAD_DIM:, HEAD_DIM:]
    return jnp.stack([s_a, s_b], axis=2).reshape(batch, RWKV_HEADS, HEAD_DIM, HEAD_DIM)


def _decode_attention(proj, caches, dec_batch, t_new, scan_src=None, batch=0, seq=0):
    n_g = len(ATT_GROUPS)
    hps = HPG // 2
    assert all(k.shape[1] == w for (k, _), (w, _) in zip(caches, ATT_GROUPS))

    def heads(col0):
        x = proj[:, col0:col0 + QKV].reshape(dec_batch, t_new, n_g, HPG, HEAD_DIM)
        return jnp.transpose(x, (0, 2, 3, 1, 4))

    q, kn, vn = heads(C_Q), heads(C_K), heads(C_V)
    native = [jnp.transpose(c, (0, 2, 3, 1)) for kv in caches for c in kv]
    slopes = jnp.concatenate([_alibi_slopes(g) for g in range(n_g)])

    new_q = pl.BlockSpec((None, n_g, hps, t_new, HEAD_DIM), lambda b, h: (b, 0, h, 0, 0))
    cache_specs = [pl.BlockSpec((None, hps, HEAD_DIM, c.shape[-1]), lambda b, h: (b, h, 0, 0))
                   for c in native]
    in_specs = [pl.BlockSpec(memory_space=pltpu.SMEM), new_q, new_q, new_q] + cache_specs
    out_specs = [pl.BlockSpec((None, hps, t_new, HEAD_DIM), lambda b, h: (b, h, 0, 0))] + cache_specs
    out_shape = ([jax.ShapeDtypeStruct((dec_batch, HPG, t_new, HEAD_DIM), F32)]
                 + [jax.ShapeDtypeStruct(c.shape, F32) for c in native])
    grid = (dec_batch, HPG // hps)
    if scan_src is None:
        outs = pl.pallas_call(
            functools.partial(_decode_kernel, t_new=t_new, hps=hps),
            grid=grid, in_specs=in_specs, out_specs=out_specs, out_shape=out_shape,
            compiler_params=_cparams(("arbitrary", "arbitrary")),
            name="decode_attn",
        )(slopes, q, kn, vn, *native)
        scan_out = ()
    else:
        assert grid == (seq // CHUNK, 2)
        proj_p, (mu, *prep_params) = scan_src
        pv = proj_p.reshape(batch, seq, P_COLS)
        n = batch * (RW_W // LANES)
        main_w = 3 * RW_W
        cb, wb = C_SH // main_w, (C_SH + main_w) // LANES

        def halo(c, h):
            return jnp.maximum(c * (CHUNK // 8) - 1, 0)

        tok = pl.BlockSpec((batch, CHUNK, RW_W), lambda c, h: (0, c, 0))
        st = pl.BlockSpec((n, LANES, LANES), lambda c, h: (0, 0, 0))
        full = lambda a: pl.BlockSpec(a.shape, lambda c, h: (0, 0))
        params = (mu[:, 0:main_w], mu[:, main_w:], *prep_params)
        prep_specs = [pl.BlockSpec((batch, CHUNK, main_w), lambda c, h: (0, c, cb)),
                      pl.BlockSpec((batch, CHUNK, LANES), lambda c, h: (0, c, wb)),
                      pl.BlockSpec((batch, 8, main_w), lambda c, h: (0, halo(c, h), cb)),
                      pl.BlockSpec((batch, 8, LANES), lambda c, h: (0, halo(c, h), wb))]
        sq = pltpu.VMEM((n, LANES, LANES), F32)
        tok_sds = jax.ShapeDtypeStruct((batch, seq, RW_W), F32)
        outs = pl.pallas_call(
            functools.partial(_decode_scan_kernel, t_new=t_new, hps=hps, nb=batch),
            grid=grid,
            in_specs=in_specs + prep_specs + [full(a) for a in params],
            out_specs=out_specs + [tok, st, tok],
            out_shape=out_shape + [tok_sds, jax.ShapeDtypeStruct((n, LANES, LANES), F32), tok_sds],
            scratch_shapes=[sq, sq, sq, sq, pltpu.VMEM((n, LANES, 2 * LANES), F32),
                            pltpu.VMEM((6, batch, CHUNK, RW_W), F32)],
            compiler_params=_cparams(("arbitrary", "arbitrary")),
            name="decode_attn_rwkv_scan",
        )(slopes, q, kn, vn, *native, pv, pv, pv, pv, *params)
        flat = (batch * seq, RW_W)
        scan_out = (outs[-3].reshape(flat), _unpair_state(outs[-2], batch), outs[-1].reshape(flat))
        outs = outs[:-3]
    o_a = jnp.transpose(outs[0], (0, 2, 1, 3)).reshape(dec_batch * t_new, ATT_W)
    new = [jnp.transpose(c, (0, 3, 1, 2)) for c in outs[1:]]
    return (o_a, new[0::2], new[1::2]) + scan_out


def _prep_kernel(cur_ref, wa_ref, pcur_ref, pwa_ref, fcur_ref, fwa_ref, mu_ref, muwa_ref, lora_ref,
                 w0_ref, a0_ref, kk_ref, ka_ref, rk_ref, bd_ref,
                 r_o, lw_o, k_o, v_o, kkn_o, b_o, bonus_o, *, seq_len, tm):
    i = pl.program_id(0)
    row = lax.broadcasted_iota(jnp.int32, (tm, 1), 0)

    def shifted(cur, p8, first):
        rolled = pltpu.roll(cur, 1, 0)
        if seq_len >= tm:
            at_start = (i * tm) % seq_len == 0
            edge = jnp.where(at_start, first, p8[7:8, :])
            return jnp.where(row == 0, edge, rolled)
        return jnp.where(row % seq_len == 0, first, rolled)

    cur = cur_ref[...]
    wa = wa_ref[...]
    params = tuple(ref[...] for ref in (mu_ref, muwa_ref, lora_ref, w0_ref, a0_ref, kk_ref, ka_ref, rk_ref,
                                        bd_ref))
    outs = _prep_math(cur, wa, shifted(cur, pcur_ref[...], fcur_ref[...]),
                      shifted(wa, pwa_ref[...], fwa_ref[...]), *params)
    for ref, val in zip((r_o, lw_o, k_o, v_o, kkn_o, b_o, bonus_o), outs):
        ref[...] = val


def _prep_math(cur, wa, prev_cur, prev_wa, mu, muwa, lora, w0, a0, k_k, k_a, r_k, bd):
    xs = cur + (prev_cur - cur) * mu
    xwa = wa + (prev_wa - wa) * muwa
    r = xs[:, 0:RW_W]
    kr = xs[:, RW_W:2 * RW_W]
    vr = xs[:, 2 * RW_W:3 * RW_W]
    lane = lax.broadcasted_iota(jnp.int32, (1, LANES), 1)
    lin = jnp.where(lane < LORA, jnp.tanh(xwa), xwa).astype(BF16)
    lo = jnp.dot(lin, lora, preferred_element_type=F32)
    z = -(w0 + lo[:, 0:RW_W])
    softplus = jnp.maximum(z, 0.0) + jnp.log1p(jnp.exp(-jnp.abs(z)))
    lw = -jnp.exp(-softplus - 0.5)
    a = jax.nn.sigmoid(a0 + lo[:, RW_W:2 * RW_W])
    kk = kr * k_k
    k_mod = kr * (1.0 + (a - 1.0) * k_a)
    rkk = r * k_mod * r_k
    kkn, b, bonus = [], [], []
    for c in range(RW_W // LANES):
        cols = slice(c * LANES, (c + 1) * LANES)
        kc = kk[:, cols]
        kn = kc / jnp.maximum(jnp.sqrt(_segsum(kc * kc, bd)), L2_EPS)
        kkn.append(kn)
        b.append(kn * a[:, cols])
        bonus.append(_segsum(rkk[:, cols], bd) * vr[:, cols])
    cat = lambda parts: jnp.concatenate(parts, axis=1)
    return r, lw, k_mod, vr, cat(kkn), cat(b), cat(bonus)


def _rwkv_prep(proj, first, mu, lora_w, w0, a0, k_k, k_a, r_k, bd, seq_len, tm):
    n = proj.shape[0]
    main_w = 3 * RW_W
    cb, wb = C_SH // main_w, (C_SH + main_w) // LANES
    per_seq_first = seq_len >= tm
    if per_seq_first:
        f_main = first[:, None, 0:main_w]
        f_wa = first[:, None, main_w:]
        fm_spec = pl.BlockSpec((None, 1, main_w), lambda i: ((i * tm) // seq_len, 0, 0))
        fw_spec = pl.BlockSpec((None, 1, LANES), lambda i: ((i * tm) // seq_len, 0, 0))
    else:
        rep = jnp.repeat(first, seq_len, axis=0)
        f_main, f_wa = rep[:, 0:main_w], rep[:, main_w:]
        fm_spec = pl.BlockSpec((tm, main_w), lambda i: (i, 0))
        fw_spec = pl.BlockSpec((tm, LANES), lambda i: (i, 0))
    vec = lambda w: pl.BlockSpec((1, w), lambda i: (0, 0))
    out_spec = pl.BlockSpec((tm, RW_W), lambda i: (i, 0))
    out_sds = jax.ShapeDtypeStruct((n, RW_W), F32)
    return pl.pallas_call(
        functools.partial(_prep_kernel, seq_len=seq_len, tm=tm),
        grid=(n // tm,),
        in_specs=[
            pl.BlockSpec((tm, main_w), lambda i: (i, cb)),
            pl.BlockSpec((tm, LANES), lambda i: (i, wb)),
            pl.BlockSpec((8, main_w), lambda i: (jnp.maximum(i * (tm // 8) - 1, 0), cb)),
            pl.BlockSpec((8, LANES), lambda i: (jnp.maximum(i * (tm // 8) - 1, 0), wb)),
            fm_spec, fw_spec,
            vec(main_w), vec(LANES),
            pl.BlockSpec((LANES, 2 * RW_W), lambda i: (0, 0)),
            vec(RW_W), vec(RW_W), vec(RW_W), vec(RW_W), vec(RW_W),
            pl.BlockSpec((LANES, LANES), lambda i: (0, 0)),
        ],
        out_specs=[out_spec] * 7,
        out_shape=[out_sds] * 7,
        compiler_params=_cparams(("arbitrary",)),
        name="rwkv_prep",
    )(proj, proj, proj, proj, f_main, f_wa, mu[:, 0:main_w], mu[:, main_w:], lora_w,
      w0, a0, k_k, k_a, r_k, bd)


def _mm(a, b):
    return jnp.dot(a.astype(BF16), b.astype(BF16), preferred_element_type=F32)


def _mm_nt(a, b):
    return lax.dot_general(a.astype(BF16), b.astype(BF16), (((1,), (1,)), ((), ())),
                           preferred_element_type=F32)


def _mm_tn(a, b):
    return lax.dot_general(a.astype(BF16), b.astype(BF16), (((0,), (0,)), ((), ())),
                           preferred_element_type=F32)


SPLIT_STEPS = 2


def _scan_chains(seqs):
    C = CHUNK
    ti = lax.broadcasted_iota(jnp.int32, (C, C), 0)
    tj = lax.broadcasted_iota(jnp.int32, (C, C), 1)
    tri = (ti >= tj).astype(F32)
    in_a = lax.broadcasted_iota(jnp.int32, (1, LANES), 1) < HEAD_DIM

    def stack(x):
        return jnp.concatenate([jnp.where(in_a, x, 0.0), jnp.where(in_a, 0.0, x)], axis=0)

    chains = []
    for r, lw, k, v, kk, b in seqs:
        cum = jnp.dot(tri, lw, precision=lax.Precision.HIGHEST, preferred_element_type=F32)
        cum_end = cum[C - 1:C, :]
        w_in = jnp.exp(-cum)
        w_rest = jnp.exp(cum_end - cum)
        vals = (("a", jnp.exp(cum - lw) * (-kk)), ("b", b * w_in), ("k", k * w_in), ("r", r * jnp.exp(cum)),
                ("v", v), ("bh", b * w_rest), ("kh", k * w_rest))
        w_end = jnp.exp(cum_end)
        for p in range(RW_W // LANES):
            cols = slice(p * LANES, (p + 1) * LANES)
            ch = {name: stack(x[:, cols]) for name, x in vals}
            ch["w_end"] = w_end[:, cols]
            chains.append(ch)
    return chains


def _scan_masks():
    C = CHUNK
    si = lax.broadcasted_iota(jnp.int32, (2 * C, 2 * C), 0)
    sj = lax.broadcasted_iota(jnp.int32, (2 * C, 2 * C), 1)
    same = (si // C) == (sj // C)
    strict = jnp.logical_and(same, (si % C) > (sj % C))
    incl = jnp.logical_and(same, (si % C) >= (sj % C))
    return strict, incl, (si == sj).astype(F32)


def _double(prod, lp, steps):
    n = 2 * CHUNK
    for _ in range(steps):
        res = [_mm(jnp.concatenate([a, x], axis=0), x) for a, x in zip(prod, lp)]
        prod = [a + r[0:n] for a, r in zip(prod, res)]
        lp = [r[n:2 * n] for r in res]
    return prod, lp


def _scan_first_half(chains):
    C = CHUNK
    strict, incl, eye = _scan_masks()
    g = [_mm_nt(jnp.concatenate([ch["a"], ch["r"]], axis=0), jnp.concatenate([ch["b"], ch["k"]], axis=0))
         for ch in chains]
    ab = [jnp.where(strict, x[0:2 * C, 0:2 * C], 0.0) for x in g]
    ak = [jnp.where(strict, x[0:2 * C, 2 * C:4 * C], 0.0) for x in g]
    rbk = [jnp.concatenate([jnp.where(incl, x[2 * C:4 * C, 0:2 * C], 0.0),
                            jnp.where(incl, x[2 * C:4 * C, 2 * C:4 * C], 0.0)], axis=1) for x in g]
    akv = [_mm(x, ch["v"]) for x, ch in zip(ak, chains)]
    tinv, lp = _double([eye + x for x in ab], [_mm(x, x) for x in ab], SPLIT_STEPS)
    return tinv, lp, akv, rbk


def _scan_second_half(chains, tinv, lp, akv, rbk, zs):
    C = CHUNK
    eye = _scan_masks()[2]
    n = len(chains)
    levels = (C - 1).bit_length()
    tinv, lp = _double(tinv, lp, levels - 2 - SPLIT_STEPS)
    tinv = [a + _mm(a, x) for a, x in zip(tinv, lp)]
    au = [_mm(tinv[i], jnp.concatenate([chains[i]["a"], akv[i]], axis=1)) for i in range(n)]
    zeros = jnp.zeros((2 * C, LANES), F32)
    rhs = [jnp.concatenate([au[i], jnp.concatenate([zeros, chains[i]["v"]], axis=1)], axis=0)
           for i in range(n)]
    mn = [_mm_tn(jnp.concatenate([chains[i]["bh"], chains[i]["kh"]], axis=0), rhs[i])
          for i in range(n)]
    ry = [_mm(rbk[i], rhs[i]) for i in range(n)]
    yz = [_mm(jnp.concatenate([chains[i]["r"] + ry[i][:, 0:LANES],
                               mn[i][:, 0:LANES] + eye * chains[i]["w_end"]], axis=0), zs[i])
          for i in range(n)]
    z_news = [yz[i][2 * C:2 * C + LANES] + mn[i][:, LANES:2 * LANES] for i in range(n)]
    y_st = [yz[i][0:2 * C] + ry[i][:, LANES:2 * LANES] for i in range(n)]
    return [y[0:C] + y[C:2 * C] for y in y_st], z_news


def _store_scan(ys, z_news, y_ref, zt_ref, z_scr, nb):
    n_pairs = RW_W // LANES
    y_ref[...] = jnp.stack([jnp.concatenate(ys[bi * n_pairs:(bi + 1) * n_pairs], axis=1)
                            for bi in range(nb)], axis=0)
    z_scr[...] = jnp.stack(z_news, axis=0)
    zt_ref[...] = jnp.stack([z.T for z in z_news], axis=0)


def _scan_kernel(r_ref, lw_ref, k_ref, v_ref, kk_ref, b_ref, y_ref, zt_ref, z_scr, *, nb):
    @pl.when(pl.program_id(0) == 0)
    def _():
        z_scr[...] = jnp.zeros_like(z_scr)

    chains = _scan_chains([tuple(ref[bi] for ref in (r_ref, lw_ref, k_ref, v_ref, kk_ref, b_ref))
                           for bi in range(nb)])
    ys, z_news = _scan_second_half(chains, *_scan_first_half(chains),
                                   [z_scr[i] for i in range(len(chains))])
    _store_scan(ys, z_news, y_ref, zt_ref, z_scr, nb)


def _rwkv_scan(prep, batch, seq):
    assert seq % CHUNK == 0
    n_pairs = RW_W // LANES
    tok = pl.BlockSpec((batch, CHUNK, RW_W), lambda c: (0, c, 0))
    st = pl.BlockSpec((batch * n_pairs, LANES, LANES), lambda c: (0, 0, 0))
    y, zt = pl.pallas_call(
        functools.partial(_scan_kernel, nb=batch),
        grid=(seq // CHUNK,),
        in_specs=[tok] * 6,
        out_specs=[tok, st],
        out_shape=[jax.ShapeDtypeStruct((batch, seq, RW_W), F32),
                   jax.ShapeDtypeStruct((batch * n_pairs, LANES, LANES), F32)],
        scratch_shapes=[pltpu.VMEM((batch * n_pairs, LANES, LANES), F32)],
        compiler_params=_cparams(("arbitrary",)),
        name="rwkv_scan",
    )(*[x.reshape(batch, seq, RW_W) for x in prep])
    return y.reshape(batch * seq, RW_W), _unpair_state(zt, batch)


def _seq_scan_kernel(r_ref, lw_ref, k_ref, v_ref, kk_ref, b_ref, s_ref, y_ref, so_ref, w_scr, *, t_new):
    w_scr[...] = jnp.exp(lw_ref[...])

    def body(vi, carry):
        sv = s_ref[vi]
        for t in range(t_new):
            sa = jnp.sum(sv * kk_ref[t], axis=0, keepdims=True)
            sv = sv * w_scr[t] - sa * b_ref[t] + v_ref[t, pl.ds(vi, 1), :] * k_ref[t]
            y_ref[t, pl.ds(vi, 1), :] = jnp.sum(sv * r_ref[t], axis=0, keepdims=True)
        so_ref[vi] = sv
        return carry

    lax.fori_loop(0, HEAD_DIM, body, 0)


def _seq_scan(prep, state, dec_batch, t_new):
    def lanes(x):
        return jnp.transpose(x.reshape(dec_batch, t_new, RWKV_HEADS, HEAD_DIM), (1, 2, 3, 0))

    vec = pl.BlockSpec((t_new, None, HEAD_DIM, dec_batch), lambda h: (0, h, 0, 0))
    st = pl.BlockSpec((None, HEAD_DIM, HEAD_DIM, dec_batch), lambda h: (h, 0, 0, 0))
    s_native = jnp.transpose(state, (1, 2, 3, 0))
    y, s_new = pl.pallas_call(
        functools.partial(_seq_scan_kernel, t_new=t_new),
        grid=(RWKV_HEADS,),
        in_specs=[vec] * 6 + [st],
        out_specs=[vec, st],
        out_shape=[jax.ShapeDtypeStruct((t_new, RWKV_HEADS, HEAD_DIM, dec_batch), F32),
                   jax.ShapeDtypeStruct(s_native.shape, F32)],
        scratch_shapes=[pltpu.VMEM((t_new, HEAD_DIM, dec_batch), F32)],
        compiler_params=_cparams(("arbitrary",)),
        name="rwkv_seq_scan",
    )(*[lanes(x) for x in prep], s_native)
    y = jnp.transpose(y, (3, 0, 1, 2)).reshape(dec_batch * t_new, RW_W)
    return y, jnp.transpose(s_new, (3, 0, 1, 2))


def _final_kernel(oa_ref, y_ref, bonus_ref, gates_ref, x_ref, lnw_ref, lnb_ref, pa_ref, pb_ref, wo_ref, bd_ref,
                  o_ref):
    o_a = oa_ref[...]
    gates = gates_ref[...]
    z_a = gates[:, 0:ATT_W]
    z_b = gates[:, ATT_W:ATT_W + RW_W]
    g_a = gates[:, ATT_W + RW_W:ATT_W + RW_W + 1024]
    g_b = gates[:, ATT_W + RW_W + 1024:]
    y = y_ref[...]
    bd = bd_ref[...]
    cols_out = []
    for c in range(RW_W // LANES):
        cols = slice(c * LANES, (c + 1) * LANES)
        yc = y[:, cols]
        mu = _segsum(yc, bd) * (1.0 / HEAD_DIM)
        d = yc - mu
        var = _segsum(d * d, bd) * (1.0 / HEAD_DIM)
        cols_out.append(d * lax.rsqrt(var + GN_EPS))
    yn = jnp.concatenate(cols_out, axis=1)
    o_b = yn * lnw_ref[...] + lnb_ref[...] + bonus_ref[...]
    h_a = jnp.dot((o_a * jax.nn.silu(z_a)).astype(BF16), pa_ref[...], preferred_element_type=F32)
    h_b = jnp.dot((o_b * jax.nn.silu(z_b)).astype(BF16), pb_ref[...], preferred_element_type=F32)
    mixed = jax.nn.sigmoid(g_a) * h_a + jax.nn.sigmoid(g_b) * h_b
    o_ref[...] = x_ref[...] + jnp.dot(mixed.astype(BF16), wo_ref[...], preferred_element_type=F32)


def _final(o_a, y_raw, bonus, proj, x, ln_w, ln_b, p_a, p_b, w_out, bd, tm):
    n, d = x.shape
    tm = min(tm, n)
    gates_w = C_Q - C_GATES
    row = lambda w: pl.BlockSpec((tm, w), lambda i: (i, 0))
    full = lambda a: pl.BlockSpec(a.shape, lambda i: (0, 0))
    return pl.pallas_call(
        _final_kernel,
        grid=(n // tm,),
        in_specs=[row(ATT_W), row(RW_W), row(RW_W), row(gates_w), row(d),
                  full(ln_w), full(ln_b), full(p_a), full(p_b), full(w_out), full(bd)],
        out_specs=row(d),
        out_shape=jax.ShapeDtypeStruct((n, d), F32),
        compiler_params=_cparams(("arbitrary",)),
        name="gated_out",
    )(o_a, y_raw, bonus, proj, x, ln_w, ln_b, p_a, p_b, w_out, bd)


def _permute_w_in(w_in):
    q, k, v, sh, gates = (w_in[:, 0:QKV], w_in[:, QKV:2 * QKV], w_in[:, 2 * QKV:3 * QKV],
                          w_in[:, 3 * QKV:3 * QKV + SHIFT_COLS], w_in[:, 3 * QKV + SHIFT_COLS:])
    pad = jnp.zeros((w_in.shape[0], P_COLS - IN_COLS), w_in.dtype)
    return jnp.concatenate([gates, q, k, v, sh, pad], axis=1).astype(BF16)


def _layer(h_p, h_s, batch, seq, dec_batch, dec_seq, shift_s, state_s, caches, lw):
    (norm_g, w_perm, shift_mu, qg, kg, lora_w, w0, a0, k_k, k_a, r_k, ln_w, ln_b, p_a, p_b, w_out,
     bd) = lw
    n_s = dec_batch * dec_seq
    prep_args = (shift_mu, lora_w, w0, a0, k_k, k_a, r_k, bd)
    out_args = (ln_w, ln_b, p_a, p_b, w_out, bd)

    proj_p = _inproj(h_p, norm_g, w_perm, qg, kg, bd, tm=2048)
    o_a_p = _band_attention(proj_p, batch, seq, tile=2048)
    pv = proj_p.reshape(batch, seq, P_COLS)
    st_p = []
    for g, (window, _) in enumerate(ATT_GROUPS):
        keep = min(window, seq)
        for c0 in (C_K, C_V):
            st_p.append(pv[:, seq - keep:, c0 + g * ATT_W:c0 + (g + 1) * ATT_W]
                        .reshape(batch, keep, HPG, HEAD_DIM))

    proj_s = _inproj(h_s, norm_g, w_perm, qg, kg, bd, tm=min(1024, n_s))
    prep_s = _rwkv_prep(proj_s, shift_s, *prep_args, seq_len=dec_seq, tm=min(512, n_s))
    if seq % CHUNK == 0 and dec_batch == seq // CHUNK:
        o_a, new_k, new_v, y_p, state_p, bonus_p = _decode_attention(
            proj_s, caches, dec_batch, dec_seq, scan_src=(proj_p, prep_args), batch=batch, seq=seq)
    else:
        o_a, new_k, new_v = _decode_attention(proj_s, caches, dec_batch, dec_seq)
        prep_p = _rwkv_prep(proj_p, jnp.zeros((batch, SHIFT_COLS), F32), *prep_args, seq_len=seq, tm=512)
        y_p, state_p = _rwkv_scan(prep_p[:6], batch, seq)
        bonus_p = prep_p[6]
    y_s, state_s_new = _seq_scan(prep_s[:6], state_s, dec_batch, dec_seq)

    out_p = _final(o_a_p, y_p, bonus_p, proj_p, h_p, *out_args, tm=512)
    out_s = _final(o_a, y_s, prep_s[6], proj_s, h_s, *out_args, tm=512)
    st_p += [state_p, pv[:, seq - 1, C_SH:C_SH + SHIFT_COLS]]
    st_s = [c for kv in zip(new_k, new_v) for c in kv]
    st_s += [state_s_new, proj_s.reshape(dec_batch, dec_seq, P_COLS)[:, dec_seq - 1, C_SH:C_SH + SHIFT_COLS]]
    return out_p, out_s, st_p, st_s


def kernel(x_prompt, x_sample, cache_k_g0, cache_v_g0, cache_k_g1, cache_v_g1, cache_k_g2, cache_v_g2, state_rwkv, state_shift, norm_g, w_in, shift_mu, q_norm_g, k_norm_g, w0, w2, a0, a2, k_k, k_a, r_k, ln_x_w, ln_x_b, p_a, p_b, w_out):
    depth = norm_g.shape[0]
    batch, seq, d_model = x_prompt.shape
    dec_batch, dec_seq, _ = x_sample.shape
    bd = _bd_ones()
    h_p = x_prompt.reshape(batch * seq, d_model)
    h_s = x_sample.reshape(dec_batch * dec_seq, d_model)
    prompt_states, sample_states = [], []
    for layer in range(depth):
        zero = jnp.zeros((LORA, RW_W), F32)
        lora_w = jnp.concatenate([jnp.concatenate([w2[layer], zero], axis=1),
                                  jnp.concatenate([zero, a2[layer]], axis=1)], axis=0).astype(BF16)
        row = lambda a: a.reshape(1, -1)
        lw = (row(norm_g[layer]), _permute_w_in(w_in[layer]), row(shift_mu[layer]),
              jnp.tile(row(q_norm_g[layer]), (1, LANES // HEAD_DIM)),
              jnp.tile(row(k_norm_g[layer]), (1, LANES // HEAD_DIM)),
              lora_w, row(w0[layer]), row(a0[layer]), row(k_k[layer]), row(k_a[layer]),
              row(r_k[layer]), row(ln_x_w[layer]), row(ln_x_b[layer]),
              p_a[layer].astype(BF16), p_b[layer].astype(BF16), w_out[layer].astype(BF16), bd)
        caches = [(cache_k_g0[layer], cache_v_g0[layer]), (cache_k_g1[layer], cache_v_g1[layer]),
                  (cache_k_g2[layer], cache_v_g2[layer])]
        h_p, h_s, st_p, st_s = _layer(h_p, h_s, batch, seq, dec_batch, dec_seq, state_shift[layer],
                                      state_rwkv[layer], caches, lw)
        prompt_states.append(st_p)
        sample_states.append(st_s)
    ps = [jnp.stack(t) for t in zip(*prompt_states)]
    ss = [jnp.stack(t) for t in zip(*sample_states)]
    return (h_p.reshape(batch, seq, d_model), h_s.reshape(dec_batch, dec_seq, d_model),
            *ps, *ss)
```

```python
import functools

import jax
import jax.numpy as jnp
from jax import lax
from jax.experimental import pallas as pl
from jax.experimental.pallas import tpu as pltpu

F32 = jnp.float32
BF16 = jnp.bfloat16

HEAD_DIM = 64
HPG = 8
ATT_GROUPS = ((128, 1), (512, 4), (2048, 16))
N_ATT_HEADS = 24
RWKV_HEADS = 8
LORA = 64
QBLOCK = 128
RMS_EPS = 1e-6
GN_EPS = 64e-5
L2_EPS = 1e-12
CHUNK = 64

LANES = 128
NEW_PAD = 16
VMEM_LIMIT = 52 * 1024 * 1024

QKV = N_ATT_HEADS * HEAD_DIM
ATT_W = HPG * HEAD_DIM
RW_W = RWKV_HEADS * HEAD_DIM
C_GATES = 0
C_Q = 3072
C_K = C_Q + QKV
C_V = C_K + QKV
C_SH = C_V + QKV
SHIFT_COLS = 3 * RW_W + 2 * LORA
IN_COLS = C_SH + SHIFT_COLS
TN = 512
P_COLS = pl.cdiv(IN_COLS, TN) * TN


def _cparams(sem):
    return pltpu.CompilerParams(dimension_semantics=sem, vmem_limit_bytes=VMEM_LIMIT)


def _segsum(x, bd):
    hi = x.astype(BF16)
    lo = (x - hi.astype(F32)).astype(BF16)
    return (jnp.dot(hi, bd, preferred_element_type=F32)
            + jnp.dot(lo, bd, preferred_element_type=F32))


def _bd_ones():
    i = jnp.arange(LANES)
    return (i[:, None] // HEAD_DIM == i[None, :] // HEAD_DIM).astype(BF16)


def _alibi_slopes(g):
    return jnp.exp2(-8.0 * (jnp.arange(HPG, dtype=F32) + (g * HPG + 1)) / N_ATT_HEADS)


def _inproj_kernel(x_ref, g_ref, w_ref, qg_ref, kg_ref, bd_ref, o_ref, xn_ref):
    j = pl.program_id(1)

    @pl.when(j == 0)
    def _():
        x = x_ref[...]
        ms = jnp.mean(x * x, axis=-1, keepdims=True)
        xn_ref[...] = (x * lax.rsqrt(ms + RMS_EPS) * g_ref[...]).astype(BF16)

    t = jnp.dot(xn_ref[...], w_ref[...], preferred_element_type=F32)
    q_lo, k_lo, k_hi = C_Q // TN, C_K // TN, C_V // TN

    @pl.when(jnp.logical_or(j < q_lo, j >= k_hi))
    def _():
        o_ref[...] = t

    @pl.when(jnp.logical_and(j >= q_lo, j < k_hi))
    def _():
        gain = jnp.where(j < k_lo, qg_ref[...], kg_ref[...])
        bd = bd_ref[...]
        for c in range(TN // LANES):
            tc = t[:, c * LANES:(c + 1) * LANES]
            ms = _segsum(tc * tc, bd) * (1.0 / HEAD_DIM)
            o_ref[:, c * LANES:(c + 1) * LANES] = tc * lax.rsqrt(ms + RMS_EPS) * gain


def _inproj(x, norm_g, w_perm, qg, kg, bd, tm):
    n, d = x.shape
    return pl.pallas_call(
        _inproj_kernel,
        grid=(n // tm, P_COLS // TN),
        in_specs=[
            pl.BlockSpec((tm, d), lambda i, j: (i, 0)),
            pl.BlockSpec((1, d), lambda i, j: (0, 0)),
            pl.BlockSpec((d, TN), lambda i, j: (0, j)),
            pl.BlockSpec((1, LANES), lambda i, j: (0, 0)),
            pl.BlockSpec((1, LANES), lambda i, j: (0, 0)),
            pl.BlockSpec((LANES, LANES), lambda i, j: (0, 0)),
        ],
        out_specs=pl.BlockSpec((tm, TN), lambda i, j: (i, j)),
        out_shape=jax.ShapeDtypeStruct((n, P_COLS), F32),
        scratch_shapes=[pltpu.VMEM((tm, d), BF16)],
        compiler_params=_cparams(("arbitrary", "arbitrary")),
        name="inproj",
    )(x, norm_g, w_perm, qg, kg, bd)


def _band_kernel(sl_ref, *refs, tile):
    n_g = len(ATT_GROUPS)
    ins, o_ref, scr = refs[0:5 * n_g], refs[5 * n_g], refs[5 * n_g + 1:]
    n = pl.program_id(1)
    p = pl.program_id(2)
    lane = lax.broadcasted_iota(jnp.int32, (1, LANES), 1)
    mask_a = lane < HEAD_DIM
    qi = lax.broadcasted_iota(jnp.int32, (QBLOCK, 2 * QBLOCK), 0)
    ki = lax.broadcasted_iota(jnp.int32, (QBLOCK, 2 * QBLOCK), 1)
    rel = qi + QBLOCK - ki
    band = jnp.logical_and(rel >= 0, rel <= QBLOCK)
    band_first = jnp.logical_and(band, jnp.logical_or(ki >= QBLOCK, n > 0))

    for g, (_, dil) in enumerate(ATT_GROUPS):
        q_ref, kp_ref, kc_ref, vp_ref, vc_ref = ins[5 * g:5 * g + 5]
        acc_scr, m_scr, l_scr = scr[3 * g:3 * g + 3]
        dist = (rel * dil).astype(F32)
        alibi = [-(sl_ref[g * HPG + 2 * p + h] * dist) for h in range(2)]
        bias = [jnp.where(band, a, -jnp.inf) for a in alibi]
        bias_first = [jnp.where(band_first, a, -jnp.inf) for a in alibi]

        def rows(r, s, dil=dil):
            if dil == 1:
                return pl.ds(s * QBLOCK, QBLOCK)
            return pl.ds(r + s * QBLOCK * dil, QBLOCK, stride=dil)

        for r in range(dil):
            for s in range(tile // (dil * QBLOCK)):
                blk_bias = bias_first if s == 0 else bias
                q = q_ref[rows(r, s), :] * (1.0 / 8.0)
                if s == 0:
                    k_prev, v_prev = kp_ref[rows(r, 0), :], vp_ref[rows(r, 0), :]
                else:
                    k_prev, v_prev = kc_ref[rows(r, s - 1), :], vc_ref[rows(r, s - 1), :]
                k = jnp.concatenate([k_prev, kc_ref[rows(r, s), :]], axis=0).astype(BF16)
                v = jnp.concatenate([v_prev, vc_ref[rows(r, s), :]], axis=0).astype(BF16)
                outs = []
                for h in range(2):
                    hm = mask_a if h == 0 else jnp.logical_not(mask_a)
                    qh = jnp.where(hm, q, 0.0).astype(BF16)
                    sc = lax.dot_general(qh, k, (((1,), (1,)), ((), ())), preferred_element_type=F32)
                    sc = sc + blk_bias[h]
                    m = jnp.max(sc, axis=-1, keepdims=True)
                    pr = jnp.exp(sc - m)
                    l = jnp.sum(pr, axis=-1, keepdims=True)
                    acc = jnp.dot(pr.astype(BF16), v, preferred_element_type=F32)
                    outs.append((acc, m, l))
                (acc0, m0, l0), (acc1, m1, l1) = outs
                acc_scr[rows(r, s), :] = jnp.where(mask_a, acc0, acc1)
                m_scr[rows(r, s), :] = jnp.where(mask_a, m0, m1)
                l_scr[rows(r, s), :] = jnp.where(mask_a, l0, l1)

    parts = [(scr[3 * g][...], scr[3 * g + 1][...], scr[3 * g + 2][...]) for g in range(n_g)]
    m_all = functools.reduce(jnp.maximum, [m for _, m, _ in parts])
    num = sum(jnp.exp(m - m_all) * a for a, m, _ in parts)
    den = sum(jnp.exp(m - m_all) * l for _, m, l in parts)
    o_ref[...] = num / den


def _band_attention(proj, batch, seq, tile):
    n_g = len(ATT_GROUPS)
    tile = min(tile, seq)
    assert seq % tile == 0
    nt = seq // tile
    in_specs = [pl.BlockSpec(memory_space=pltpu.SMEM)]
    for g, (window, dil) in enumerate(ATT_GROUPS):
        assert window // dil == QBLOCK
        halo = QBLOCK * dil
        assert tile % halo == 0
        nh = seq // halo
        cq, ck, cv = ((c + g * ATT_W) // LANES for c in (C_Q, C_K, C_V))

        def cur(col):
            return pl.BlockSpec((tile, LANES), lambda b, n, p: (b * nt + n, col + p))

        def prev(col, halo=halo, nh=nh):
            return pl.BlockSpec(
                (halo, LANES),
                lambda b, n, p: (b * nh + jnp.maximum(n * (tile // halo) - 1, 0), col + p))

        in_specs += [cur(cq), prev(ck), cur(ck), prev(cv), cur(cv)]
    slopes = jnp.concatenate([_alibi_slopes(g) for g in range(n_g)])
    return pl.pallas_call(
        functools.partial(_band_kernel, tile=tile),
        grid=(batch, nt, ATT_W // LANES),
        in_specs=in_specs,
        out_specs=pl.BlockSpec((tile, LANES), lambda b, n, p: (b * nt + n, p)),
        out_shape=jax.ShapeDtypeStruct((batch * seq, ATT_W), F32),
        scratch_shapes=[pltpu.VMEM((tile, LANES), F32)] * (3 * n_g),
        compiler_params=_cparams(("arbitrary", "arbitrary", "arbitrary")),
        name="band_attn",
    )(slopes, *([proj] * (5 * n_g)))


def _decode_body(sl_ref, q_ref, kn_ref, vn_ref, kb, vb, o_ref, ko, vo, hh, t_new, hps):
    n_g = len(ATT_GROUPS)
    ti_n = lax.broadcasted_iota(jnp.int32, (t_new, NEW_PAD), 0)
    tj_n = lax.broadcasted_iota(jnp.int32, (t_new, NEW_PAD), 1)
    pi = lax.broadcasted_iota(jnp.int32, (NEW_PAD, LANES), 0)
    pj = lax.broadcasted_iota(jnp.int32, (NEW_PAD, LANES), 1)
    place = jnp.logical_and(pj == pi + (LANES - t_new), pi < t_new).astype(BF16)
    tail = lax.broadcasted_iota(jnp.int32, (HEAD_DIM, LANES), 1) >= LANES - t_new
    nt = (((1,), (1,)), ((), ()))
    tn = (((0,), (0,)), ((), ()))
    zpad = jnp.zeros((NEW_PAD - t_new, HEAD_DIM), F32)

    combos = [(j, g) for j in range(hps) for g in range(n_g)]
    d_new = ti_n - tj_n
    kn = {c: jnp.concatenate([kn_ref[c[1], c[0]], zpad], axis=0) for c in combos}
    vn = {c: jnp.concatenate([vn_ref[c[1], c[0]], zpad], axis=0) for c in combos}
    q = {c: q_ref[c[1], c[0]].astype(BF16) for c in combos}
    s_buf = {(j, g): jnp.dot(q[j, g], kb[g][j].astype(BF16), preferred_element_type=F32)
             for j, g in combos}
    s_new = {c: lax.dot_general(q[c], kn[c].astype(BF16), nt, preferred_element_type=F32)
             for c in combos}
    p_buf, p_new, stats = {}, {}, {}
    for j, g in combos:
        wb, dil = ATT_GROUPS[g]
        slope = sl_ref[g * HPG + hh * hps + j]
        wi = lax.broadcasted_iota(jnp.int32, (t_new, wb), 1)
        ti = lax.broadcasted_iota(jnp.int32, (t_new, wb), 0)
        d_buf = wb + ti - wi
        ok_buf = wi >= ti
        ok_new = d_new >= 0
        if dil > 1:
            ok_buf = jnp.logical_and(ok_buf, (d_buf & (dil - 1)) == 0)
            ok_new = jnp.logical_and(ok_new, (d_new & (dil - 1)) == 0)
        sb = jnp.where(ok_buf, s_buf[j, g] * (1.0 / 8.0) - slope * d_buf.astype(F32), -jnp.inf)
        sn = jnp.where(ok_new, s_new[j, g] * (1.0 / 8.0) - slope * d_new.astype(F32), -jnp.inf)
        m = jnp.maximum(jnp.max(sb, axis=-1, keepdims=True), jnp.max(sn, axis=-1, keepdims=True))
        pb, pn = jnp.exp(sb - m), jnp.exp(sn - m)
        p_buf[j, g], p_new[j, g] = pb.astype(BF16), pn.astype(BF16)
        stats[j, g] = (m, jnp.sum(pb, axis=-1, keepdims=True) + jnp.sum(pn, axis=-1, keepdims=True))
    acc = {(j, g): lax.dot_general(p_buf[j, g], vb[g][j].astype(BF16), nt, preferred_element_type=F32)
           + jnp.dot(p_new[j, g], vn[j, g].astype(BF16), preferred_element_type=F32)
           for j, g in combos}
    for j in range(hps):
        m_all = functools.reduce(jnp.maximum, [stats[j, g][0] for g in range(n_g)])
        wts = [jnp.exp(stats[j, g][0] - m_all) for g in range(n_g)]
        num = sum(wts[g] * acc[j, g] for g in range(n_g))
        den = sum(wts[g] * stats[j, g][1] for g in range(n_g))
        o_ref[j] = num / den

    for j, g in combos:
        wb = ATT_GROUPS[g][0]
        for src, new, dst in ((kb[g], kn[j, g], ko[g]), (vb[g], vn[j, g], vo[g])):
            rolled = pltpu.roll(src[j], wb - t_new, 1)
            hi = new.astype(BF16)
            lo = (new - hi.astype(F32)).astype(BF16)
            placed = (lax.dot_general(hi, place, tn, preferred_element_type=F32)
                      + lax.dot_general(lo, place, tn, preferred_element_type=F32))
            if wb > LANES:
                dst[j, :, 0:wb - LANES] = rolled[:, 0:wb - LANES]
            dst[j, :, wb - LANES:wb] = jnp.where(tail, placed, rolled[:, wb - LANES:wb])


def _decode_kernel(sl_ref, q_ref, kn_ref, vn_ref, *refs, t_new, hps):
    n_c = 2 * len(ATT_GROUPS)
    _decode_body(sl_ref, q_ref, kn_ref, vn_ref, refs[0:n_c:2], refs[1:n_c:2], refs[n_c],
                 refs[n_c + 1::2], refs[n_c + 2::2], pl.program_id(1), t_new, hps)


def _decode_scan_kernel(sl_ref, q_ref, kn_ref, vn_ref, *refs, t_new, hps, nb):
    n_c = 2 * len(ATT_GROUPS)
    caches_in = refs[0:n_c]
    cur_ref, wa_ref, pcur_ref, pwa_ref = refs[n_c:n_c + 4]
    param_refs = refs[n_c + 4:n_c + 13]
    o_ref = refs[n_c + 13]
    caches_out = refs[n_c + 14:2 * n_c + 14]
    y_ref, zt_ref, bonus_ref, z_scr, tinv_scr, lp_scr, akv_scr, rbk_scr, tok_scr = refs[2 * n_c + 14:]
    c, half = pl.program_id(0), pl.program_id(1)
    n = nb * (RW_W // LANES)

    def decode(hh):
        _decode_body(sl_ref, q_ref, kn_ref, vn_ref, caches_in[0::2], caches_in[1::2], o_ref,
                     caches_out[0::2], caches_out[1::2], hh, t_new, hps)

    @pl.when(jnp.logical_and(c == 0, half == 0))
    def _():
        z_scr[...] = jnp.zeros_like(z_scr)

    @pl.when(half == 0)
    def _():
        decode(0)
        params = tuple(ref[...] for ref in param_refs)
        row0 = lax.broadcasted_iota(jnp.int32, (CHUNK, 1), 0) == 0
        seqs = []
        for bi in range(nb):
            def prev(x, p8):
                edge = jnp.where(c == 0, 0.0, p8[7:8, :])
                return jnp.where(row0, edge, pltpu.roll(x, 1, 0))
            cur, wa = cur_ref[bi], wa_ref[bi]
            outs = _prep_math(cur, wa, prev(cur, pcur_ref[bi]), prev(wa, pwa_ref[bi]), *params)
            seqs.append(outs[:6])
            bonus_ref[bi] = outs[6]
            for j in range(6):
                tok_scr[j, bi] = outs[j]
        tinv, lp, akv, rbk = _scan_first_half(_scan_chains(seqs))
        tinv_scr[...] = jnp.stack(tinv, axis=0)
        lp_scr[...] = jnp.stack(lp, axis=0)
        akv_scr[...] = jnp.stack(akv, axis=0)
        rbk_scr[...] = jnp.stack(rbk, axis=0)

    @pl.when(half == 1)
    def _():
        decode(1)
        seqs = [tuple(tok_scr[j, bi] for j in range(6)) for bi in range(nb)]
        carried = [[scr[i] for i in range(n)] for scr in (tinv_scr, lp_scr, akv_scr, rbk_scr)]
        ys, z_news = _scan_second_half(_scan_chains(seqs), *carried, [z_scr[i] for i in range(n)])
        _store_scan(ys, z_news, y_ref, zt_ref, z_scr, nb)


def _unpair_state(zt, batch):
    zt = zt.reshape(batch, RW_W // LANES, LANES, LANES)
    s_a = zt[:, :, 0:HEAD_DIM, 0:HEAD_DIM]
    s_b = zt[:, :, HEAD_DIM:, HEAD_DIM:]
    return jnp.stack([s_a, s_b], axis=2).reshape(batch, RWKV_HEADS, HEAD_DIM, HEAD_DIM)


def _decode_attention(proj, caches, dec_batch, t_new, scan_src=None, batch=0, seq=0):
    n_g = len(ATT_GROUPS)
    hps = HPG // 2
    assert all(k.shape[1] == w for (k, _), (w, _) in zip(caches, ATT_GROUPS))

    def heads(col0):
        x = proj[:, col0:col0 + QKV].reshape(dec_batch, t_new, n_g, HPG, HEAD_DIM)
        return jnp.transpose(x, (0, 2, 3, 1, 4))

    q, kn, vn = heads(C_Q), heads(C_K), heads(C_V)
    native = [jnp.transpose(c, (0, 2, 3, 1)) for kv in caches for c in kv]
    slopes = jnp.concatenate([_alibi_slopes(g) for g in range(n_g)])

    new_q = pl.BlockSpec((None, n_g, hps, t_new, HEAD_DIM), lambda b, h: (b, 0, h, 0, 0))
    cache_specs = [pl.BlockSpec((None, hps, HEAD_DIM, c.shape[-1]), lambda b, h: (b, h, 0, 0))
                   for c in native]
    in_specs = [pl.BlockSpec(memory_space=pltpu.SMEM), new_q, new_q, new_q] + cache_specs
    out_specs = [pl.BlockSpec((None, hps, t_new, HEAD_DIM), lambda b, h: (b, h, 0, 0))] + cache_specs
    out_shape = ([jax.ShapeDtypeStruct((dec_batch, HPG, t_new, HEAD_DIM), F32)]
                 + [jax.ShapeDtypeStruct(c.shape, F32) for c in native])
    grid = (dec_batch, HPG // hps)
    if scan_src is None:
        outs = pl.pallas_call(
            functools.partial(_decode_kernel, t_new=t_new, hps=hps),
            grid=grid, in_specs=in_specs, out_specs=out_specs, out_shape=out_shape,
            compiler_params=_cparams(("arbitrary", "arbitrary")),
            name="decode_attn",
        )(slopes, q, kn, vn, *native)
        scan_out = ()
    else:
        assert grid == (seq // CHUNK, 2)
        proj_p, (mu, *prep_params) = scan_src
        pv = proj_p.reshape(batch, seq, P_COLS)
        n = batch * (RW_W // LANES)
        main_w = 3 * RW_W
        cb, wb = C_SH // main_w, (C_SH + main_w) // LANES

        def halo(c, h):
            return jnp.maximum(c * (CHUNK // 8) - 1, 0)

        tok = pl.BlockSpec((batch, CHUNK, RW_W), lambda c, h: (0, c, 0))
        st = pl.BlockSpec((n, LANES, LANES), lambda c, h: (0, 0, 0))
        full = lambda a: pl.BlockSpec(a.shape, lambda c, h: (0, 0))
        params = (mu[:, 0:main_w], mu[:, main_w:], *prep_params)
        prep_specs = [pl.BlockSpec((batch, CHUNK, main_w), lambda c, h: (0, c, cb)),
                      pl.BlockSpec((batch, CHUNK, LANES), lambda c, h: (0, c, wb)),
                      pl.BlockSpec((batch, 8, main_w), lambda c, h: (0, halo(c, h), cb)),
                      pl.BlockSpec((batch, 8, LANES), lambda c, h: (0, halo(c, h), wb))]
        sq = pltpu.VMEM((n, LANES, LANES), F32)
        tok_sds = jax.ShapeDtypeStruct((batch, seq, RW_W), F32)
        outs = pl.pallas_call(
            functools.partial(_decode_scan_kernel, t_new=t_new, hps=hps, nb=batch),
            grid=grid,
            in_specs=in_specs + prep_specs + [full(a) for a in params],
            out_specs=out_specs + [tok, st, tok],
            out_shape=out_shape + [tok_sds, jax.ShapeDtypeStruct((n, LANES, LANES), F32), tok_sds],
            scratch_shapes=[sq, sq, sq, sq, pltpu.VMEM((n, LANES, 2 * LANES), F32),
                            pltpu.VMEM((6, batch, CHUNK, RW_W), F32)],
            compiler_params=_cparams(("arbitrary", "arbitrary")),
            name="decode_attn_rwkv_scan",
        )(slopes, q, kn, vn, *native, pv, pv, pv, pv, *params)
        flat = (batch * seq, RW_W)
        scan_out = (outs[-3].reshape(flat), _unpair_state(outs[-2], batch), outs[-1].reshape(flat))
        outs = outs[:-3]
    o_a = jnp.transpose(outs[0], (0, 2, 1, 3)).reshape(dec_batch * t_new, ATT_W)
    new = [jnp.transpose(c, (0, 3, 1, 2)) for c in outs[1:]]
    return (o_a, new[0::2], new[1::2]) + scan_out


def _prep_kernel(cur_ref, wa_ref, pcur_ref, pwa_ref, fcur_ref, fwa_ref, mu_ref, muwa_ref, lora_ref,
                 w0_ref, a0_ref, kk_ref, ka_ref, rk_ref, bd_ref,
                 r_o, lw_o, k_o, v_o, kkn_o, b_o, bonus_o, *, seq_len, tm):
    i = pl.program_id(0)
    row = lax.broadcasted_iota(jnp.int32, (tm, 1), 0)

    def shifted(cur, p8, first):
        rolled = pltpu.roll(cur, 1, 0)
        if seq_len >= tm:
            at_start = (i * tm) % seq_len == 0
            edge = jnp.where(at_start, first, p8[7:8, :])
            return jnp.where(row == 0, edge, rolled)
        return jnp.where(row % seq_len == 0, first, rolled)

    cur = cur_ref[...]
    wa = wa_ref[...]
    params = tuple(ref[...] for ref in (mu_ref, muwa_ref, lora_ref, w0_ref, a0_ref, kk_ref, ka_ref, rk_ref,
                                        bd_ref))
    outs = _prep_math(cur, wa, shifted(cur, pcur_ref[...], fcur_ref[...]),
                      shifted(wa, pwa_ref[...], fwa_ref[...]), *params)
    for ref, val in zip((r_o, lw_o, k_o, v_o, kkn_o, b_o, bonus_o), outs):
        ref[...] = val


def _prep_math(cur, wa, prev_cur, prev_wa, mu, muwa, lora, w0, a0, k_k, k_a, r_k, bd):
    xs = cur + (prev_cur - cur) * mu
    xwa = wa + (prev_wa - wa) * muwa
    r = xs[:, 0:RW_W]
    kr = xs[:, RW_W:2 * RW_W]
    vr = xs[:, 2 * RW_W:3 * RW_W]
    lane = lax.broadcasted_iota(jnp.int32, (1, LANES), 1)
    lin = jnp.where(lane < LORA, jnp.tanh(xwa), xwa).astype(BF16)
    lo = jnp.dot(lin, lora, preferred_element_type=F32)
    z = -(w0 + lo[:, 0:RW_W])
    softplus = jnp.maximum(z, 0.0) + jnp.log1p(jnp.exp(-jnp.abs(z)))
    lw = -jnp.exp(-softplus - 0.5)
    a = jax.nn.sigmoid(a0 + lo[:, RW_W:2 * RW_W])
    kk = kr * k_k
    k_mod = kr * (1.0 + (a - 1.0) * k_a)
    rkk = r * k_mod * r_k
    kkn, b, bonus = [], [], []
    for c in range(RW_W // LANES):
        cols = slice(c * LANES, (c + 1) * LANES)
        kc = kk[:, cols]
        kn = kc / jnp.maximum(jnp.sqrt(_segsum(kc * kc, bd)), L2_EPS)
        kkn.append(kn)
        b.append(kn * a[:, cols])
        bonus.append(_segsum(rkk[:, cols], bd) * vr[:, cols])
    cat = lambda parts: jnp.concatenate(parts, axis=1)
    return r, lw, k_mod, vr, cat(kkn), cat(b), cat(bonus)


def _rwkv_prep(proj, first, mu, lora_w, w0, a0, k_k, k_a, r_k, bd, seq_len, tm):
    n = proj.shape[0]
    main_w = 3 * RW_W
    cb, wb = C_SH // main_w, (C_SH + main_w) // LANES
    per_seq_first = seq_len >= tm
    if per_seq_first:
        f_main = first[:, None, 0:main_w]
        f_wa = first[:, None, main_w:]
        fm_spec = pl.BlockSpec((None, 1, main_w), lambda i: ((i * tm) // seq_len, 0, 0))
        fw_spec = pl.BlockSpec((None, 1, LANES), lambda i: ((i * tm) // seq_len, 0, 0))
    else:
        rep = jnp.repeat(first, seq_len, axis=0)
        f_main, f_wa = rep[:, 0:main_w], rep[:, main_w:]
        fm_spec = pl.BlockSpec((tm, main_w), lambda i: (i, 0))
        fw_spec = pl.BlockSpec((tm, LANES), lambda i: (i, 0))
    vec = lambda w: pl.BlockSpec((1, w), lambda i: (0, 0))
    out_spec = pl.BlockSpec((tm, RW_W), lambda i: (i, 0))
    out_sds = jax.ShapeDtypeStruct((n, RW_W), F32)
    return pl.pallas_call(
        functools.partial(_prep_kernel, seq_len=seq_len, tm=tm),
        grid=(n // tm,),
        in_specs=[
            pl.BlockSpec((tm, main_w), lambda i: (i, cb)),
            pl.BlockSpec((tm, LANES), lambda i: (i, wb)),
            pl.BlockSpec((8, main_w), lambda i: (jnp.maximum(i * (tm // 8) - 1, 0), cb)),
            pl.BlockSpec((8, LANES), lambda i: (jnp.maximum(i * (tm // 8) - 1, 0), wb)),
            fm_spec, fw_spec,
            vec(main_w), vec(LANES),
            pl.BlockSpec((LANES, 2 * RW_W), lambda i: (0, 0)),
            vec(RW_W), vec(RW_W), vec(RW_W), vec(RW_W), vec(RW_W),
            pl.BlockSpec((LANES, LANES), lambda i: (0, 0)),
        ],
        out_specs=[out_spec] * 7,
        out_shape=[out_sds] * 7,
        compiler_params=_cparams(("arbitrary",)),
        name="rwkv_prep",
    )(proj, proj, proj, proj, f_main, f_wa, mu[:, 0:main_w], mu[:, main_w:], lora_w,
      w0, a0, k_k, k_a, r_k, bd)


def _mm(a, b):
    return jnp.dot(a.astype(BF16), b.astype(BF16), preferred_element_type=F32)


def _mm_nt(a, b):
    return lax.dot_general(a.astype(BF16), b.astype(BF16), (((1,), (1,)), ((), ())),
                           preferred_element_type=F32)


def _mm_tn(a, b):
    return lax.dot_general(a.astype(BF16), b.astype(BF16), (((0,), (0,)), ((), ())),
                           preferred_element_type=F32)


SPLIT_STEPS = 2


def _scan_chains(seqs):
    C = CHUNK
    ti = lax.broadcasted_iota(jnp.int32, (C, C), 0)
    tj = lax.broadcasted_iota(jnp.int32, (C, C), 1)
    tri = (ti >= tj).astype(F32)
    in_a = lax.broadcasted_iota(jnp.int32, (1, LANES), 1) < HEAD_DIM

    def stack(x):
        return jnp.concatenate([jnp.where(in_a, x, 0.0), jnp.where(in_a, 0.0, x)], axis=0)

    chains = []
    for r, lw, k, v, kk, b in seqs:
        cum = jnp.dot(tri, lw, precision=lax.Precision.HIGHEST, preferred_element_type=F32)
        cum_end = cum[C - 1:C, :]
        w_in = jnp.exp(-cum)
        w_rest = jnp.exp(cum_end - cum)
        vals = (("a", jnp.exp(cum - lw) * (-kk)), ("b", b * w_in), ("k", k * w_in), ("r", r * jnp.exp(cum)),
                ("v", v), ("bh", b * w_rest), ("kh", k * w_rest))
        w_end = jnp.exp(cum_end)
        for p in range(RW_W // LANES):
            cols = slice(p * LANES, (p + 1) * LANES)
            ch = {name: stack(x[:, cols]) for name, x in vals}
            ch["w_end"] = w_end[:, cols]
            chains.append(ch)
    return chains


def _scan_masks():
    C = CHUNK
    si = lax.broadcasted_iota(jnp.int32, (2 * C, 2 * C), 0)
    sj = lax.broadcasted_iota(jnp.int32, (2 * C, 2 * C), 1)
    same = (si // C) == (sj // C)
    strict = jnp.logical_and(same, (si % C) > (sj % C))
    incl = jnp.logical_and(same, (si % C) >= (sj % C))
    return strict, incl, (si == sj).astype(F32)


def _double(prod, lp, steps):
    n = 2 * CHUNK
    for _ in range(steps):
        res = [_mm(jnp.concatenate([a, x], axis=0), x) for a, x in zip(prod, lp)]
        prod = [a + r[0:n] for a, r in zip(prod, res)]
        lp = [r[n:2 * n] for r in res]
    return prod, lp


def _scan_first_half(chains):
    C = CHUNK
    strict, incl, eye = _scan_masks()
    g = [_mm_nt(jnp.concatenate([ch["a"], ch["r"]], axis=0), jnp.concatenate([ch["b"], ch["k"]], axis=0))
         for ch in chains]
    ab = [jnp.where(strict, x[0:2 * C, 0:2 * C], 0.0) for x in g]
    ak = [jnp.where(strict, x[0:2 * C, 2 * C:4 * C], 0.0) for x in g]
    rbk = [jnp.concatenate([jnp.where(incl, x[2 * C:4 * C, 0:2 * C], 0.0),
                            jnp.where(incl, x[2 * C:4 * C, 2 * C:4 * C], 0.0)], axis=1) for x in g]
    akv = [_mm(x, ch["v"]) for x, ch in zip(ak, chains)]
    tinv, lp = _double([eye + x for x in ab], [_mm(x, x) for x in ab], SPLIT_STEPS)
    return tinv, lp, akv, rbk


def _scan_second_half(chains, tinv, lp, akv, rbk, zs):
    C = CHUNK
    eye = _scan_masks()[2]
    n = len(chains)
    levels = (C - 1).bit_length()
    tinv, lp = _double(tinv, lp, levels - 2 - SPLIT_STEPS)
    tinv = [a + _mm(a, x) for a, x in zip(tinv, lp)]
    au = [_mm(tinv[i], jnp.concatenate([chains[i]["a"], akv[i]], axis=1)) for i in range(n)]
    zeros = jnp.zeros((2 * C, LANES), F32)
    rhs = [jnp.concatenate([au[i], jnp.concatenate([zeros, chains[i]["v"]], axis=1)], axis=0)
           for i in range(n)]
    mn = [_mm_tn(jnp.concatenate([chains[i]["bh"], chains[i]["kh"]], axis=0), rhs[i])
          for i in range(n)]
    ry = [_mm(rbk[i], rhs[i]) for i in range(n)]
    yz = [_mm(jnp.concatenate([chains[i]["r"] + ry[i][:, 0:LANES],
                               mn[i][:, 0:LANES] + eye * chains[i]["w_end"]], axis=0), zs[i])
          for i in range(n)]
    z_news = [yz[i][2 * C:2 * C + LANES] + mn[i][:, LANES:2 * LANES] for i in range(n)]
    y_st = [yz[i][0:2 * C] + ry[i][:, LANES:2 * LANES] for i in range(n)]
    return [y[0:C] + y[C:2 * C] for y in y_st], z_news


def _store_scan(ys, z_news, y_ref, zt_ref, z_scr, nb):
    n_pairs = RW_W // LANES
    y_ref[...] = jnp.stack([jnp.concatenate(ys[bi * n_pairs:(bi + 1) * n_pairs], axis=1)
                            for bi in range(nb)], axis=0)
    z_scr[...] = jnp.stack(z_news, axis=0)
    zt_ref[...] = jnp.stack([z.T for z in z_news], axis=0)


def _scan_kernel(r_ref, lw_ref, k_ref, v_ref, kk_ref, b_ref, y_ref, zt_ref, z_scr, *, nb):
    @pl.when(pl.program_id(0) == 0)
    def _():
        z_scr[...] = jnp.zeros_like(z_scr)

    chains = _scan_chains([tuple(ref[bi] for ref in (r_ref, lw_ref, k_ref, v_ref, kk_ref, b_ref))
                           for bi in range(nb)])
    ys, z_news = _scan_second_half(chains, *_scan_first_half(chains),
                                   [z_scr[i] for i in range(len(chains))])
    _store_scan(ys, z_news, y_ref, zt_ref, z_scr, nb)


def _rwkv_scan(prep, batch, seq):
    assert seq % CHUNK == 0
    n_pairs = RW_W // LANES
    tok = pl.BlockSpec((batch, CHUNK, RW_W), lambda c: (0, c, 0))
    st = pl.BlockSpec((batch * n_pairs, LANES, LANES), lambda c: (0, 0, 0))
    y, zt = pl.pallas_call(
        functools.partial(_scan_kernel, nb=batch),
        grid=(seq // CHUNK,),
        in_specs=[tok] * 6,
        out_specs=[tok, st],
        out_shape=[jax.ShapeDtypeStruct((batch, seq, RW_W), F32),
                   jax.ShapeDtypeStruct((batch * n_pairs, LANES, LANES), F32)],
        scratch_shapes=[pltpu.VMEM((batch * n_pairs, LANES, LANES), F32)],
        compiler_params=_cparams(("arbitrary",)),
        name="rwkv_scan",
    )(*[x.reshape(batch, seq, RW_W) for x in prep])
    return y.reshape(batch * seq, RW_W), _unpair_state(zt, batch)


def _seq_scan_kernel(r_ref, lw_ref, k_ref, v_ref, kk_ref, b_ref, s_ref, y_ref, so_ref, t_scr, y_scr,
                     *, t_new, nb):
    for j, ref in enumerate((r_ref, lw_ref, k_ref, v_ref, kk_ref, b_ref)):
        for t in range(t_new):
            x = ref[pl.ds(t, nb, stride=t_new), :].T
            t_scr[j, t] = jnp.exp(x) if j == 1 else x

    for hd in range(LANES // HEAD_DIM):
        lo = hd * HEAD_DIM
        rows = pl.ds(lo, HEAD_DIM)

        def body(vi, carry, hd=hd, lo=lo, rows=rows):
            sv = s_ref[hd, vi]
            for t in range(t_new):
                sa = jnp.sum(sv * t_scr[4, t, rows, :], axis=0, keepdims=True)
                sv = (sv * t_scr[1, t, rows, :] - sa * t_scr[5, t, rows, :]
                      + t_scr[3, t, pl.ds(lo + vi, 1), :] * t_scr[2, t, rows, :])
                y_scr[t, pl.ds(lo + vi, 1), :] = jnp.sum(sv * t_scr[0, t, rows, :], axis=0, keepdims=True)
            so_ref[hd, vi] = sv
            return carry

        lax.fori_loop(0, HEAD_DIM, body, 0)

    for t in range(t_new):
        y_ref[pl.ds(t, nb, stride=t_new), :] = y_scr[t].T


def _seq_scan(prep, state, dec_batch, t_new):
    hpp = LANES // HEAD_DIM
    rows = dec_batch * t_new
    vec = pl.BlockSpec((rows, LANES), lambda p: (0, p))
    st = pl.BlockSpec((hpp, HEAD_DIM, HEAD_DIM, dec_batch), lambda p: (p, 0, 0, 0))
    s_native = jnp.transpose(state, (1, 2, 3, 0))
    y, s_new = pl.pallas_call(
        functools.partial(_seq_scan_kernel, t_new=t_new, nb=dec_batch),
        grid=(RWKV_HEADS // hpp,),
        in_specs=[vec] * 6 + [st],
        out_specs=[vec, st],
        out_shape=[jax.ShapeDtypeStruct((rows, RW_W), F32), jax.ShapeDtypeStruct(s_native.shape, F32)],
        scratch_shapes=[pltpu.VMEM((6, t_new, LANES, dec_batch), F32),
                        pltpu.VMEM((t_new, LANES, dec_batch), F32)],
        compiler_params=_cparams(("arbitrary",)),
        name="rwkv_seq_scan",
    )(*prep, s_native)
    return y, jnp.transpose(s_new, (3, 0, 1, 2))


def _final_kernel(oa_ref, y_ref, bonus_ref, gates_ref, x_ref, lnw_ref, lnb_ref, pa_ref, pb_ref, wo_ref, bd_ref,
                  o_ref):
    o_a = oa_ref[...]
    gates = gates_ref[...]
    z_a = gates[:, 0:ATT_W]
    z_b = gates[:, ATT_W:ATT_W + RW_W]
    g_a = gates[:, ATT_W + RW_W:ATT_W + RW_W + 1024]
    g_b = gates[:, ATT_W + RW_W + 1024:]
    y = y_ref[...]
    bd = bd_ref[...]
    cols_out = []
    for c in range(RW_W // LANES):
        cols = slice(c * LANES, (c + 1) * LANES)
        yc = y[:, cols]
        mu = _segsum(yc, bd) * (1.0 / HEAD_DIM)
        d = yc - mu
        var = _segsum(d * d, bd) * (1.0 / HEAD_DIM)
        cols_out.append(d * lax.rsqrt(var + GN_EPS))
    yn = jnp.concatenate(cols_out, axis=1)
    o_b = yn * lnw_ref[...] + lnb_ref[...] + bonus_ref[...]
    h_a = jnp.dot((o_a * jax.nn.silu(z_a)).astype(BF16), pa_ref[...], preferred_element_type=F32)
    h_b = jnp.dot((o_b * jax.nn.silu(z_b)).astype(BF16), pb_ref[...], preferred_element_type=F32)
    mixed = jax.nn.sigmoid(g_a) * h_a + jax.nn.sigmoid(g_b) * h_b
    o_ref[...] = x_ref[...] + jnp.dot(mixed.astype(BF16), wo_ref[...], preferred_element_type=F32)


def _final(o_a, y_raw, bonus, proj, x, ln_w, ln_b, p_a, p_b, w_out, bd, tm):
    n, d = x.shape
    tm = min(tm, n)
    gates_w = C_Q - C_GATES
    row = lambda w: pl.BlockSpec((tm, w), lambda i: (i, 0))
    full = lambda a: pl.BlockSpec(a.shape, lambda i: (0, 0))
    return pl.pallas_call(
        _final_kernel,
        grid=(n // tm,),
        in_specs=[row(ATT_W), row(RW_W), row(RW_W), row(gates_w), row(d),
                  full(ln_w), full(ln_b), full(p_a), full(p_b), full(w_out), full(bd)],
        out_specs=row(d),
        out_shape=jax.ShapeDtypeStruct((n, d), F32),
        compiler_params=_cparams(("arbitrary",)),
        name="gated_out",
    )(o_a, y_raw, bonus, proj, x, ln_w, ln_b, p_a, p_b, w_out, bd)


def _permute_w_in(w_in):
    q, k, v, sh, gates = (w_in[:, 0:QKV], w_in[:, QKV:2 * QKV], w_in[:, 2 * QKV:3 * QKV],
                          w_in[:, 3 * QKV:3 * QKV + SHIFT_COLS], w_in[:, 3 * QKV + SHIFT_COLS:])
    pad = jnp.zeros((w_in.shape[0], P_COLS - IN_COLS), w_in.dtype)
    return jnp.concatenate([gates, q, k, v, sh, pad], axis=1).astype(BF16)


def _layer(h_p, h_s, batch, seq, dec_batch, dec_seq, shift_s, state_s, caches, lw):
    (norm_g, w_perm, shift_mu, qg, kg, lora_w, w0, a0, k_k, k_a, r_k, ln_w, ln_b, p_a, p_b, w_out,
     bd) = lw
    n_s = dec_batch * dec_seq
    prep_args = (shift_mu, lora_w, w0, a0, k_k, k_a, r_k, bd)
    out_args = (ln_w, ln_b, p_a, p_b, w_out, bd)

    proj_p = _inproj(h_p, norm_g, w_perm, qg, kg, bd, tm=2048)
    o_a_p = _band_attention(proj_p, batch, seq, tile=2048)
    pv = proj_p.reshape(batch, seq, P_COLS)
    st_p = []
    for g, (window, _) in enumerate(ATT_GROUPS):
        keep = min(window, seq)
        for c0 in (C_K, C_V):
            st_p.append(pv[:, seq - keep:, c0 + g * ATT_W:c0 + (g + 1) * ATT_W]
                        .reshape(batch, keep, HPG, HEAD_DIM))

    proj_s = _inproj(h_s, norm_g, w_perm, qg, kg, bd, tm=min(1024, n_s))
    prep_s = _rwkv_prep(proj_s, shift_s, *prep_args, seq_len=dec_seq, tm=min(512, n_s))
    if seq % CHUNK == 0 and dec_batch == seq // CHUNK:
        o_a, new_k, new_v, y_p, state_p, bonus_p = _decode_attention(
            proj_s, caches, dec_batch, dec_seq, scan_src=(proj_p, prep_args), batch=batch, seq=seq)
    else:
        o_a, new_k, new_v = _decode_attention(proj_s, caches, dec_batch, dec_seq)
        prep_p = _rwkv_prep(proj_p, jnp.zeros((batch, SHIFT_COLS), F32), *prep_args, seq_len=seq, tm=512)
        y_p, state_p = _rwkv_scan(prep_p[:6], batch, seq)
        bonus_p = prep_p[6]
    y_s, state_s_new = _seq_scan(prep_s[:6], state_s, dec_batch, dec_seq)

    out_p = _final(o_a_p, y_p, bonus_p, proj_p, h_p, *out_args, tm=512)
    out_s = _final(o_a, y_s, prep_s[6], proj_s, h_s, *out_args, tm=512)
    st_p += [state_p, pv[:, seq - 1, C_SH:C_SH + SHIFT_COLS]]
    st_s = [c for kv in zip(new_k, new_v) for c in kv]
    st_s += [state_s_new, proj_s.reshape(dec_batch, dec_seq, P_COLS)[:, dec_seq - 1, C_SH:C_SH + SHIFT_COLS]]
    return out_p, out_s, st_p, st_s


def kernel(x_prompt, x_sample, cache_k_g0, cache_v_g0, cache_k_g1, cache_v_g1, cache_k_g2, cache_v_g2, state_rwkv, state_shift, norm_g, w_in, shift_mu, q_norm_g, k_norm_g, w0, w2, a0, a2, k_k, k_a, r_k, ln_x_w, ln_x_b, p_a, p_b, w_out):
    depth = norm_g.shape[0]
    batch, seq, d_model = x_prompt.shape
    dec_batch, dec_seq, _ = x_sample.shape
    bd = _bd_ones()
    h_p = x_prompt.reshape(batch * seq, d_model)
    h_s = x_sample.reshape(dec_batch * dec_seq, d_model)
    prompt_states, sample_states = [], []
    for layer in range(depth):
        zero = jnp.zeros((LORA, RW_W), F32)
        lora_w = jnp.concatenate([jnp.concatenate([w2[layer], zero], axis=1),
                                  jnp.concatenate([zero, a2[layer]], axis=1)], axis=0).astype(BF16)
        row = lambda a: a.reshape(1, -1)
        lw = (row(norm_g[layer]), _permute_w_in(w_in[layer]), row(shift_mu[layer]),
              jnp.tile(row(q_norm_g[layer]), (1, LANES // HEAD_DIM)),
              jnp.tile(row(k_norm_g[layer]), (1, LANES // HEAD_DIM)),
              lora_w, row(w0[layer]), row(a0[layer]), row(k_k[layer]), row(k_a[layer]),
              row(r_k[layer]), row(ln_x_w[layer]), row(ln_x_b[layer]),
              p_a[layer].astype(BF16), p_b[layer].astype(BF16), w_out[layer].astype(BF16), bd)
        caches = [(cache_k_g0[layer], cache_v_g0[layer]), (cache_k_g1[layer], cache_v_g1[layer]),
                  (cache_k_g2[layer], cache_v_g2[layer])]
        h_p, h_s, st_p, st_s = _layer(h_p, h_s, batch, seq, dec_batch, dec_seq, state_shift[layer],
                                      state_rwkv[layer], caches, lw)
        prompt_states.append(st_p)
        sample_states.append(st_s)
    ps = [jnp.stack(t) for t in zip(*prompt_states)]
    ss = [jnp.stack(t) for t in zip(*sample_states)]
    return (h_p.reshape(batch, seq, d_model), h_s.reshape(dec_batch, dec_seq, d_model),
            *ps, *ss)
```

```python
import functools

import jax
import jax.numpy as jnp
from jax import lax
from jax.experimental import pallas as pl
from jax.experimental.pallas import tpu as pltpu

F32 = jnp.float32
BF16 = jnp.bfloat16

HEAD_DIM = 64
HPG = 8
ATT_GROUPS = ((128, 1), (512, 4), (2048, 16))
N_ATT_HEADS = 24
RWKV_HEADS = 8
LORA = 64
QBLOCK = 128
RMS_EPS = 1e-6
GN_EPS = 64e-5
L2_EPS = 1e-12
CHUNK = 64

LANES = 128
NEW_PAD = 16
VMEM_LIMIT = 52 * 1024 * 1024

QKV = N_ATT_HEADS * HEAD_DIM
ATT_W = HPG * HEAD_DIM
RW_W = RWKV_HEADS * HEAD_DIM
C_GATES = 0
C_Q = 3072
C_K = C_Q + QKV
C_V = C_K + QKV
C_SH = C_V + QKV
SHIFT_COLS = 3 * RW_W + 2 * LORA
IN_COLS = C_SH + SHIFT_COLS
TN = 512
P_COLS = pl.cdiv(IN_COLS, TN) * TN


def _cparams(sem):
    return pltpu.CompilerParams(dimension_semantics=sem, vmem_limit_bytes=VMEM_LIMIT)


def _segsum(x, bd):
    hi = x.astype(BF16)
    lo = (x - hi.astype(F32)).astype(BF16)
    return (jnp.dot(hi, bd, preferred_element_type=F32)
            + jnp.dot(lo, bd, preferred_element_type=F32))


def _bd_ones():
    i = jnp.arange(LANES)
    return (i[:, None] // HEAD_DIM == i[None, :] // HEAD_DIM).astype(BF16)


def _alibi_slopes(g):
    return jnp.exp2(-8.0 * (jnp.arange(HPG, dtype=F32) + (g * HPG + 1)) / N_ATT_HEADS)


def _inproj_kernel(x_ref, g_ref, w_ref, qg_ref, kg_ref, bd_ref, o_ref, xn_ref):
    j = pl.program_id(1)

    @pl.when(j == 0)
    def _():
        x = x_ref[...]
        ms = jnp.mean(x * x, axis=-1, keepdims=True)
        xn_ref[...] = (x * lax.rsqrt(ms + RMS_EPS) * g_ref[...]).astype(BF16)

    t = jnp.dot(xn_ref[...], w_ref[...], preferred_element_type=F32)
    q_lo, k_lo, k_hi = C_Q // TN, C_K // TN, C_V // TN

    @pl.when(jnp.logical_or(j < q_lo, j >= k_hi))
    def _():
        o_ref[...] = t

    @pl.when(jnp.logical_and(j >= q_lo, j < k_hi))
    def _():
        gain = jnp.where(j < k_lo, qg_ref[...], kg_ref[...])
        bd = bd_ref[...]
        for c in range(TN // LANES):
            tc = t[:, c * LANES:(c + 1) * LANES]
            ms = _segsum(tc * tc, bd) * (1.0 / HEAD_DIM)
            o_ref[:, c * LANES:(c + 1) * LANES] = tc * lax.rsqrt(ms + RMS_EPS) * gain


def _inproj(x, norm_g, w_perm, qg, kg, bd, tm):
    n, d = x.shape
    return pl.pallas_call(
        _inproj_kernel,
        grid=(n // tm, P_COLS // TN),
        in_specs=[
            pl.BlockSpec((tm, d), lambda i, j: (i, 0)),
            pl.BlockSpec((1, d), lambda i, j: (0, 0)),
            pl.BlockSpec((d, TN), lambda i, j: (0, j)),
            pl.BlockSpec((1, LANES), lambda i, j: (0, 0)),
            pl.BlockSpec((1, LANES), lambda i, j: (0, 0)),
            pl.BlockSpec((LANES, LANES), lambda i, j: (0, 0)),
        ],
        out_specs=pl.BlockSpec((tm, TN), lambda i, j: (i, j)),
        out_shape=jax.ShapeDtypeStruct((n, P_COLS), F32),
        scratch_shapes=[pltpu.VMEM((tm, d), BF16)],
        compiler_params=_cparams(("arbitrary", "arbitrary")),
        name="inproj",
    )(x, norm_g, w_perm, qg, kg, bd)


def _band_kernel(sl_ref, *refs, tile):
    n_g = len(ATT_GROUPS)
    ins, o_ref, scr = refs[0:5 * n_g], refs[5 * n_g], refs[5 * n_g + 1:]
    n = pl.program_id(1)
    p = pl.program_id(2)
    lane = lax.broadcasted_iota(jnp.int32, (1, LANES), 1)
    mask_a = lane < HEAD_DIM
    qi = lax.broadcasted_iota(jnp.int32, (QBLOCK, 2 * QBLOCK), 0)
    ki = lax.broadcasted_iota(jnp.int32, (QBLOCK, 2 * QBLOCK), 1)
    rel = qi + QBLOCK - ki
    band = jnp.logical_and(rel >= 0, rel <= QBLOCK)
    band_first = jnp.logical_and(band, jnp.logical_or(ki >= QBLOCK, n > 0))

    for g, (_, dil) in enumerate(ATT_GROUPS):
        q_ref, kp_ref, kc_ref, vp_ref, vc_ref = ins[5 * g:5 * g + 5]
        acc_scr, m_scr, l_scr = scr[3 * g:3 * g + 3]
        dist = (rel * dil).astype(F32)
        alibi = [-(sl_ref[g * HPG + 2 * p + h] * dist) for h in range(2)]
        bias = [jnp.where(band, a, -jnp.inf) for a in alibi]
        bias_first = [jnp.where(band_first, a, -jnp.inf) for a in alibi]

        def rows(r, s, dil=dil):
            if dil == 1:
                return pl.ds(s * QBLOCK, QBLOCK)
            return pl.ds(r + s * QBLOCK * dil, QBLOCK, stride=dil)

        for r in range(dil):
            for s in range(tile // (dil * QBLOCK)):
                blk_bias = bias_first if s == 0 else bias
                q = q_ref[rows(r, s), :] * (1.0 / 8.0)
                if s == 0:
                    k_prev, v_prev = kp_ref[rows(r, 0), :], vp_ref[rows(r, 0), :]
                else:
                    k_prev, v_prev = kc_ref[rows(r, s - 1), :], vc_ref[rows(r, s - 1), :]
                k = jnp.concatenate([k_prev, kc_ref[rows(r, s), :]], axis=0).astype(BF16)
                v = jnp.concatenate([v_prev, vc_ref[rows(r, s), :]], axis=0).astype(BF16)
                outs = []
                for h in range(2):
                    hm = mask_a if h == 0 else jnp.logical_not(mask_a)
                    qh = jnp.where(hm, q, 0.0).astype(BF16)
                    sc = lax.dot_general(qh, k, (((1,), (1,)), ((), ())), preferred_element_type=F32)
                    sc = sc + blk_bias[h]
                    m = jnp.max(sc, axis=-1, keepdims=True)
                    pr = jnp.exp(sc - m)
                    l = jnp.sum(pr, axis=-1, keepdims=True)
                    acc = jnp.dot(pr.astype(BF16), v, preferred_element_type=F32)
                    outs.append((acc, m, l))
                (acc0, m0, l0), (acc1, m1, l1) = outs
                acc_scr[rows(r, s), :] = jnp.where(mask_a, acc0, acc1)
                m_scr[rows(r, s), :] = jnp.where(mask_a, m0, m1)
                l_scr[rows(r, s), :] = jnp.where(mask_a, l0, l1)

    parts = [(scr[3 * g][...], scr[3 * g + 1][...], scr[3 * g + 2][...]) for g in range(n_g)]
    m_all = functools.reduce(jnp.maximum, [m for _, m, _ in parts])
    num = sum(jnp.exp(m - m_all) * a for a, m, _ in parts)
    den = sum(jnp.exp(m - m_all) * l for _, m, l in parts)
    o_ref[...] = num / den


def _band_attention(proj, batch, seq, tile):
    n_g = len(ATT_GROUPS)
    tile = min(tile, seq)
    assert seq % tile == 0
    nt = seq // tile
    in_specs = [pl.BlockSpec(memory_space=pltpu.SMEM)]
    for g, (window, dil) in enumerate(ATT_GROUPS):
        assert window // dil == QBLOCK
        halo = QBLOCK * dil
        assert tile % halo == 0
        nh = seq // halo
        cq, ck, cv = ((c + g * ATT_W) // LANES for c in (C_Q, C_K, C_V))

        def cur(col):
            return pl.BlockSpec((tile, LANES), lambda b, n, p: (b * nt + n, col + p))

        def prev(col, halo=halo, nh=nh):
            return pl.BlockSpec(
                (halo, LANES),
                lambda b, n, p: (b * nh + jnp.maximum(n * (tile // halo) - 1, 0), col + p))

        in_specs += [cur(cq), prev(ck), cur(ck), prev(cv), cur(cv)]
    slopes = jnp.concatenate([_alibi_slopes(g) for g in range(n_g)])
    return pl.pallas_call(
        functools.partial(_band_kernel, tile=tile),
        grid=(batch, nt, ATT_W // LANES),
        in_specs=in_specs,
        out_specs=pl.BlockSpec((tile, LANES), lambda b, n, p: (b * nt + n, p)),
        out_shape=jax.ShapeDtypeStruct((batch * seq, ATT_W), F32),
        scratch_shapes=[pltpu.VMEM((tile, LANES), F32)] * (3 * n_g),
        compiler_params=_cparams(("arbitrary", "arbitrary", "arbitrary")),
        name="band_attn",
    )(slopes, *([proj] * (5 * n_g)))


def _decode_body(sl_ref, new_refs, kb, vb, o_ref, ko, vo, hh, t_new, hps):
    n_g = len(ATT_GROUPS)
    q_new, k_new, v_new = ([ref[...] for ref in new_refs[i * n_g:(i + 1) * n_g]] for i in range(3))

    def head(x, j):
        return x[:, j * HEAD_DIM:(j + 1) * HEAD_DIM]

    ti_n = lax.broadcasted_iota(jnp.int32, (t_new, NEW_PAD), 0)
    tj_n = lax.broadcasted_iota(jnp.int32, (t_new, NEW_PAD), 1)
    pi = lax.broadcasted_iota(jnp.int32, (NEW_PAD, LANES), 0)
    pj = lax.broadcasted_iota(jnp.int32, (NEW_PAD, LANES), 1)
    place = jnp.logical_and(pj == pi + (LANES - t_new), pi < t_new).astype(BF16)
    tail = lax.broadcasted_iota(jnp.int32, (HEAD_DIM, LANES), 1) >= LANES - t_new
    nt = (((1,), (1,)), ((), ()))
    tn = (((0,), (0,)), ((), ()))
    zpad = jnp.zeros((NEW_PAD - t_new, HEAD_DIM), F32)

    combos = [(j, g) for j in range(hps) for g in range(n_g)]
    d_new = ti_n - tj_n
    kn = {(j, g): jnp.concatenate([head(k_new[g], j), zpad], axis=0) for j, g in combos}
    vn = {(j, g): jnp.concatenate([head(v_new[g], j), zpad], axis=0) for j, g in combos}
    q = {(j, g): head(q_new[g], j).astype(BF16) for j, g in combos}
    s_buf = {(j, g): jnp.dot(q[j, g], kb[g][j].astype(BF16), preferred_element_type=F32)
             for j, g in combos}
    s_new = {c: lax.dot_general(q[c], kn[c].astype(BF16), nt, preferred_element_type=F32)
             for c in combos}
    p_buf, p_new, stats = {}, {}, {}
    for j, g in combos:
        wb, dil = ATT_GROUPS[g]
        slope = sl_ref[g * HPG + hh * hps + j]
        wi = lax.broadcasted_iota(jnp.int32, (t_new, wb), 1)
        ti = lax.broadcasted_iota(jnp.int32, (t_new, wb), 0)
        d_buf = wb + ti - wi
        ok_buf = wi >= ti
        ok_new = d_new >= 0
        if dil > 1:
            ok_buf = jnp.logical_and(ok_buf, (d_buf & (dil - 1)) == 0)
            ok_new = jnp.logical_and(ok_new, (d_new & (dil - 1)) == 0)
        sb = jnp.where(ok_buf, s_buf[j, g] * (1.0 / 8.0) - slope * d_buf.astype(F32), -jnp.inf)
        sn = jnp.where(ok_new, s_new[j, g] * (1.0 / 8.0) - slope * d_new.astype(F32), -jnp.inf)
        m = jnp.maximum(jnp.max(sb, axis=-1, keepdims=True), jnp.max(sn, axis=-1, keepdims=True))
        pb, pn = jnp.exp(sb - m), jnp.exp(sn - m)
        p_buf[j, g], p_new[j, g] = pb.astype(BF16), pn.astype(BF16)
        stats[j, g] = (m, jnp.sum(pb, axis=-1, keepdims=True) + jnp.sum(pn, axis=-1, keepdims=True))
    acc = {(j, g): lax.dot_general(p_buf[j, g], vb[g][j].astype(BF16), nt, preferred_element_type=F32)
           + jnp.dot(p_new[j, g], vn[j, g].astype(BF16), preferred_element_type=F32)
           for j, g in combos}
    merged = []
    for j in range(hps):
        m_all = functools.reduce(jnp.maximum, [stats[j, g][0] for g in range(n_g)])
        wts = [jnp.exp(stats[j, g][0] - m_all) for g in range(n_g)]
        num = sum(wts[g] * acc[j, g] for g in range(n_g))
        den = sum(wts[g] * stats[j, g][1] for g in range(n_g))
        merged.append(num / den)
    o_ref[...] = jnp.concatenate(merged, axis=1)

    for j, g in combos:
        wb = ATT_GROUPS[g][0]
        for src, new, dst in ((kb[g], kn[j, g], ko[g]), (vb[g], vn[j, g], vo[g])):
            rolled = pltpu.roll(src[j], wb - t_new, 1)
            hi = new.astype(BF16)
            lo = (new - hi.astype(F32)).astype(BF16)
            placed = (lax.dot_general(hi, place, tn, preferred_element_type=F32)
                      + lax.dot_general(lo, place, tn, preferred_element_type=F32))
            if wb > LANES:
                dst[j, :, 0:wb - LANES] = rolled[:, 0:wb - LANES]
            dst[j, :, wb - LANES:wb] = jnp.where(tail, placed, rolled[:, wb - LANES:wb])


def _decode_kernel(sl_ref, *refs, t_new, hps):
    n_new = 3 * len(ATT_GROUPS)
    n_c = 2 * len(ATT_GROUPS)
    new_refs, refs = refs[:n_new], refs[n_new:]
    _decode_body(sl_ref, new_refs, refs[0:n_c:2], refs[1:n_c:2], refs[n_c],
                 refs[n_c + 1::2], refs[n_c + 2::2], pl.program_id(1), t_new, hps)


def _decode_scan_kernel(sl_ref, *refs, t_new, hps, nb):
    n_new = 3 * len(ATT_GROUPS)
    n_c = 2 * len(ATT_GROUPS)
    new_refs, refs = refs[:n_new], refs[n_new:]
    caches_in = refs[0:n_c]
    cur_ref, wa_ref, pcur_ref, pwa_ref = refs[n_c:n_c + 4]
    param_refs = refs[n_c + 4:n_c + 13]
    o_ref = refs[n_c + 13]
    caches_out = refs[n_c + 14:2 * n_c + 14]
    y_ref, zt_ref, bonus_ref, z_scr, tinv_scr, lp_scr, akv_scr, rbk_scr, tok_scr = refs[2 * n_c + 14:]
    c, half = pl.program_id(0), pl.program_id(1)
    n = nb * (RW_W // LANES)

    def decode(hh):
        _decode_body(sl_ref, new_refs, caches_in[0::2], caches_in[1::2], o_ref,
                     caches_out[0::2], caches_out[1::2], hh, t_new, hps)

    @pl.when(jnp.logical_and(c == 0, half == 0))
    def _():
        z_scr[...] = jnp.zeros_like(z_scr)

    @pl.when(half == 0)
    def _():
        decode(0)
        params = tuple(ref[...] for ref in param_refs)
        row0 = lax.broadcasted_iota(jnp.int32, (CHUNK, 1), 0) == 0
        seqs = []
        for bi in range(nb):
            def prev(x, p8):
                edge = jnp.where(c == 0, 0.0, p8[7:8, :])
                return jnp.where(row0, edge, pltpu.roll(x, 1, 0))
            cur, wa = cur_ref[bi], wa_ref[bi]
            outs = _prep_math(cur, wa, prev(cur, pcur_ref[bi]), prev(wa, pwa_ref[bi]), *params)
            seqs.append(outs[:6])
            bonus_ref[bi] = outs[6]
            for j in range(6):
                tok_scr[j, bi] = outs[j]
        tinv, lp, akv, rbk = _scan_first_half(_scan_chains(seqs))
        tinv_scr[...] = jnp.stack(tinv, axis=0)
        lp_scr[...] = jnp.stack(lp, axis=0)
        akv_scr[...] = jnp.stack(akv, axis=0)
        rbk_scr[...] = jnp.stack(rbk, axis=0)

    @pl.when(half == 1)
    def _():
        decode(1)
        seqs = [tuple(tok_scr[j, bi] for j in range(6)) for bi in range(nb)]
        carried = [[scr[i] for i in range(n)] for scr in (tinv_scr, lp_scr, akv_scr, rbk_scr)]
        ys, z_news = _scan_second_half(_scan_chains(seqs), *carried, [z_scr[i] for i in range(n)])
        _store_scan(ys, z_news, y_ref, zt_ref, z_scr, nb)


def _unpair_state(zt, batch):
    zt = zt.reshape(batch, RW_W // LANES, LANES, LANES)
    s_a = zt[:, :, 0:HEAD_DIM, 0:HEAD_DIM]
    s_b = zt[:, :, HEAD_DIM:, HEAD_DIM:]
    return jnp.stack([s_a, s_b], axis=2).reshape(batch, RWKV_HEADS, HEAD_DIM, HEAD_DIM)


def _decode_attention(proj, caches, dec_batch, t_new, scan_src=None, batch=0, seq=0):
    n_g = len(ATT_GROUPS)
    hps = HPG // 2
    assert all(k.shape[1] == w for (k, _), (w, _) in zip(caches, ATT_GROUPS))

    native = [jnp.transpose(c, (0, 2, 3, 1)) for kv in caches for c in kv]
    slopes = jnp.concatenate([_alibi_slopes(g) for g in range(n_g)])
    proj3 = proj.reshape(dec_batch, t_new, P_COLS)
    hw = hps * HEAD_DIM

    def new_spec(col0):
        return pl.BlockSpec((None, t_new, hw), lambda b, h: (b, 0, col0 // hw + h))

    new_specs = [new_spec(c0 + g * ATT_W) for c0 in (C_Q, C_K, C_V) for g in range(n_g)]
    cache_specs = [pl.BlockSpec((None, hps, HEAD_DIM, c.shape[-1]), lambda b, h: (b, h, 0, 0))
                   for c in native]
    in_specs = [pl.BlockSpec(memory_space=pltpu.SMEM)] + new_specs + cache_specs
    out_specs = [pl.BlockSpec((None, t_new, hw), lambda b, h: (b, 0, h))] + cache_specs
    out_shape = ([jax.ShapeDtypeStruct((dec_batch, t_new, ATT_W), F32)]
                 + [jax.ShapeDtypeStruct(c.shape, F32) for c in native])
    new_args = [proj3] * len(new_specs)
    grid = (dec_batch, HPG // hps)
    if scan_src is None:
        outs = pl.pallas_call(
            functools.partial(_decode_kernel, t_new=t_new, hps=hps),
            grid=grid, in_specs=in_specs, out_specs=out_specs, out_shape=out_shape,
            compiler_params=_cparams(("arbitrary", "arbitrary")),
            name="decode_attn",
        )(slopes, *new_args, *native)
        scan_out = ()
    else:
        assert grid == (seq // CHUNK, 2)
        proj_p, (mu, *prep_params) = scan_src
        pv = proj_p.reshape(batch, seq, P_COLS)
        n = batch * (RW_W // LANES)
        main_w = 3 * RW_W
        cb, wb = C_SH // main_w, (C_SH + main_w) // LANES

        def halo(c, h):
            return jnp.maximum(c * (CHUNK // 8) - 1, 0)

        tok = pl.BlockSpec((batch, CHUNK, RW_W), lambda c, h: (0, c, 0))
        st = pl.BlockSpec((n, LANES, LANES), lambda c, h: (0, 0, 0))
        full = lambda a: pl.BlockSpec(a.shape, lambda c, h: (0, 0))
        params = (mu[:, 0:main_w], mu[:, main_w:], *prep_params)
        prep_specs = [pl.BlockSpec((batch, CHUNK, main_w), lambda c, h: (0, c, cb)),
                      pl.BlockSpec((batch, CHUNK, LANES), lambda c, h: (0, c, wb)),
                      pl.BlockSpec((batch, 8, main_w), lambda c, h: (0, halo(c, h), cb)),
                      pl.BlockSpec((batch, 8, LANES), lambda c, h: (0, halo(c, h), wb))]
        sq = pltpu.VMEM((n, LANES, LANES), F32)
        tok_sds = jax.ShapeDtypeStruct((batch, seq, RW_W), F32)
        outs = pl.pallas_call(
            functools.partial(_decode_scan_kernel, t_new=t_new, hps=hps, nb=batch),
            grid=grid,
            in_specs=in_specs + prep_specs + [full(a) for a in params],
            out_specs=out_specs + [tok, st, tok],
            out_shape=out_shape + [tok_sds, jax.ShapeDtypeStruct((n, LANES, LANES), F32), tok_sds],
            scratch_shapes=[sq, sq, sq, sq, pltpu.VMEM((n, LANES, 2 * LANES), F32),
                            pltpu.VMEM((6, batch, CHUNK, RW_W), F32)],
            compiler_params=_cparams(("arbitrary", "arbitrary")),
            name="decode_attn_rwkv_scan",
        )(slopes, *new_args, *native, pv, pv, pv, pv, *params)
        flat = (batch * seq, RW_W)
        scan_out = (outs[-3].reshape(flat), _unpair_state(outs[-2], batch), outs[-1].reshape(flat))
        outs = outs[:-3]
    o_a = outs[0].reshape(dec_batch * t_new, ATT_W)
    new = [jnp.transpose(c, (0, 3, 1, 2)) for c in outs[1:]]
    return (o_a, new[0::2], new[1::2]) + scan_out


def _prep_kernel(cur_ref, wa_ref, pcur_ref, pwa_ref, fcur_ref, fwa_ref, mu_ref, muwa_ref, lora_ref,
                 w0_ref, a0_ref, kk_ref, ka_ref, rk_ref, bd_ref,
                 r_o, lw_o, k_o, v_o, kkn_o, b_o, bonus_o, *, seq_len, tm):
    i = pl.program_id(0)
    row = lax.broadcasted_iota(jnp.int32, (tm, 1), 0)

    def shifted(cur, p8, first):
        rolled = pltpu.roll(cur, 1, 0)
        if seq_len >= tm:
            at_start = (i * tm) % seq_len == 0
            edge = jnp.where(at_start, first, p8[7:8, :])
            return jnp.where(row == 0, edge, rolled)
        return jnp.where(row % seq_len == 0, first, rolled)

    cur = cur_ref[...]
    wa = wa_ref[...]
    params = tuple(ref[...] for ref in (mu_ref, muwa_ref, lora_ref, w0_ref, a0_ref, kk_ref, ka_ref, rk_ref,
                                        bd_ref))
    outs = _prep_math(cur, wa, shifted(cur, pcur_ref[...], fcur_ref[...]),
                      shifted(wa, pwa_ref[...], fwa_ref[...]), *params)
    for ref, val in zip((r_o, lw_o, k_o, v_o, kkn_o, b_o, bonus_o), outs):
        ref[...] = val


def _prep_math(cur, wa, prev_cur, prev_wa, mu, muwa, lora, w0, a0, k_k, k_a, r_k, bd):
    xs = cur + (prev_cur - cur) * mu
    xwa = wa + (prev_wa - wa) * muwa
    r = xs[:, 0:RW_W]
    kr = xs[:, RW_W:2 * RW_W]
    vr = xs[:, 2 * RW_W:3 * RW_W]
    lane = lax.broadcasted_iota(jnp.int32, (1, LANES), 1)
    lin = jnp.where(lane < LORA, jnp.tanh(xwa), xwa).astype(BF16)
    lo = jnp.dot(lin, lora, preferred_element_type=F32)
    z = -(w0 + lo[:, 0:RW_W])
    softplus = jnp.maximum(z, 0.0) + jnp.log1p(jnp.exp(-jnp.abs(z)))
    lw = -jnp.exp(-softplus - 0.5)
    a = jax.nn.sigmoid(a0 + lo[:, RW_W:2 * RW_W])
    kk = kr * k_k
    k_mod = kr * (1.0 + (a - 1.0) * k_a)
    rkk = r * k_mod * r_k
    kkn, b, bonus = [], [], []
    for c in range(RW_W // LANES):
        cols = slice(c * LANES, (c + 1) * LANES)
        kc = kk[:, cols]
        kn = kc / jnp.maximum(jnp.sqrt(_segsum(kc * kc, bd)), L2_EPS)
        kkn.append(kn)
        b.append(kn * a[:, cols])
        bonus.append(_segsum(rkk[:, cols], bd) * vr[:, cols])
    cat = lambda parts: jnp.concatenate(parts, axis=1)
    return r, lw, k_mod, vr, cat(kkn), cat(b), cat(bonus)


def _rwkv_prep(proj, first, mu, lora_w, w0, a0, k_k, k_a, r_k, bd, seq_len, tm):
    n = proj.shape[0]
    main_w = 3 * RW_W
    cb, wb = C_SH // main_w, (C_SH + main_w) // LANES
    per_seq_first = seq_len >= tm
    if per_seq_first:
        f_main = first[:, None, 0:main_w]
        f_wa = first[:, None, main_w:]
        fm_spec = pl.BlockSpec((None, 1, main_w), lambda i: ((i * tm) // seq_len, 0, 0))
        fw_spec = pl.BlockSpec((None, 1, LANES), lambda i: ((i * tm) // seq_len, 0, 0))
    else:
        rep = jnp.repeat(first, seq_len, axis=0)
        f_main, f_wa = rep[:, 0:main_w], rep[:, main_w:]
        fm_spec = pl.BlockSpec((tm, main_w), lambda i: (i, 0))
        fw_spec = pl.BlockSpec((tm, LANES), lambda i: (i, 0))
    vec = lambda w: pl.BlockSpec((1, w), lambda i: (0, 0))
    out_spec = pl.BlockSpec((tm, RW_W), lambda i: (i, 0))
    out_sds = jax.ShapeDtypeStruct((n, RW_W), F32)
    return pl.pallas_call(
        functools.partial(_prep_kernel, seq_len=seq_len, tm=tm),
        grid=(n // tm,),
        in_specs=[
            pl.BlockSpec((tm, main_w), lambda i: (i, cb)),
            pl.BlockSpec((tm, LANES), lambda i: (i, wb)),
            pl.BlockSpec((8, main_w), lambda i: (jnp.maximum(i * (tm // 8) - 1, 0), cb)),
            pl.BlockSpec((8, LANES), lambda i: (jnp.maximum(i * (tm // 8) - 1, 0), wb)),
            fm_spec, fw_spec,
            vec(main_w), vec(LANES),
            pl.BlockSpec((LANES, 2 * RW_W), lambda i: (0, 0)),
            vec(RW_W), vec(RW_W), vec(RW_W), vec(RW_W), vec(RW_W),
            pl.BlockSpec((LANES, LANES), lambda i: (0, 0)),
        ],
        out_specs=[out_spec] * 7,
        out_shape=[out_sds] * 7,
        compiler_params=_cparams(("arbitrary",)),
        name="rwkv_prep",
    )(proj, proj, proj, proj, f_main, f_wa, mu[:, 0:main_w], mu[:, main_w:], lora_w,
      w0, a0, k_k, k_a, r_k, bd)


def _mm(a, b):
    return jnp.dot(a.astype(BF16), b.astype(BF16), preferred_element_type=F32)


def _mm_nt(a, b):
    return lax.dot_general(a.astype(BF16), b.astype(BF16), (((1,), (1,)), ((), ())),
                           preferred_element_type=F32)


def _mm_tn(a, b):
    return lax.dot_general(a.astype(BF16), b.astype(BF16), (((0,), (0,)), ((), ())),
                           preferred_element_type=F32)


SPLIT_STEPS = 2


def _scan_chains(seqs):
    C = CHUNK
    ti = lax.broadcasted_iota(jnp.int32, (C, C), 0)
    tj = lax.broadcasted_iota(jnp.int32, (C, C), 1)
    tri = (ti >= tj).astype(F32)
    in_a = lax.broadcasted_iota(jnp.int32, (1, LANES), 1) < HEAD_DIM

    def stack(x):
        return jnp.concatenate([jnp.where(in_a, x, 0.0), jnp.where(in_a, 0.0, x)], axis=0)

    chains = []
    for r, lw, k, v, kk, b in seqs:
        cum = jnp.dot(tri, lw, precision=lax.Precision.HIGHEST, preferred_element_type=F32)
        cum_end = cum[C - 1:C, :]
        w_in = jnp.exp(-cum)
        w_rest = jnp.exp(cum_end - cum)
        vals = (("a", jnp.exp(cum - lw) * (-kk)), ("b", b * w_in), ("k", k * w_in), ("r", r * jnp.exp(cum)),
                ("v", v), ("bh", b * w_rest), ("kh", k * w_rest))
        w_end = jnp.exp(cum_end)
        for p in range(RW_W // LANES):
            cols = slice(p * LANES, (p + 1) * LANES)
            ch = {name: stack(x[:, cols]) for name, x in vals}
            ch["w_end"] = w_end[:, cols]
            chains.append(ch)
    return chains


def _scan_masks():
    C = CHUNK
    si = lax.broadcasted_iota(jnp.int32, (2 * C, 2 * C), 0)
    sj = lax.broadcasted_iota(jnp.int32, (2 * C, 2 * C), 1)
    same = (si // C) == (sj // C)
    strict = jnp.logical_and(same, (si % C) > (sj % C))
    incl = jnp.logical_and(same, (si % C) >= (sj % C))
    return strict, incl, (si == sj).astype(F32)


def _double(prod, lp, steps):
    n = 2 * CHUNK
    for _ in range(steps):
        res = [_mm(jnp.concatenate([a, x], axis=0), x) for a, x in zip(prod, lp)]
        prod = [a + r[0:n] for a, r in zip(prod, res)]
        lp = [r[n:2 * n] for r in res]
    return prod, lp


def _scan_first_half(chains):
    C = CHUNK
    strict, incl, eye = _scan_masks()
    g = [_mm_nt(jnp.concatenate([ch["a"], ch["r"]], axis=0), jnp.concatenate([ch["b"], ch["k"]], axis=0))
         for ch in chains]
    ab = [jnp.where(strict, x[0:2 * C, 0:2 * C], 0.0) for x in g]
    ak = [jnp.where(strict, x[0:2 * C, 2 * C:4 * C], 0.0) for x in g]
    rbk = [jnp.concatenate([jnp.where(incl, x[2 * C:4 * C, 0:2 * C], 0.0),
                            jnp.where(incl, x[2 * C:4 * C, 2 * C:4 * C], 0.0)], axis=1) for x in g]
    akv = [_mm(x, ch["v"]) for x, ch in zip(ak, chains)]
    tinv, lp = _double([eye + x for x in ab], [_mm(x, x) for x in ab], SPLIT_STEPS)
    return tinv, lp, akv, rbk


def _scan_second_half(chains, tinv, lp, akv, rbk, zs):
    C = CHUNK
    eye = _scan_masks()[2]
    n = len(chains)
    levels = (C - 1).bit_length()
    tinv, lp = _double(tinv, lp, levels - 2 - SPLIT_STEPS)
    tinv = [a + _mm(a, x) for a, x in zip(tinv, lp)]
    au = [_mm(tinv[i], jnp.concatenate([chains[i]["a"], akv[i]], axis=1)) for i in range(n)]
    zeros = jnp.zeros((2 * C, LANES), F32)
    rhs = [jnp.concatenate([au[i], jnp.concatenate([zeros, chains[i]["v"]], axis=1)], axis=0)
           for i in range(n)]
    mn = [_mm_tn(jnp.concatenate([chains[i]["bh"], chains[i]["kh"]], axis=0), rhs[i])
          for i in range(n)]
    ry = [_mm(rbk[i], rhs[i]) for i in range(n)]
    yz = [_mm(jnp.concatenate([chains[i]["r"] + ry[i][:, 0:LANES],
                               mn[i][:, 0:LANES] + eye * chains[i]["w_end"]], axis=0), zs[i])
          for i in range(n)]
    z_news = [yz[i][2 * C:2 * C + LANES] + mn[i][:, LANES:2 * LANES] for i in range(n)]
    y_st = [yz[i][0:2 * C] + ry[i][:, LANES:2 * LANES] for i in range(n)]
    return [y[0:C] + y[C:2 * C] for y in y_st], z_news


def _store_scan(ys, z_news, y_ref, zt_ref, z_scr, nb):
    n_pairs = RW_W // LANES
    y_ref[...] = jnp.stack([jnp.concatenate(ys[bi * n_pairs:(bi + 1) * n_pairs], axis=1)
                            for bi in range(nb)], axis=0)
    z_scr[...] = jnp.stack(z_news, axis=0)
    zt_ref[...] = jnp.stack([z.T for z in z_news], axis=0)


def _scan_kernel(r_ref, lw_ref, k_ref, v_ref, kk_ref, b_ref, y_ref, zt_ref, z_scr, *, nb):
    @pl.when(pl.program_id(0) == 0)
    def _():
        z_scr[...] = jnp.zeros_like(z_scr)

    chains = _scan_chains([tuple(ref[bi] for ref in (r_ref, lw_ref, k_ref, v_ref, kk_ref, b_ref))
                           for bi in range(nb)])
    ys, z_news = _scan_second_half(chains, *_scan_first_half(chains),
                                   [z_scr[i] for i in range(len(chains))])
    _store_scan(ys, z_news, y_ref, zt_ref, z_scr, nb)


def _rwkv_scan(prep, batch, seq):
    assert seq % CHUNK == 0
    n_pairs = RW_W // LANES
    tok = pl.BlockSpec((batch, CHUNK, RW_W), lambda c: (0, c, 0))
    st = pl.BlockSpec((batch * n_pairs, LANES, LANES), lambda c: (0, 0, 0))
    y, zt = pl.pallas_call(
        functools.partial(_scan_kernel, nb=batch),
        grid=(seq // CHUNK,),
        in_specs=[tok] * 6,
        out_specs=[tok, st],
        out_shape=[jax.ShapeDtypeStruct((batch, seq, RW_W), F32),
                   jax.ShapeDtypeStruct((batch * n_pairs, LANES, LANES), F32)],
        scratch_shapes=[pltpu.VMEM((batch * n_pairs, LANES, LANES), F32)],
        compiler_params=_cparams(("arbitrary",)),
        name="rwkv_scan",
    )(*[x.reshape(batch, seq, RW_W) for x in prep])
    return y.reshape(batch * seq, RW_W), _unpair_state(zt, batch)


def _seq_scan_kernel(r_ref, lw_ref, k_ref, v_ref, kk_ref, b_ref, s_ref, y_ref, so_ref, t_scr, y_scr,
                     *, t_new, nb):
    for j, ref in enumerate((r_ref, lw_ref, k_ref, v_ref, kk_ref, b_ref)):
        for t in range(t_new):
            x = ref[pl.ds(t, nb, stride=t_new), :].T
            t_scr[j, t] = jnp.exp(x) if j == 1 else x

    for hd in range(LANES // HEAD_DIM):
        lo = hd * HEAD_DIM
        rows = pl.ds(lo, HEAD_DIM)

        def body(vi, carry, hd=hd, lo=lo, rows=rows):
            sv = s_ref[hd, vi]
            for t in range(t_new):
                sa = jnp.sum(sv * t_scr[4, t, rows, :], axis=0, keepdims=True)
                sv = (sv * t_scr[1, t, rows, :] - sa * t_scr[5, t, rows, :]
                      + t_scr[3, t, pl.ds(lo + vi, 1), :] * t_scr[2, t, rows, :])
                y_scr[t, pl.ds(lo + vi, 1), :] = jnp.sum(sv * t_scr[0, t, rows, :], axis=0, keepdims=True)
            so_ref[hd, vi] = sv
            return carry

        lax.fori_loop(0, HEAD_DIM, body, 0)

    for t in range(t_new):
        y_ref[pl.ds(t, nb, stride=t_new), :] = y_scr[t].T


def _seq_scan(prep, state, dec_batch, t_new):
    hpp = LANES // HEAD_DIM
    rows = dec_batch * t_new
    vec = pl.BlockSpec((rows, LANES), lambda p: (0, p))
    st = pl.BlockSpec((hpp, HEAD_DIM, HEAD_DIM, dec_batch), lambda p: (p, 0, 0, 0))
    s_native = jnp.transpose(state, (1, 2, 3, 0))
    y, s_new = pl.pallas_call(
        functools.partial(_seq_scan_kernel, t_new=t_new, nb=dec_batch),
        grid=(RWKV_HEADS // hpp,),
        in_specs=[vec] * 6 + [st],
        out_specs=[vec, st],
        out_shape=[jax.ShapeDtypeStruct((rows, RW_W), F32), jax.ShapeDtypeStruct(s_native.shape, F32)],
        scratch_shapes=[pltpu.VMEM((6, t_new, LANES, dec_batch), F32),
                        pltpu.VMEM((t_new, LANES, dec_batch), F32)],
        compiler_params=_cparams(("arbitrary",)),
        name="rwkv_seq_scan",
    )(*prep, s_native)
    return y, jnp.transpose(s_new, (3, 0, 1, 2))


def _final_kernel(oa_ref, y_ref, bonus_ref, gates_ref, x_ref, lnw_ref, lnb_ref, pa_ref, pb_ref, wo_ref, bd_ref,
                  o_ref):
    o_a = oa_ref[...]
    gates = gates_ref[...]
    z_a = gates[:, 0:ATT_W]
    z_b = gates[:, ATT_W:ATT_W + RW_W]
    g_a = gates[:, ATT_W + RW_W:ATT_W + RW_W + 1024]
    g_b = gates[:, ATT_W + RW_W + 1024:]
    y = y_ref[...]
    bd = bd_ref[...]
    cols_out = []
    for c in range(RW_W // LANES):
        cols = slice(c * LANES, (c + 1) * LANES)
        yc = y[:, cols]
        mu = _segsum(yc, bd) * (1.0 / HEAD_DIM)
        d = yc - mu
        var = _segsum(d * d, bd) * (1.0 / HEAD_DIM)
        cols_out.append(d * lax.rsqrt(var + GN_EPS))
    yn = jnp.concatenate(cols_out, axis=1)
    o_b = yn * lnw_ref[...] + lnb_ref[...] + bonus_ref[...]
    h_a = jnp.dot((o_a * jax.nn.silu(z_a)).astype(BF16), pa_ref[...], preferred_element_type=F32)
    h_b = jnp.dot((o_b * jax.nn.silu(z_b)).astype(BF16), pb_ref[...], preferred_element_type=F32)
    mixed = jax.nn.sigmoid(g_a) * h_a + jax.nn.sigmoid(g_b) * h_b
    o_ref[...] = x_ref[...] + jnp.dot(mixed.astype(BF16), wo_ref[...], preferred_element_type=F32)


def _final(o_a, y_raw, bonus, proj, x, ln_w, ln_b, p_a, p_b, w_out, bd, tm):
    n, d = x.shape
    tm = min(tm, n)
    gates_w = C_Q - C_GATES
    row = lambda w: pl.BlockSpec((tm, w), lambda i: (i, 0))
    full = lambda a: pl.BlockSpec(a.shape, lambda i: (0, 0))
    return pl.pallas_call(
        _final_kernel,
        grid=(n // tm,),
        in_specs=[row(ATT_W), row(RW_W), row(RW_W), row(gates_w), row(d),
                  full(ln_w), full(ln_b), full(p_a), full(p_b), full(w_out), full(bd)],
        out_specs=row(d),
        out_shape=jax.ShapeDtypeStruct((n, d), F32),
        compiler_params=_cparams(("arbitrary",)),
        name="gated_out",
    )(o_a, y_raw, bonus, proj, x, ln_w, ln_b, p_a, p_b, w_out, bd)


def _permute_w_in(w_in):
    q, k, v, sh, gates = (w_in[:, 0:QKV], w_in[:, QKV:2 * QKV], w_in[:, 2 * QKV:3 * QKV],
                          w_in[:, 3 * QKV:3 * QKV + SHIFT_COLS], w_in[:, 3 * QKV + SHIFT_COLS:])
    pad = jnp.zeros((w_in.shape[0], P_COLS - IN_COLS), w_in.dtype)
    return jnp.concatenate([gates, q, k, v, sh, pad], axis=1).astype(BF16)


def _layer(h_p, h_s, batch, seq, dec_batch, dec_seq, shift_s, state_s, caches, lw):
    (norm_g, w_perm, shift_mu, qg, kg, lora_w, w0, a0, k_k, k_a, r_k, ln_w, ln_b, p_a, p_b, w_out,
     bd) = lw
    n_s = dec_batch * dec_seq
    prep_args = (shift_mu, lora_w, w0, a0, k_k, k_a, r_k, bd)
    out_args = (ln_w, ln_b, p_a, p_b, w_out, bd)

    proj_p = _inproj(h_p, norm_g, w_perm, qg, kg, bd, tm=2048)
    o_a_p = _band_attention(proj_p, batch, seq, tile=2048)
    pv = proj_p.reshape(batch, seq, P_COLS)
    st_p = []
    for g, (window, _) in enumerate(ATT_GROUPS):
        keep = min(window, seq)
        for c0 in (C_K, C_V):
            st_p.append(pv[:, seq - keep:, c0 + g * ATT_W:c0 + (g + 1) * ATT_W]
                        .reshape(batch, keep, HPG, HEAD_DIM))

    proj_s = _inproj(h_s, norm_g, w_perm, qg, kg, bd, tm=min(1024, n_s))
    prep_s = _rwkv_prep(proj_s, shift_s, *prep_args, seq_len=dec_seq, tm=min(512, n_s))
    if seq % CHUNK == 0 and dec_batch == seq // CHUNK:
        o_a, new_k, new_v, y_p, state_p, bonus_p = _decode_attention(
            proj_s, caches, dec_batch, dec_seq, scan_src=(proj_p, prep_args), batch=batch, seq=seq)
    else:
        o_a, new_k, new_v = _decode_attention(proj_s, caches, dec_batch, dec_seq)
        prep_p = _rwkv_prep(proj_p, jnp.zeros((batch, SHIFT_COLS), F32), *prep_args, seq_len=seq, tm=512)
        y_p, state_p = _rwkv_scan(prep_p[:6], batch, seq)
        bonus_p = prep_p[6]
    y_s, state_s_new = _seq_scan(prep_s[:6], state_s, dec_batch, dec_seq)

    out_p = _final(o_a_p, y_p, bonus_p, proj_p, h_p, *out_args, tm=512)
    out_s = _final(o_a, y_s, prep_s[6], proj_s, h_s, *out_args, tm=512)
    st_p += [state_p, pv[:, seq - 1, C_SH:C_SH + SHIFT_COLS]]
    st_s = [c for kv in zip(new_k, new_v) for c in kv]
    st_s += [state_s_new, proj_s.reshape(dec_batch, dec_seq, P_COLS)[:, dec_seq - 1, C_SH:C_SH + SHIFT_COLS]]
    return out_p, out_s, st_p, st_s


def kernel(x_prompt, x_sample, cache_k_g0, cache_v_g0, cache_k_g1, cache_v_g1, cache_k_g2, cache_v_g2, state_rwkv, state_shift, norm_g, w_in, shift_mu, q_norm_g, k_norm_g, w0, w2, a0, a2, k_k, k_a, r_k, ln_x_w, ln_x_b, p_a, p_b, w_out):
    depth = norm_g.shape[0]
    batch, seq, d_model = x_prompt.shape
    dec_batch, dec_seq, _ = x_sample.shape
    bd = _bd_ones()
    h_p = x_prompt.reshape(batch * seq, d_model)
    h_s = x_sample.reshape(dec_batch * dec_seq, d_model)
    prompt_states, sample_states = [], []
    for layer in range(depth):
        zero = jnp.zeros((LORA, RW_W), F32)
        lora_w = jnp.concatenate([jnp.concatenate([w2[layer], zero], axis=1),
                                  jnp.concatenate([zero, a2[layer]], axis=1)], axis=0).astype(BF16)
        row = lambda a: a.reshape(1, -1)
        lw = (row(norm_g[layer]), _permute_w_in(w_in[layer]), row(shift_mu[layer]),
              jnp.tile(row(q_norm_g[layer]), (1, LANES // HEAD_DIM)),
              jnp.tile(row(k_norm_g[layer]), (1, LANES // HEAD_DIM)),
              lora_w, row(w0[layer]), row(a0[layer]), row(k_k[layer]), row(k_a[layer]),
              row(r_k[layer]), row(ln_x_w[layer]), row(ln_x_b[layer]),
              p_a[layer].astype(BF16), p_b[layer].astype(BF16), w_out[layer].astype(BF16), bd)
        caches = [(cache_k_g0[layer], cache_v_g0[layer]), (cache_k_g1[layer], cache_v_g1[layer]),
                  (cache_k_g2[layer], cache_v_g2[layer])]
        h_p, h_s, st_p, st_s = _layer(h_p, h_s, batch, seq, dec_batch, dec_seq, state_shift[layer],
                                      state_rwkv[layer], caches, lw)
        prompt_states.append(st_p)
        sample_states.append(st_s)
    ps = [jnp.stack(t) for t in zip(*prompt_states)]
    ss = [jnp.stack(t) for t in zip(*sample_states)]
    return (h_p.reshape(batch, seq, d_model), h_s.reshape(dec_batch, dec_seq, d_model),
            *ps, *ss)
```

```python
import functools

import jax
import jax.numpy as jnp
from jax import lax
from jax.experimental import pallas as pl
from jax.experimental.pallas import tpu as pltpu

F32 = jnp.float32
BF16 = jnp.bfloat16

HEAD_DIM = 64
HPG = 8
ATT_GROUPS = ((128, 1), (512, 4), (2048, 16))
N_ATT_HEADS = 24
RWKV_HEADS = 8
LORA = 64
QBLOCK = 128
RMS_EPS = 1e-6
GN_EPS = 64e-5
L2_EPS = 1e-12
CHUNK = 64

LANES = 128
NEW_PAD = 16
VMEM_LIMIT = 52 * 1024 * 1024

QKV = N_ATT_HEADS * HEAD_DIM
ATT_W = HPG * HEAD_DIM
RW_W = RWKV_HEADS * HEAD_DIM
C_GATES = 0
C_Q = 3072
C_K = C_Q + QKV
C_V = C_K + QKV
C_SH = C_V + QKV
SHIFT_COLS = 3 * RW_W + 2 * LORA
IN_COLS = C_SH + SHIFT_COLS
TN = 512
P_COLS = pl.cdiv(IN_COLS, TN) * TN


def _cparams(sem):
    return pltpu.CompilerParams(dimension_semantics=sem, vmem_limit_bytes=VMEM_LIMIT)


def _segsum(x, bd):
    hi = x.astype(BF16)
    lo = (x - hi.astype(F32)).astype(BF16)
    return (jnp.dot(hi, bd, preferred_element_type=F32)
            + jnp.dot(lo, bd, preferred_element_type=F32))


def _bd_ones():
    i = jnp.arange(LANES)
    return (i[:, None] // HEAD_DIM == i[None, :] // HEAD_DIM).astype(BF16)


def _alibi_slopes(g):
    return jnp.exp2(-8.0 * (jnp.arange(HPG, dtype=F32) + (g * HPG + 1)) / N_ATT_HEADS)


def _inproj_kernel(x_ref, g_ref, w_ref, qg_ref, kg_ref, bd_ref, o_ref, xn_ref):
    j = pl.program_id(1)

    @pl.when(j == 0)
    def _():
        x = x_ref[...]
        ms = jnp.mean(x * x, axis=-1, keepdims=True)
        xn_ref[...] = (x * lax.rsqrt(ms + RMS_EPS) * g_ref[...]).astype(BF16)

    t = jnp.dot(xn_ref[...], w_ref[...], preferred_element_type=F32)
    q_lo, k_lo, k_hi = C_Q // TN, C_K // TN, C_V // TN

    @pl.when(jnp.logical_or(j < q_lo, j >= k_hi))
    def _():
        o_ref[...] = t

    @pl.when(jnp.logical_and(j >= q_lo, j < k_hi))
    def _():
        gain = jnp.where(j < k_lo, qg_ref[...], kg_ref[...])
        bd = bd_ref[...]
        for c in range(TN // LANES):
            tc = t[:, c * LANES:(c + 1) * LANES]
            ms = jnp.dot((tc * tc).astype(BF16), bd, preferred_element_type=F32) * (1.0 / HEAD_DIM)
            o_ref[:, c * LANES:(c + 1) * LANES] = tc * lax.rsqrt(ms + RMS_EPS) * gain


def _inproj(x, norm_g, w_perm, qg, kg, bd, tm):
    n, d = x.shape
    return pl.pallas_call(
        _inproj_kernel,
        grid=(n // tm, P_COLS // TN),
        in_specs=[
            pl.BlockSpec((tm, d), lambda i, j: (i, 0)),
            pl.BlockSpec((1, d), lambda i, j: (0, 0)),
            pl.BlockSpec((d, TN), lambda i, j: (0, j)),
            pl.BlockSpec((1, LANES), lambda i, j: (0, 0)),
            pl.BlockSpec((1, LANES), lambda i, j: (0, 0)),
            pl.BlockSpec((LANES, LANES), lambda i, j: (0, 0)),
        ],
        out_specs=pl.BlockSpec((tm, TN), lambda i, j: (i, j)),
        out_shape=jax.ShapeDtypeStruct((n, P_COLS), F32),
        scratch_shapes=[pltpu.VMEM((tm, d), BF16)],
        compiler_params=_cparams(("arbitrary", "arbitrary")),
        name="inproj",
    )(x, norm_g, w_perm, qg, kg, bd)


def _band_kernel(sl_ref, *refs, tile):
    n_g = len(ATT_GROUPS)
    ins, o_ref, scr = refs[0:5 * n_g], refs[5 * n_g], refs[5 * n_g + 1:]
    n = pl.program_id(1)
    p = pl.program_id(2)
    lane = lax.broadcasted_iota(jnp.int32, (1, LANES), 1)
    mask_a = lane < HEAD_DIM
    qi = lax.broadcasted_iota(jnp.int32, (QBLOCK, 2 * QBLOCK), 0)
    ki = lax.broadcasted_iota(jnp.int32, (QBLOCK, 2 * QBLOCK), 1)
    rel = qi + QBLOCK - ki
    band = jnp.logical_and(rel >= 0, rel <= QBLOCK)
    band_first = jnp.logical_and(band, jnp.logical_or(ki >= QBLOCK, n > 0))

    for g, (_, dil) in enumerate(ATT_GROUPS):
        q_ref, kp_ref, kc_ref, vp_ref, vc_ref = ins[5 * g:5 * g + 5]
        acc_scr, m_scr, l_scr = scr[3 * g:3 * g + 3]
        dist = (rel * dil).astype(F32)
        alibi = [-(sl_ref[g * HPG + 2 * p + h] * dist) for h in range(2)]
        bias = [jnp.where(band, a, -jnp.inf) for a in alibi]
        bias_first = [jnp.where(band_first, a, -jnp.inf) for a in alibi]

        def rows(r, s, dil=dil):
            if dil == 1:
                return pl.ds(s * QBLOCK, QBLOCK)
            return pl.ds(r + s * QBLOCK * dil, QBLOCK, stride=dil)

        for r in range(dil):
            for s in range(tile // (dil * QBLOCK)):
                blk_bias = bias_first if s == 0 else bias
                q = q_ref[rows(r, s), :] * (1.0 / 8.0)
                if s == 0:
                    k_prev, v_prev = kp_ref[rows(r, 0), :], vp_ref[rows(r, 0), :]
                else:
                    k_prev, v_prev = kc_ref[rows(r, s - 1), :], vc_ref[rows(r, s - 1), :]
                k = jnp.concatenate([k_prev, kc_ref[rows(r, s), :]], axis=0).astype(BF16)
                v = jnp.concatenate([v_prev, vc_ref[rows(r, s), :]], axis=0).astype(BF16)
                outs = []
                for h in range(2):
                    hm = mask_a if h == 0 else jnp.logical_not(mask_a)
                    qh = jnp.where(hm, q, 0.0).astype(BF16)
                    sc = lax.dot_general(qh, k, (((1,), (1,)), ((), ())), preferred_element_type=F32)
                    sc = sc + blk_bias[h]
                    m = jnp.max(sc, axis=-1, keepdims=True)
                    pr = jnp.exp(sc - m)
                    l = jnp.sum(pr, axis=-1, keepdims=True)
                    acc = jnp.dot(pr.astype(BF16), v, preferred_element_type=F32)
                    outs.append((acc, m, l))
                (acc0, m0, l0), (acc1, m1, l1) = outs
                acc_scr[rows(r, s), :] = jnp.where(mask_a, acc0, acc1)
                m_scr[rows(r, s), :] = jnp.where(mask_a, m0, m1)
                l_scr[rows(r, s), :] = jnp.where(mask_a, l0, l1)

    parts = [(scr[3 * g][...], scr[3 * g + 1][...], scr[3 * g + 2][...]) for g in range(n_g)]
    m_all = functools.reduce(jnp.maximum, [m for _, m, _ in parts])
    num = sum(jnp.exp(m - m_all) * a for a, m, _ in parts)
    den = sum(jnp.exp(m - m_all) * l for _, m, l in parts)
    o_ref[...] = num / den


def _band_attention(proj, batch, seq, tile):
    n_g = len(ATT_GROUPS)
    tile = min(tile, seq)
    assert seq % tile == 0
    nt = seq // tile
    in_specs = [pl.BlockSpec(memory_space=pltpu.SMEM)]
    for g, (window, dil) in enumerate(ATT_GROUPS):
        assert window // dil == QBLOCK
        halo = QBLOCK * dil
        assert tile % halo == 0
        nh = seq // halo
        cq, ck, cv = ((c + g * ATT_W) // LANES for c in (C_Q, C_K, C_V))

        def cur(col):
            return pl.BlockSpec((tile, LANES), lambda b, n, p: (b * nt + n, col + p))

        def prev(col, halo=halo, nh=nh):
            return pl.BlockSpec(
                (halo, LANES),
                lambda b, n, p: (b * nh + jnp.maximum(n * (tile // halo) - 1, 0), col + p))

        in_specs += [cur(cq), prev(ck), cur(ck), prev(cv), cur(cv)]
    slopes = jnp.concatenate([_alibi_slopes(g) for g in range(n_g)])
    return pl.pallas_call(
        functools.partial(_band_kernel, tile=tile),
        grid=(batch, nt, ATT_W // LANES),
        in_specs=in_specs,
        out_specs=pl.BlockSpec((tile, LANES), lambda b, n, p: (b * nt + n, p)),
        out_shape=jax.ShapeDtypeStruct((batch * seq, ATT_W), F32),
        scratch_shapes=[pltpu.VMEM((tile, LANES), F32)] * (3 * n_g),
        compiler_params=_cparams(("arbitrary", "arbitrary", "arbitrary")),
        name="band_attn",
    )(slopes, *([proj] * (5 * n_g)))


def _decode_body(sl_ref, new_refs, kb, vb, o_ref, ko, vo, hh, t_new, hps):
    n_g = len(ATT_GROUPS)
    q_new, k_new, v_new = ([ref[...] for ref in new_refs[i * n_g:(i + 1) * n_g]] for i in range(3))

    def head(x, j):
        return x[:, j * HEAD_DIM:(j + 1) * HEAD_DIM]

    ti_n = lax.broadcasted_iota(jnp.int32, (t_new, NEW_PAD), 0)
    tj_n = lax.broadcasted_iota(jnp.int32, (t_new, NEW_PAD), 1)
    pi = lax.broadcasted_iota(jnp.int32, (NEW_PAD, LANES), 0)
    pj = lax.broadcasted_iota(jnp.int32, (NEW_PAD, LANES), 1)
    place = jnp.logical_and(pj == pi + (LANES - t_new), pi < t_new).astype(BF16)
    tail = lax.broadcasted_iota(jnp.int32, (HEAD_DIM, LANES), 1) >= LANES - t_new
    nt = (((1,), (1,)), ((), ()))
    tn = (((0,), (0,)), ((), ()))
    zpad = jnp.zeros((NEW_PAD - t_new, HEAD_DIM), F32)

    combos = [(j, g) for j in range(hps) for g in range(n_g)]
    d_new = ti_n - tj_n
    kn = {(j, g): jnp.concatenate([head(k_new[g], j), zpad], axis=0) for j, g in combos}
    vn = {(j, g): jnp.concatenate([head(v_new[g], j), zpad], axis=0) for j, g in combos}
    q = {(j, g): head(q_new[g], j).astype(BF16) for j, g in combos}
    s_buf = {(j, g): jnp.dot(q[j, g], kb[g][j].astype(BF16), preferred_element_type=F32)
             for j, g in combos}
    s_new = {c: lax.dot_general(q[c], kn[c].astype(BF16), nt, preferred_element_type=F32)
             for c in combos}
    p_buf, p_new, stats = {}, {}, {}
    for j, g in combos:
        wb, dil = ATT_GROUPS[g]
        slope = sl_ref[g * HPG + hh * hps + j]
        wi = lax.broadcasted_iota(jnp.int32, (t_new, wb), 1)
        ti = lax.broadcasted_iota(jnp.int32, (t_new, wb), 0)
        d_buf = wb + ti - wi
        ok_buf = wi >= ti
        ok_new = d_new >= 0
        if dil > 1:
            ok_buf = jnp.logical_and(ok_buf, (d_buf & (dil - 1)) == 0)
            ok_new = jnp.logical_and(ok_new, (d_new & (dil - 1)) == 0)
        sb = jnp.where(ok_buf, s_buf[j, g] * (1.0 / 8.0) - slope * d_buf.astype(F32), -jnp.inf)
        sn = jnp.where(ok_new, s_new[j, g] * (1.0 / 8.0) - slope * d_new.astype(F32), -jnp.inf)
        m = jnp.maximum(jnp.max(sb, axis=-1, keepdims=True), jnp.max(sn, axis=-1, keepdims=True))
        pb, pn = jnp.exp(sb - m), jnp.exp(sn - m)
        p_buf[j, g], p_new[j, g] = pb.astype(BF16), pn.astype(BF16)
        stats[j, g] = (m, jnp.sum(pb, axis=-1, keepdims=True) + jnp.sum(pn, axis=-1, keepdims=True))
    acc = {(j, g): lax.dot_general(p_buf[j, g], vb[g][j].astype(BF16), nt, preferred_element_type=F32)
           + jnp.dot(p_new[j, g], vn[j, g].astype(BF16), preferred_element_type=F32)
           for j, g in combos}
    merged = []
    for j in range(hps):
        m_all = functools.reduce(jnp.maximum, [stats[j, g][0] for g in range(n_g)])
        wts = [jnp.exp(stats[j, g][0] - m_all) for g in range(n_g)]
        num = sum(wts[g] * acc[j, g] for g in range(n_g))
        den = sum(wts[g] * stats[j, g][1] for g in range(n_g))
        merged.append(num / den)
    o_ref[...] = jnp.concatenate(merged, axis=1)

    for j, g in combos:
        wb = ATT_GROUPS[g][0]
        for src, new, dst in ((kb[g], kn[j, g], ko[g]), (vb[g], vn[j, g], vo[g])):
            rolled = pltpu.roll(src[j], wb - t_new, 1)
            hi = new.astype(BF16)
            lo = (new - hi.astype(F32)).astype(BF16)
            placed = (lax.dot_general(hi, place, tn, preferred_element_type=F32)
                      + lax.dot_general(lo, place, tn, preferred_element_type=F32))
            if wb > LANES:
                dst[j, :, 0:wb - LANES] = rolled[:, 0:wb - LANES]
            dst[j, :, wb - LANES:wb] = jnp.where(tail, placed, rolled[:, wb - LANES:wb])


def _decode_kernel(sl_ref, *refs, t_new, hps):
    n_new = 3 * len(ATT_GROUPS)
    n_c = 2 * len(ATT_GROUPS)
    new_refs, refs = refs[:n_new], refs[n_new:]
    _decode_body(sl_ref, new_refs, refs[0:n_c:2], refs[1:n_c:2], refs[n_c],
                 refs[n_c + 1::2], refs[n_c + 2::2], pl.program_id(1), t_new, hps)


def _decode_scan_kernel(sl_ref, *refs, t_new, hps, nb):
    n_new = 3 * len(ATT_GROUPS)
    n_c = 2 * len(ATT_GROUPS)
    new_refs, refs = refs[:n_new], refs[n_new:]
    caches_in = refs[0:n_c]
    cur_ref, wa_ref, pcur_ref, pwa_ref = refs[n_c:n_c + 4]
    param_refs = refs[n_c + 4:n_c + 13]
    o_ref = refs[n_c + 13]
    caches_out = refs[n_c + 14:2 * n_c + 14]
    y_ref, zt_ref, bonus_ref, z_scr, tinv_scr, lp_scr, akv_scr, rbk_scr, tok_scr = refs[2 * n_c + 14:]
    c, half = pl.program_id(0), pl.program_id(1)
    n = nb * (RW_W // LANES)

    def decode(hh):
        _decode_body(sl_ref, new_refs, caches_in[0::2], caches_in[1::2], o_ref,
                     caches_out[0::2], caches_out[1::2], hh, t_new, hps)

    @pl.when(jnp.logical_and(c == 0, half == 0))
    def _():
        z_scr[...] = jnp.zeros_like(z_scr)

    @pl.when(half == 0)
    def _():
        decode(0)
        params = tuple(ref[...] for ref in param_refs)
        row0 = lax.broadcasted_iota(jnp.int32, (CHUNK, 1), 0) == 0
        seqs = []
        for bi in range(nb):
            def prev(x, p8):
                edge = jnp.where(c == 0, 0.0, p8[7:8, :])
                return jnp.where(row0, edge, pltpu.roll(x, 1, 0))
            cur, wa = cur_ref[bi], wa_ref[bi]
            outs = _prep_math(cur, wa, prev(cur, pcur_ref[bi]), prev(wa, pwa_ref[bi]), *params)
            seqs.append(outs[:6])
            bonus_ref[bi] = outs[6]
            for j in range(6):
                tok_scr[j, bi] = outs[j]
        tinv, lp, akv, rbk = _scan_first_half(_scan_chains(seqs))
        tinv_scr[...] = jnp.stack(tinv, axis=0)
        lp_scr[...] = jnp.stack(lp, axis=0)
        akv_scr[...] = jnp.stack(akv, axis=0)
        rbk_scr[...] = jnp.stack(rbk, axis=0)

    @pl.when(half == 1)
    def _():
        decode(1)
        seqs = [tuple(tok_scr[j, bi] for j in range(6)) for bi in range(nb)]
        carried = [[scr[i] for i in range(n)] for scr in (tinv_scr, lp_scr, akv_scr, rbk_scr)]
        ys, z_news = _scan_second_half(_scan_chains(seqs), *carried, [z_scr[i] for i in range(n)])
        _store_scan(ys, z_news, y_ref, zt_ref, z_scr, nb)


def _unpair_state(zt, batch):
    zt = zt.reshape(batch, RW_W // LANES, LANES, LANES)
    s_a = zt[:, :, 0:HEAD_DIM, 0:HEAD_DIM]
    s_b = zt[:, :, HEAD_DIM:, HEAD_DIM:]
    return jnp.stack([s_a, s_b], axis=2).reshape(batch, RWKV_HEADS, HEAD_DIM, HEAD_DIM)


def _decode_attention(proj, caches, dec_batch, t_new, scan_src=None, batch=0, seq=0):
    n_g = len(ATT_GROUPS)
    hps = HPG // 2
    assert all(k.shape[1] == w for (k, _), (w, _) in zip(caches, ATT_GROUPS))

    native = [jnp.transpose(c, (0, 2, 3, 1)) for kv in caches for c in kv]
    slopes = jnp.concatenate([_alibi_slopes(g) for g in range(n_g)])
    proj3 = proj.reshape(dec_batch, t_new, P_COLS)
    hw = hps * HEAD_DIM

    def new_spec(col0):
        return pl.BlockSpec((None, t_new, hw), lambda b, h: (b, 0, col0 // hw + h))

    new_specs = [new_spec(c0 + g * ATT_W) for c0 in (C_Q, C_K, C_V) for g in range(n_g)]
    cache_specs = [pl.BlockSpec((None, hps, HEAD_DIM, c.shape[-1]), lambda b, h: (b, h, 0, 0))
                   for c in native]
    in_specs = [pl.BlockSpec(memory_space=pltpu.SMEM)] + new_specs + cache_specs
    out_specs = [pl.BlockSpec((None, t_new, hw), lambda b, h: (b, 0, h))] + cache_specs
    out_shape = ([jax.ShapeDtypeStruct((dec_batch, t_new, ATT_W), F32)]
                 + [jax.ShapeDtypeStruct(c.shape, F32) for c in native])
    new_args = [proj3] * len(new_specs)
    grid = (dec_batch, HPG // hps)
    if scan_src is None:
        outs = pl.pallas_call(
            functools.partial(_decode_kernel, t_new=t_new, hps=hps),
            grid=grid, in_specs=in_specs, out_specs=out_specs, out_shape=out_shape,
            compiler_params=_cparams(("arbitrary", "arbitrary")),
            name="decode_attn",
        )(slopes, *new_args, *native)
        scan_out = ()
    else:
        assert grid == (seq // CHUNK, 2)
        proj_p, (mu, *prep_params) = scan_src
        pv = proj_p.reshape(batch, seq, P_COLS)
        n = batch * (RW_W // LANES)
        main_w = 3 * RW_W
        cb, wb = C_SH // main_w, (C_SH + main_w) // LANES

        def halo(c, h):
            return jnp.maximum(c * (CHUNK // 8) - 1, 0)

        tok = pl.BlockSpec((batch, CHUNK, RW_W), lambda c, h: (0, c, 0))
        st = pl.BlockSpec((n, LANES, LANES), lambda c, h: (0, 0, 0))
        full = lambda a: pl.BlockSpec(a.shape, lambda c, h: (0, 0))
        params = (mu[:, 0:main_w], mu[:, main_w:], *prep_params)
        prep_specs = [pl.BlockSpec((batch, CHUNK, main_w), lambda c, h: (0, c, cb)),
                      pl.BlockSpec((batch, CHUNK, LANES), lambda c, h: (0, c, wb)),
                      pl.BlockSpec((batch, 8, main_w), lambda c, h: (0, halo(c, h), cb)),
                      pl.BlockSpec((batch, 8, LANES), lambda c, h: (0, halo(c, h), wb))]
        sq = pltpu.VMEM((n, LANES, LANES), F32)
        tok_sds = jax.ShapeDtypeStruct((batch, seq, RW_W), F32)
        outs = pl.pallas_call(
            functools.partial(_decode_scan_kernel, t_new=t_new, hps=hps, nb=batch),
            grid=grid,
            in_specs=in_specs + prep_specs + [full(a) for a in params],
            out_specs=out_specs + [tok, st, tok],
            out_shape=out_shape + [tok_sds, jax.ShapeDtypeStruct((n, LANES, LANES), F32), tok_sds],
            scratch_shapes=[sq, sq, sq, sq, pltpu.VMEM((n, LANES, 2 * LANES), F32),
                            pltpu.VMEM((6, batch, CHUNK, RW_W), F32)],
            compiler_params=_cparams(("arbitrary", "arbitrary")),
            name="decode_attn_rwkv_scan",
        )(slopes, *new_args, *native, pv, pv, pv, pv, *params)
        flat = (batch * seq, RW_W)
        scan_out = (outs[-3].reshape(flat), _unpair_state(outs[-2], batch), outs[-1].reshape(flat))
        outs = outs[:-3]
    o_a = outs[0].reshape(dec_batch * t_new, ATT_W)
    new = [jnp.transpose(c, (0, 3, 1, 2)) for c in outs[1:]]
    return (o_a, new[0::2], new[1::2]) + scan_out


def _prep_kernel(cur_ref, wa_ref, pcur_ref, pwa_ref, fcur_ref, fwa_ref, mu_ref, muwa_ref, lora_ref,
                 w0_ref, a0_ref, kk_ref, ka_ref, rk_ref, bd_ref,
                 r_o, lw_o, k_o, v_o, kkn_o, b_o, bonus_o, *, seq_len, tm):
    i = pl.program_id(0)
    row = lax.broadcasted_iota(jnp.int32, (tm, 1), 0)

    def shifted(cur, p8, first):
        rolled = pltpu.roll(cur, 1, 0)
        if seq_len >= tm:
            at_start = (i * tm) % seq_len == 0
            edge = jnp.where(at_start, first, p8[7:8, :])
            return jnp.where(row == 0, edge, rolled)
        return jnp.where(row % seq_len == 0, first, rolled)

    cur = cur_ref[...]
    wa = wa_ref[...]
    params = tuple(ref[...] for ref in (mu_ref, muwa_ref, lora_ref, w0_ref, a0_ref, kk_ref, ka_ref, rk_ref,
                                        bd_ref))
    outs = _prep_math(cur, wa, shifted(cur, pcur_ref[...], fcur_ref[...]),
                      shifted(wa, pwa_ref[...], fwa_ref[...]), *params)
    for ref, val in zip((r_o, lw_o, k_o, v_o, kkn_o, b_o, bonus_o), outs):
        ref[...] = val


def _prep_math(cur, wa, prev_cur, prev_wa, mu, muwa, lora, w0, a0, k_k, k_a, r_k, bd):
    xs = cur + (prev_cur - cur) * mu
    xwa = wa + (prev_wa - wa) * muwa
    r = xs[:, 0:RW_W]
    kr = xs[:, RW_W:2 * RW_W]
    vr = xs[:, 2 * RW_W:3 * RW_W]
    lane = lax.broadcasted_iota(jnp.int32, (1, LANES), 1)
    lin = jnp.where(lane < LORA, jnp.tanh(xwa), xwa).astype(BF16)
    lo = jnp.dot(lin, lora, preferred_element_type=F32)
    z = -(w0 + lo[:, 0:RW_W])
    softplus = jnp.maximum(z, 0.0) + jnp.log1p(jnp.exp(-jnp.abs(z)))
    lw = -jnp.exp(-softplus - 0.5)
    a = jax.nn.sigmoid(a0 + lo[:, RW_W:2 * RW_W])
    kk = kr * k_k
    k_mod = kr * (1.0 + (a - 1.0) * k_a)
    rkk = r * k_mod * r_k
    kkn, b, bonus = [], [], []
    for c in range(RW_W // LANES):
        cols = slice(c * LANES, (c + 1) * LANES)
        kc = kk[:, cols]
        kn = kc / jnp.maximum(jnp.sqrt(_segsum(kc * kc, bd)), L2_EPS)
        kkn.append(kn)
        b.append(kn * a[:, cols])
        bonus.append(_segsum(rkk[:, cols], bd) * vr[:, cols])
    cat = lambda parts: jnp.concatenate(parts, axis=1)
    return r, lw, k_mod, vr, cat(kkn), cat(b), cat(bonus)


def _rwkv_prep(proj, first, mu, lora_w, w0, a0, k_k, k_a, r_k, bd, seq_len, tm):
    n = proj.shape[0]
    main_w = 3 * RW_W
    cb, wb = C_SH // main_w, (C_SH + main_w) // LANES
    per_seq_first = seq_len >= tm
    if per_seq_first:
        f_main = first[:, None, 0:main_w]
        f_wa = first[:, None, main_w:]
        fm_spec = pl.BlockSpec((None, 1, main_w), lambda i: ((i * tm) // seq_len, 0, 0))
        fw_spec = pl.BlockSpec((None, 1, LANES), lambda i: ((i * tm) // seq_len, 0, 0))
    else:
        rep = jnp.repeat(first, seq_len, axis=0)
        f_main, f_wa = rep[:, 0:main_w], rep[:, main_w:]
        fm_spec = pl.BlockSpec((tm, main_w), lambda i: (i, 0))
        fw_spec = pl.BlockSpec((tm, LANES), lambda i: (i, 0))
    vec = lambda w: pl.BlockSpec((1, w), lambda i: (0, 0))
    out_spec = pl.BlockSpec((tm, RW_W), lambda i: (i, 0))
    out_sds = jax.ShapeDtypeStruct((n, RW_W), F32)
    return pl.pallas_call(
        functools.partial(_prep_kernel, seq_len=seq_len, tm=tm),
        grid=(n // tm,),
        in_specs=[
            pl.BlockSpec((tm, main_w), lambda i: (i, cb)),
            pl.BlockSpec((tm, LANES), lambda i: (i, wb)),
            pl.BlockSpec((8, main_w), lambda i: (jnp.maximum(i * (tm // 8) - 1, 0), cb)),
            pl.BlockSpec((8, LANES), lambda i: (jnp.maximum(i * (tm // 8) - 1, 0), wb)),
            fm_spec, fw_spec,
            vec(main_w), vec(LANES),
            pl.BlockSpec((LANES, 2 * RW_W), lambda i: (0, 0)),
            vec(RW_W), vec(RW_W), vec(RW_W), vec(RW_W), vec(RW_W),
            pl.BlockSpec((LANES, LANES), lambda i: (0, 0)),
        ],
        out_specs=[out_spec] * 7,
        out_shape=[out_sds] * 7,
        compiler_params=_cparams(("arbitrary",)),
        name="rwkv_prep",
    )(proj, proj, proj, proj, f_main, f_wa, mu[:, 0:main_w], mu[:, main_w:], lora_w,
      w0, a0, k_k, k_a, r_k, bd)


def _mm(a, b):
    return jnp.dot(a.astype(BF16), b.astype(BF16), preferred_element_type=F32)


def _mm_nt(a, b):
    return lax.dot_general(a.astype(BF16), b.astype(BF16), (((1,), (1,)), ((), ())),
                           preferred_element_type=F32)


def _mm_tn(a, b):
    return lax.dot_general(a.astype(BF16), b.astype(BF16), (((0,), (0,)), ((), ())),
                           preferred_element_type=F32)


SPLIT_STEPS = 2


def _scan_chains(seqs):
    C = CHUNK
    ti = lax.broadcasted_iota(jnp.int32, (C, C), 0)
    tj = lax.broadcasted_iota(jnp.int32, (C, C), 1)
    tri = (ti >= tj).astype(F32)
    in_a = lax.broadcasted_iota(jnp.int32, (1, LANES), 1) < HEAD_DIM

    def stack(x):
        return jnp.concatenate([jnp.where(in_a, x, 0.0), jnp.where(in_a, 0.0, x)], axis=0)

    chains = []
    for r, lw, k, v, kk, b in seqs:
        cum = jnp.dot(tri, lw, precision=lax.Precision.HIGHEST, preferred_element_type=F32)
        cum_end = cum[C - 1:C, :]
        w_in = jnp.exp(-cum)
        w_rest = jnp.exp(cum_end - cum)
        vals = (("a", jnp.exp(cum - lw) * (-kk)), ("b", b * w_in), ("k", k * w_in), ("r", r * jnp.exp(cum)),
                ("v", v), ("bh", b * w_rest), ("kh", k * w_rest))
        w_end = jnp.exp(cum_end)
        for p in range(RW_W // LANES):
            cols = slice(p * LANES, (p + 1) * LANES)
            ch = {name: stack(x[:, cols]) for name, x in vals}
            ch["w_end"] = w_end[:, cols]
            chains.append(ch)
    return chains


def _scan_masks():
    C = CHUNK
    si = lax.broadcasted_iota(jnp.int32, (2 * C, 2 * C), 0)
    sj = lax.broadcasted_iota(jnp.int32, (2 * C, 2 * C), 1)
    same = (si // C) == (sj // C)
    strict = jnp.logical_and(same, (si % C) > (sj % C))
    incl = jnp.logical_and(same, (si % C) >= (sj % C))
    return strict, incl, (si == sj).astype(F32)


def _double(prod, lp, steps):
    n = 2 * CHUNK
    for _ in range(steps):
        res = [_mm(jnp.concatenate([a, x], axis=0), x) for a, x in zip(prod, lp)]
        prod = [a + r[0:n] for a, r in zip(prod, res)]
        lp = [r[n:2 * n] for r in res]
    return prod, lp


def _scan_first_half(chains):
    C = CHUNK
    strict, incl, eye = _scan_masks()
    g = [_mm_nt(jnp.concatenate([ch["a"], ch["r"]], axis=0), jnp.concatenate([ch["b"], ch["k"]], axis=0))
         for ch in chains]
    ab = [jnp.where(strict, x[0:2 * C, 0:2 * C], 0.0) for x in g]
    ak = [jnp.where(strict, x[0:2 * C, 2 * C:4 * C], 0.0) for x in g]
    rbk = [jnp.concatenate([jnp.where(incl, x[2 * C:4 * C, 0:2 * C], 0.0),
                            jnp.where(incl, x[2 * C:4 * C, 2 * C:4 * C], 0.0)], axis=1) for x in g]
    akv = [_mm(x, ch["v"]) for x, ch in zip(ak, chains)]
    tinv, lp = _double([eye + x for x in ab], [_mm(x, x) for x in ab], SPLIT_STEPS)
    return tinv, lp, akv, rbk


def _scan_second_half(chains, tinv, lp, akv, rbk, zs):
    C = CHUNK
    eye = _scan_masks()[2]
    n = len(chains)
    levels = (C - 1).bit_length()
    tinv, lp = _double(tinv, lp, levels - 2 - SPLIT_STEPS)
    tinv = [a + _mm(a, x) for a, x in zip(tinv, lp)]
    au = [_mm(tinv[i], jnp.concatenate([chains[i]["a"], akv[i]], axis=1)) for i in range(n)]
    zeros = jnp.zeros((2 * C, LANES), F32)
    rhs = [jnp.concatenate([au[i], jnp.concatenate([zeros, chains[i]["v"]], axis=1)], axis=0)
           for i in range(n)]
    mn = [_mm_tn(jnp.concatenate([chains[i]["bh"], chains[i]["kh"]], axis=0), rhs[i])
          for i in range(n)]
    ry = [_mm(rbk[i], rhs[i]) for i in range(n)]
    yz = [_mm(jnp.concatenate([chains[i]["r"] + ry[i][:, 0:LANES],
                               mn[i][:, 0:LANES] + eye * chains[i]["w_end"]], axis=0), zs[i])
          for i in range(n)]
    z_news = [yz[i][2 * C:2 * C + LANES] + mn[i][:, LANES:2 * LANES] for i in range(n)]
    y_st = [yz[i][0:2 * C] + ry[i][:, LANES:2 * LANES] for i in range(n)]
    return [y[0:C] + y[C:2 * C] for y in y_st], z_news


def _store_scan(ys, z_news, y_ref, zt_ref, z_scr, nb):
    n_pairs = RW_W // LANES
    y_ref[...] = jnp.stack([jnp.concatenate(ys[bi * n_pairs:(bi + 1) * n_pairs], axis=1)
                            for bi in range(nb)], axis=0)
    z_scr[...] = jnp.stack(z_news, axis=0)
    zt_ref[...] = jnp.stack([z.T for z in z_news], axis=0)


def _scan_kernel(r_ref, lw_ref, k_ref, v_ref, kk_ref, b_ref, y_ref, zt_ref, z_scr, *, nb):
    @pl.when(pl.program_id(0) == 0)
    def _():
        z_scr[...] = jnp.zeros_like(z_scr)

    chains = _scan_chains([tuple(ref[bi] for ref in (r_ref, lw_ref, k_ref, v_ref, kk_ref, b_ref))
                           for bi in range(nb)])
    ys, z_news = _scan_second_half(chains, *_scan_first_half(chains),
                                   [z_scr[i] for i in range(len(chains))])
    _store_scan(ys, z_news, y_ref, zt_ref, z_scr, nb)


def _rwkv_scan(prep, batch, seq):
    assert seq % CHUNK == 0
    n_pairs = RW_W // LANES
    tok = pl.BlockSpec((batch, CHUNK, RW_W), lambda c: (0, c, 0))
    st = pl.BlockSpec((batch * n_pairs, LANES, LANES), lambda c: (0, 0, 0))
    y, zt = pl.pallas_call(
        functools.partial(_scan_kernel, nb=batch),
        grid=(seq // CHUNK,),
        in_specs=[tok] * 6,
        out_specs=[tok, st],
        out_shape=[jax.ShapeDtypeStruct((batch, seq, RW_W), F32),
                   jax.ShapeDtypeStruct((batch * n_pairs, LANES, LANES), F32)],
        scratch_shapes=[pltpu.VMEM((batch * n_pairs, LANES, LANES), F32)],
        compiler_params=_cparams(("arbitrary",)),
        name="rwkv_scan",
    )(*[x.reshape(batch, seq, RW_W) for x in prep])
    return y.reshape(batch * seq, RW_W), _unpair_state(zt, batch)


def _seq_scan_kernel(r_ref, lw_ref, k_ref, v_ref, kk_ref, b_ref, s_ref, y_ref, so_ref, t_scr, y_scr,
                     *, t_new, nb):
    for j, ref in enumerate((r_ref, lw_ref, k_ref, v_ref, kk_ref, b_ref)):
        for t in range(t_new):
            x = ref[pl.ds(t, nb, stride=t_new), :].T
            t_scr[j, t] = jnp.exp(x) if j == 1 else x

    for hd in range(LANES // HEAD_DIM):
        lo = hd * HEAD_DIM
        rows = pl.ds(lo, HEAD_DIM)

        def body(vi, carry, hd=hd, lo=lo, rows=rows):
            sv = s_ref[hd, vi]
            for t in range(t_new):
                sa = jnp.sum(sv * t_scr[4, t, rows, :], axis=0, keepdims=True)
                sv = (sv * t_scr[1, t, rows, :] - sa * t_scr[5, t, rows, :]
                      + t_scr[3, t, pl.ds(lo + vi, 1), :] * t_scr[2, t, rows, :])
                y_scr[t, pl.ds(lo + vi, 1), :] = jnp.sum(sv * t_scr[0, t, rows, :], axis=0, keepdims=True)
            so_ref[hd, vi] = sv
            return carry

        lax.fori_loop(0, HEAD_DIM, body, 0)

    for t in range(t_new):
        y_ref[pl.ds(t, nb, stride=t_new), :] = y_scr[t].T


def _seq_scan(prep, state, dec_batch, t_new):
    hpp = LANES // HEAD_DIM
    rows = dec_batch * t_new
    vec = pl.BlockSpec((rows, LANES), lambda p: (0, p))
    st = pl.BlockSpec((hpp, HEAD_DIM, HEAD_DIM, dec_batch), lambda p: (p, 0, 0, 0))
    s_native = jnp.transpose(state, (1, 2, 3, 0))
    y, s_new = pl.pallas_call(
        functools.partial(_seq_scan_kernel, t_new=t_new, nb=dec_batch),
        grid=(RWKV_HEADS // hpp,),
        in_specs=[vec] * 6 + [st],
        out_specs=[vec, st],
        out_shape=[jax.ShapeDtypeStruct((rows, RW_W), F32), jax.ShapeDtypeStruct(s_native.shape, F32)],
        scratch_shapes=[pltpu.VMEM((6, t_new, LANES, dec_batch), F32),
                        pltpu.VMEM((t_new, LANES, dec_batch), F32)],
        compiler_params=_cparams(("arbitrary",)),
        name="rwkv_seq_scan",
    )(*prep, s_native)
    return y, jnp.transpose(s_new, (3, 0, 1, 2))


def _final_kernel(oa_ref, y_ref, bonus_ref, gates_ref, x_ref, lnw_ref, lnb_ref, pa_ref, pb_ref, wo_ref, bd_ref,
                  o_ref):
    o_a = oa_ref[...]
    gates = gates_ref[...]
    z_a = gates[:, 0:ATT_W]
    z_b = gates[:, ATT_W:ATT_W + RW_W]
    g_a = gates[:, ATT_W + RW_W:ATT_W + RW_W + 1024]
    g_b = gates[:, ATT_W + RW_W + 1024:]
    y = y_ref[...]
    bd = bd_ref[...]
    cols_out = []
    for c in range(RW_W // LANES):
        cols = slice(c * LANES, (c + 1) * LANES)
        yc = y[:, cols]
        mu = _segsum(yc, bd) * (1.0 / HEAD_DIM)
        d = yc - mu
        var = _segsum(d * d, bd) * (1.0 / HEAD_DIM)
        cols_out.append(d * lax.rsqrt(var + GN_EPS))
    yn = jnp.concatenate(cols_out, axis=1)
    o_b = yn * lnw_ref[...] + lnb_ref[...] + bonus_ref[...]
    h_a = jnp.dot((o_a * jax.nn.silu(z_a)).astype(BF16), pa_ref[...], preferred_element_type=F32)
    h_b = jnp.dot((o_b * jax.nn.silu(z_b)).astype(BF16), pb_ref[...], preferred_element_type=F32)
    mixed = jax.nn.sigmoid(g_a) * h_a + jax.nn.sigmoid(g_b) * h_b
    o_ref[...] = x_ref[...] + jnp.dot(mixed.astype(BF16), wo_ref[...], preferred_element_type=F32)


def _final(o_a, y_raw, bonus, proj, x, ln_w, ln_b, p_a, p_b, w_out, bd, tm):
    n, d = x.shape
    tm = min(tm, n)
    gates_w = C_Q - C_GATES
    row = lambda w: pl.BlockSpec((tm, w), lambda i: (i, 0))
    full = lambda a: pl.BlockSpec(a.shape, lambda i: (0, 0))
    return pl.pallas_call(
        _final_kernel,
        grid=(n // tm,),
        in_specs=[row(ATT_W), row(RW_W), row(RW_W), row(gates_w), row(d),
                  full(ln_w), full(ln_b), full(p_a), full(p_b), full(w_out), full(bd)],
        out_specs=row(d),
        out_shape=jax.ShapeDtypeStruct((n, d), F32),
        compiler_params=_cparams(("arbitrary",)),
        name="gated_out",
    )(o_a, y_raw, bonus, proj, x, ln_w, ln_b, p_a, p_b, w_out, bd)


def _permute_w_in(w_in):
    q, k, v, sh, gates = (w_in[:, 0:QKV], w_in[:, QKV:2 * QKV], w_in[:, 2 * QKV:3 * QKV],
                          w_in[:, 3 * QKV:3 * QKV + SHIFT_COLS], w_in[:, 3 * QKV + SHIFT_COLS:])
    pad = jnp.zeros((w_in.shape[0], P_COLS - IN_COLS), w_in.dtype)
    return jnp.concatenate([gates, q, k, v, sh, pad], axis=1).astype(BF16)


def _layer(h_p, h_s, batch, seq, dec_batch, dec_seq, shift_s, state_s, caches, lw):
    (norm_g, w_perm, shift_mu, qg, kg, lora_w, w0, a0, k_k, k_a, r_k, ln_w, ln_b, p_a, p_b, w_out,
     bd) = lw
    n_s = dec_batch * dec_seq
    prep_args = (shift_mu, lora_w, w0, a0, k_k, k_a, r_k, bd)
    out_args = (ln_w, ln_b, p_a, p_b, w_out, bd)

    proj_p = _inproj(h_p, norm_g, w_perm, qg, kg, bd, tm=2048)
    o_a_p = _band_attention(proj_p, batch, seq, tile=2048)
    pv = proj_p.reshape(batch, seq, P_COLS)
    st_p = []
    for g, (window, _) in enumerate(ATT_GROUPS):
        keep = min(window, seq)
        for c0 in (C_K, C_V):
            st_p.append(pv[:, seq - keep:, c0 + g * ATT_W:c0 + (g + 1) * ATT_W]
                        .reshape(batch, keep, HPG, HEAD_DIM))

    proj_s = _inproj(h_s, norm_g, w_perm, qg, kg, bd, tm=min(1024, n_s))
    prep_s = _rwkv_prep(proj_s, shift_s, *prep_args, seq_len=dec_seq, tm=min(512, n_s))
    if seq % CHUNK == 0 and dec_batch == seq // CHUNK:
        o_a, new_k, new_v, y_p, state_p, bonus_p = _decode_attention(
            proj_s, caches, dec_batch, dec_seq, scan_src=(proj_p, prep_args), batch=batch, seq=seq)
    else:
        o_a, new_k, new_v = _decode_attention(proj_s, caches, dec_batch, dec_seq)
        prep_p = _rwkv_prep(proj_p, jnp.zeros((batch, SHIFT_COLS), F32), *prep_args, seq_len=seq, tm=512)
        y_p, state_p = _rwkv_scan(prep_p[:6], batch, seq)
        bonus_p = prep_p[6]
    y_s, state_s_new = _seq_scan(prep_s[:6], state_s, dec_batch, dec_seq)

    out_p = _final(o_a_p, y_p, bonus_p, proj_p, h_p, *out_args, tm=512)
    out_s = _final(o_a, y_s, prep_s[6], proj_s, h_s, *out_args, tm=512)
    st_p += [state_p, pv[:, seq - 1, C_SH:C_SH + SHIFT_COLS]]
    st_s = [c for kv in zip(new_k, new_v) for c in kv]
    st_s += [state_s_new, proj_s.reshape(dec_batch, dec_seq, P_COLS)[:, dec_seq - 1, C_SH:C_SH + SHIFT_COLS]]
    return out_p, out_s, st_p, st_s


def kernel(x_prompt, x_sample, cache_k_g0, cache_v_g0, cache_k_g1, cache_v_g1, cache_k_g2, cache_v_g2, state_rwkv, state_shift, norm_g, w_in, shift_mu, q_norm_g, k_norm_g, w0, w2, a0, a2, k_k, k_a, r_k, ln_x_w, ln_x_b, p_a, p_b, w_out):
    depth = norm_g.shape[0]
    batch, seq, d_model = x_prompt.shape
    dec_batch, dec_seq, _ = x_sample.shape
    bd = _bd_ones()
    h_p = x_prompt.reshape(batch * seq, d_model)
    h_s = x_sample.reshape(dec_batch * dec_seq, d_model)
    prompt_states, sample_states = [], []
    for layer in range(depth):
        zero = jnp.zeros((LORA, RW_W), F32)
        lora_w = jnp.concatenate([jnp.concatenate([w2[layer], zero], axis=1),
                                  jnp.concatenate([zero, a2[layer]], axis=1)], axis=0).astype(BF16)
        row = lambda a: a.reshape(1, -1)
        lw = (row(norm_g[layer]), _permute_w_in(w_in[layer]), row(shift_mu[layer]),
              jnp.tile(row(q_norm_g[layer]), (1, LANES // HEAD_DIM)),
              jnp.tile(row(k_norm_g[layer]), (1, LANES // HEAD_DIM)),
              lora_w, row(w0[layer]), row(a0[layer]), row(k_k[layer]), row(k_a[layer]),
              row(r_k[layer]), row(ln_x_w[layer]), row(ln_x_b[layer]),
              p_a[layer].astype(BF16), p_b[layer].astype(BF16), w_out[layer].astype(BF16), bd)
        caches = [(cache_k_g0[layer], cache_v_g0[layer]), (cache_k_g1[layer], cache_v_g1[layer]),
                  (cache_k_g2[layer], cache_v_g2[layer])]
        h_p, h_s, st_p, st_s = _layer(h_p, h_s, batch, seq, dec_batch, dec_seq, state_shift[layer],
                                      state_rwkv[layer], caches, lw)
        prompt_states.append(st_p)
        sample_states.append(st_s)
    ps = [jnp.stack(t) for t in zip(*prompt_states)]
    ss = [jnp.stack(t) for t in zip(*sample_states)]
    return (h_p.reshape(batch, seq, d_model), h_s.reshape(dec_batch, dec_seq, d_model),
            *ps, *ss)
```
